```python
import jax
import jax.numpy as jnp
from jax import lax
import numpy as np

D_MODEL = 4096
BATCH = 2
SEQ = 4096
DEPTH = 2

CTX_LEN = 256
GRID_W = 64
RMS_EPS = 1e-6

RWKV_HEAD = 64
RWKV_DIM = D_MODEL // 2
RWKV_HEADS = RWKV_DIM // RWKV_HEAD
LORA_W = max(32, int(round(1.8 * RWKV_DIM ** 0.5 / 32)) * 32)
LORA_A = LORA_W
LORA_G = max(32, int(round(0.6 * RWKV_DIM ** 0.8 / 32)) * 32)
GN_EPS = 64e-5
MLSTM_HEADS = 8
MLSTM_DIM = D_MODEL // 2
MLSTM_DV = MLSTM_DIM // MLSTM_HEADS
MLSTM_DQK = MLSTM_DV // 2
MLSTM_CHUNK = 64
RWKV_COLS = (RWKV_DIM, RWKV_DIM, RWKV_DIM, LORA_W, LORA_A, LORA_G)
MLSTM_COLS = (MLSTM_HEADS * MLSTM_DQK, MLSTM_HEADS * MLSTM_DQK, MLSTM_DIM, MLSTM_DIM, 4 * MLSTM_HEADS)
RWKV_WIDTH = sum(RWKV_COLS)
EVEN_COLS = RWKV_WIDTH + sum(MLSTM_COLS)

ATT_HEAD = 128
ATT_HEADS = D_MODEL // ATT_HEAD
KV_HEADS = ATT_HEADS // 4
GROUP = ATT_HEADS // KV_HEADS
KV_DIM = KV_HEADS * ATT_HEAD
Q_BLOCK = 128
ROPE_THETA = 10000.0
ROPE_PAIRS = ATT_HEAD // 4

N_EXPERTS = 16
N_GROUPS = 4
EXPERTS_PER_GROUP = N_EXPERTS // N_GROUPS
TOP_K = 2
D_EXPERT = D_MODEL // 4
MOE_BLOCK = 128

N_EVEN = (DEPTH + 1) // 2
N_ODD = DEPTH // 2

kernel_name = 'hybrid_rwkv7_mlstm_gqa_moe_dit'


def _rms_norm(x, g, eps=RMS_EPS):
    xf = x.astype(jnp.float32)
    y = xf * lax.rsqrt(jnp.mean(xf * xf, axis=-1, keepdims=True) + eps)
    return (y * g.astype(jnp.float32)).astype(x.dtype)


def _modulate(h, shift, scale):
    return h * (1.0 + scale) + shift


def _heads(t, n):
    return t.reshape(*t.shape[:-1], n, t.shape[-1] // n)


def _split(z, sizes):
    return jnp.split(z, [int(i) for i in np.cumsum(sizes)[:-1]], axis=-1)


def _centred_shift(z):
    zp = jnp.pad(z, ((0, 0), (1, 1), (0, 0)))
    return 0.5 * (zp[:, :-2] + zp[:, 2:])


def _rwkv_dir_inputs(r, k, v, w_down, a_down, w0, w_up, a0, a_up, k_k, k_a, r_k):
    f32 = jnp.float32
    w_log = -jax.nn.softplus(-(w0 + jnp.tanh(w_down) @ w_up).astype(f32)) - 0.5
    decay = jnp.exp(-jnp.exp(w_log))
    a = jax.nn.sigmoid((a0 + a_down @ a_up).astype(f32))
    kk = _heads((k * k_k).astype(f32), RWKV_HEADS)
    kk = kk * lax.rsqrt(jnp.maximum(jnp.sum(kk * kk, -1, keepdims=True), 1e-24))
    k_mod = k.astype(f32) * (1.0 + (a - 1.0) * k_a.astype(f32))
    r_h, k_h, v_h, a_h, w_h = (_heads(t, RWKV_HEADS) for t in (r.astype(f32), k_mod, v.astype(f32), a, decay))
    bonus = jnp.sum(r_h * k_h * r_k.reshape(RWKV_HEADS, RWKV_HEAD).astype(f32), -1, keepdims=True) * v_h
    return (r_h, w_h, k_h, v_h, -kk, kk * a_h), bonus


def _rwkv7_scan(s0, inputs, reverse):
    def step(s, inp):
        r, w, k, v, a, b = inp
        sa = jnp.einsum('bhvk,bhk->bhv', s, a)
        s = s * w[:, :, None, :] + sa[..., None] * b[:, :, None, :] + v[..., None] * k[:, :, None, :]
        return s, jnp.einsum('bhvk,bhk->bhv', s, r)
    xs = tuple(jnp.swapaxes(t, 0, 1) for t in inputs)
    s, ys = lax.scan(step, s0, xs, reverse=reverse)
    return s, jnp.swapaxes(ys, 0, 1)


def _head_layer_norm(y, w, b):
    mean = jnp.mean(y, -1, keepdims=True)
    var = jnp.mean(jnp.square(y - mean), -1, keepdims=True)
    y = (y - mean) * lax.rsqrt(var + GN_EPS)
    return y.reshape(*y.shape[:-2], -1) * w.astype(jnp.float32) + b.astype(jnp.float32)


def _rwkv_mixer(zc, zl, mu, w0, w_up, a0, a_up, g_up, k_k, k_a, r_k, gn_w, gn_b, ctx_out):
    zc = zc + mu * (_centred_shift(zc) - zc)
    zl = zl + mu * (_centred_shift(zl) - zl)
    pc = _split(zc, RWKV_COLS)
    pl = _split(zl, RWKV_COLS)
    batch = zl.shape[0]
    y_c = y_l = bon_c = bon_l = 0.0
    for d in range(2):
        in_c, b_c = _rwkv_dir_inputs(*pc[:5], w0[d], w_up[d], a0[d], a_up[d], k_k, k_a, r_k)
        in_l, b_l = _rwkv_dir_inputs(*pl[:5], w0[d], w_up[d], a0[d], a_up[d], k_k, k_a, r_k)
        s0 = jnp.zeros((batch, RWKV_HEADS, RWKV_HEAD, RWKV_HEAD), jnp.float32)
        s_ctx, yc_d = _rwkv7_scan(s0, in_c, d == 1)
        _, yl_d = _rwkv7_scan(s_ctx, in_l, d == 1)
        y_c, y_l = y_c + yc_d, y_l + yl_d
        bon_c, bon_l = bon_c + b_c, bon_l + b_l

    def finish(y, bonus, g_down):
        y = _head_layer_norm(y, gn_w, gn_b) + bonus.reshape(*bonus.shape[:-2], -1)
        return y.astype(g_down.dtype) * (jax.nn.sigmoid(g_down) @ g_up)

    out_l = finish(y_l, bon_l, pl[5])
    out_c = finish(y_c, bon_c, pc[5]) if ctx_out else None
    return out_c, out_l


def _mlstm_chunkwise(q, k, v, i_pre, f_pre, state, reverse):
    f32 = jnp.float32
    q, k, v, i_pre = (t.astype(f32) for t in (q, k, v, i_pre))
    log_f = jax.nn.log_sigmoid(f_pre.astype(f32))
    if reverse:
        q, k, v, i_pre, log_f = (jnp.flip(t, 1) for t in (q, k, v, i_pre, log_f))
    batch, n_tok = q.shape[:2]
    n_chunks = n_tok // MLSTM_CHUNK

    def to_chunks(t):
        return jnp.moveaxis(t.reshape(batch, n_chunks, MLSTM_CHUNK, *t.shape[2:]), 1, 0)

    tril = jnp.tril(jnp.ones((MLSTM_CHUNK, MLSTM_CHUNK), bool))

    def step(carry, inp):
        c_st, n_st, m_st = carry
        qc, kc, vc, ic, fc = inp
        b = jnp.swapaxes(jnp.cumsum(fc, axis=1), 1, 2)
        ig = jnp.swapaxes(ic, 1, 2)
        dmat = jnp.where(tril, b[..., :, None] - b[..., None, :] + ig[..., None, :], -jnp.inf)
        inter = b + m_st[..., None]
        m_t = jnp.maximum(inter, jnp.max(dmat, -1))
        w_inter = jnp.exp(inter - m_t)
        s = jnp.einsum('blhd,bshd->bhls', qc, kc) * jnp.exp(dmat - m_t[..., None])
        num = w_inter[..., None] * jnp.einsum('blhd,bhdv->bhlv', qc, c_st) + jnp.einsum('bhls,bshv->bhlv', s, vc)
        den = w_inter * jnp.einsum('blhd,bhd->bhl', qc, n_st) + jnp.sum(s, -1)
        h = num / jnp.maximum(jnp.abs(den), jnp.exp(-m_t))[..., None]
        b_last = b[..., -1]
        g = b_last[..., None] - b + ig
        m_new = jnp.maximum(b_last + m_st, jnp.max(g, -1))
        carry_decay = jnp.exp(b_last + m_st - m_new)
        wk = jnp.exp(g - m_new[..., None])
        c_st = carry_decay[..., None, None] * c_st + jnp.einsum('bhs,bshd,bshv->bhdv', wk, kc, vc)
        n_st = carry_decay[..., None] * n_st + jnp.einsum('bhs,bshd->bhd', wk, kc)
        return (c_st, n_st, m_new), jnp.swapaxes(h, 1, 2)

    state, h = lax.scan(step, state, tuple(to_chunks(t) for t in (q, k, v, i_pre, log_f)))
    h = jnp.moveaxis(h, 0, 1).reshape(batch, n_tok, MLSTM_HEADS, MLSTM_DV)
    if reverse:
        h = jnp.flip(h, 1)
    return state, h


def _mlstm_mixer(zc, zl, gate_b, norm_w, ctx_out):
    def prep(z):
        q, k, v, o, g = _split(z, MLSTM_COLS)
        gates = g.reshape(*g.shape[:-1], 4, MLSTM_HEADS) + gate_b
        return _heads(q, MLSTM_HEADS) * MLSTM_DQK ** -0.5, _heads(k, MLSTM_HEADS), _heads(v, MLSTM_HEADS), o, gates

    qc, kc, vc, oc, gc = prep(zc)
    ql, kl, vl, ol, gl = prep(zl)
    batch = zl.shape[0]
    h_c = h_l = 0.0
    for d in range(2):
        st0 = (jnp.zeros((batch, MLSTM_HEADS, MLSTM_DQK, MLSTM_DV), jnp.float32),
               jnp.zeros((batch, MLSTM_HEADS, MLSTM_DQK), jnp.float32),
               jnp.zeros((batch, MLSTM_HEADS), jnp.float32))
        st_ctx, hc_d = _mlstm_chunkwise(qc, kc, vc, gc[..., 2 * d, :], gc[..., 2 * d + 1, :], st0, d == 1)
        _, hl_d = _mlstm_chunkwise(ql, kl, vl, gl[..., 2 * d, :], gl[..., 2 * d + 1, :], st_ctx, d == 1)
        h_c, h_l = h_c + hc_d, h_l + hl_d

    def finish(h, o):
        hn = _rms_norm(h, norm_w.reshape(MLSTM_HEADS, MLSTM_DV)).reshape(o.shape).astype(o.dtype)
        return jax.nn.sigmoid(o) * hn

    out_l = finish(h_l, ol)
    out_c = finish(h_c, oc) if ctx_out else None
    return out_c, out_l


def _even_mixer(hc, hl, w_in, w_out, mu, w0, w_up, a0, a_up, g_up, k_k, k_a, r_k, gn_w, gn_b,
                gate_b, norm_w, ctx_out):
    zc, zl = hc @ w_in, hl @ w_in
    rc, rl = _rwkv_mixer(zc[..., :RWKV_WIDTH], zl[..., :RWKV_WIDTH], mu, w0, w_up, a0, a_up, g_up,
                         k_k, k_a, r_k, gn_w, gn_b, ctx_out)
    mc, ml = _mlstm_mixer(zc[..., RWKV_WIDTH:], zl[..., RWKV_WIDTH:], gate_b, norm_w, ctx_out)
    yl = jnp.concatenate([rl, ml], -1) @ w_out
    yc = jnp.concatenate([rc, mc], -1) @ w_out if ctx_out else None
    return yc, yl


def _rope_tables(n_tok):
    rows = n_tok // GRID_W
    row = jnp.repeat(jnp.arange(rows), GRID_W).astype(jnp.float32)
    col = jnp.tile(jnp.arange(GRID_W), rows).astype(jnp.float32)
    inv = ROPE_THETA ** (-jnp.arange(ROPE_PAIRS, dtype=jnp.float32) / ROPE_PAIRS)
    ang = jnp.concatenate([row[:, None] * inv, col[:, None] * inv], -1)
    return jnp.cos(ang), jnp.sin(ang)


def _rotary(x, cos, sin):
    half = x.shape[-1] // 2
    x1, x2 = x[..., :half], x[..., half:]
    c, s = cos[None, :, None, :], sin[None, :, None, :]
    return jnp.concatenate([x1 * c - x2 * s, x1 * s + x2 * c], -1).astype(x.dtype)


def _block_attention(q, k, v):
    batch, n_tok = q.shape[:2]
    n_blocks = n_tok // Q_BLOCK
    qb = jnp.moveaxis(q.reshape(batch, n_blocks, Q_BLOCK, KV_HEADS, GROUP, ATT_HEAD), 1, 0)
    scale = ATT_HEAD ** -0.5

    def one_block(qi):
        s = jnp.einsum('bqkgd,bskd->bkgqs', qi, k, preferred_element_type=jnp.float32) * scale
        p = jax.nn.softmax(s, axis=-1).astype(v.dtype)
        return jnp.einsum('bkgqs,bskd->bqkgd', p, v)

    o = lax.map(one_block, qb)
    return jnp.moveaxis(o, 0, 1).reshape(batch, n_tok, ATT_HEADS * ATT_HEAD)


def _attention_mixer(hc, hl, w_qkv, q_norm, k_norm, w_o, ctx_out):
    w_q, w_kv = w_qkv[:, :D_MODEL], w_qkv[:, D_MODEL:]

    def queries(h):
        return _rms_norm(_heads(h @ w_q, ATT_HEADS), q_norm)

    def keys_values(h):
        k, v = jnp.split(h @ w_kv, 2, axis=-1)
        return _rms_norm(_heads(k, KV_HEADS), k_norm), _heads(v, KV_HEADS)

    cos, sin = _rope_tables(hl.shape[1])
    k_lat, v_lat = keys_values(hl)
    k_ctx, v_ctx = keys_values(hc)
    q_lat = _rotary(queries(hl), cos, sin)
    k_lat = _rotary(k_lat, cos, sin)
    keys = jnp.concatenate([k_ctx, k_lat], 1)
    vals = jnp.concatenate([v_ctx, v_lat], 1)
    yl = _block_attention(q_lat, keys, vals) @ w_o
    yc = _block_attention(queries(hc), k_ctx, v_ctx) @ w_o if ctx_out else None
    return yc, yl


def _moe(h, router_w, router_bias, w1, w3, w2):
    n_tok = h.shape[0]
    aff = jax.nn.sigmoid((h @ router_w).astype(jnp.float32))
    sel = aff + router_bias.astype(jnp.float32)
    grp_score = jnp.sum(lax.top_k(sel.reshape(n_tok, N_GROUPS, EXPERTS_PER_GROUP), 2)[0], -1)
    best = jnp.argmax(grp_score, -1)
    in_group = (jnp.arange(N_EXPERTS) // EXPERTS_PER_GROUP)[None, :] == best[:, None]
    _, idx = lax.top_k(jnp.where(in_group, sel, -jnp.inf), TOP_K)
    gate = jnp.take_along_axis(aff, idx, -1)
    gate = gate / jnp.sum(gate, -1, keepdims=True)
    n_assign = n_tok * TOP_K
    expert = idx.reshape(-1)
    token = jnp.repeat(jnp.arange(n_tok), TOP_K)
    order = jnp.argsort(expert)
    e_s, t_s, g_s = expert[order], token[order], gate.reshape(-1)[order]
    counts = jnp.bincount(expert, length=N_EXPERTS)
    padded = (counts + MOE_BLOCK - 1) // MOE_BLOCK * MOE_BLOCK
    pad_end = jnp.cumsum(padded)
    start = jnp.cumsum(counts) - counts
    dest = (pad_end - padded)[e_s] + jnp.arange(n_assign) - start[e_s]
    n_blocks = -(-n_assign // MOE_BLOCK) + N_EXPERTS
    xs = jnp.zeros((n_blocks * MOE_BLOCK, h.shape[1]), h.dtype).at[dest].set(h[t_s])
    block_expert = jnp.minimum(jnp.searchsorted(pad_end, jnp.arange(n_blocks) * MOE_BLOCK, side='right'),
                               N_EXPERTS - 1)

    def expert_block(args):
        xb, e = args
        return (jax.nn.silu(xb @ w1[e]) * (xb @ w3[e])) @ w2[e]

    ys = lax.map(expert_block, (xs.reshape(n_blocks, MOE_BLOCK, -1), block_expert))
    ys = ys.reshape(n_blocks * MOE_BLOCK, -1)
    return jnp.zeros_like(h).at[t_s].add(ys[dest] * g_s[:, None].astype(h.dtype))


def setup_inputs(seed: int = 0) -> dict:
    key = jax.random.key(seed)
    ks = iter(jax.random.split(key, 40))
    f32 = jnp.float32
    d = D_MODEL

    def nrm(shape, scale):
        return jax.random.normal(next(ks), shape, f32) * scale

    def gain(shape):
        return 1.0 + nrm(shape, 0.05)

    return {
        'x': nrm((BATCH, SEQ, d), 1.0),
        'c': nrm((BATCH, d), 1.0),
        'ctx': nrm((BATCH, CTX_LEN, d), 1.0),
        'c_ctx': nrm((d,), 1.0),
        'mod_w': nrm((DEPTH, d, 6 * d), 0.5 * d ** -0.5),
        'mod_b': nrm((DEPTH, 6 * d), 0.02),
        'norm_mix': gain((DEPTH, d)),
        'norm_ffn': gain((DEPTH, d)),
        'norm_final': gain((d,)),
        'router_w': nrm((d, N_EXPERTS), d ** -0.5),
        'router_bias': nrm((N_EXPERTS,), 0.01),
        'exp_w1': nrm((DEPTH, N_EXPERTS, d, D_EXPERT), d ** -0.5),
        'exp_w3': nrm((DEPTH, N_EXPERTS, d, D_EXPERT), d ** -0.5),
        'exp_w2': nrm((DEPTH, N_EXPERTS, D_EXPERT, d), D_EXPERT ** -0.5),
        'ev_w_in': nrm((N_EVEN, d, EVEN_COLS), d ** -0.5),
        'ev_w_out': nrm((N_EVEN, RWKV_DIM + MLSTM_DIM, d), (RWKV_DIM + MLSTM_DIM) ** -0.5),
        'rw_mu': jax.random.uniform(next(ks), (N_EVEN, RWKV_WIDTH), f32),
        'rw_w0': jax.random.uniform(next(ks), (N_EVEN, 2, RWKV_DIM), f32, -6.0, 1.0),
        'rw_w_up': nrm((N_EVEN, 2, LORA_W, RWKV_DIM), LORA_W ** -0.5),
        'rw_a0': nrm((N_EVEN, 2, RWKV_DIM), 0.1),
        'rw_a_up': nrm((N_EVEN, 2, LORA_A, RWKV_DIM), LORA_A ** -0.5),
        'rw_g_up': nrm((N_EVEN, LORA_G, RWKV_DIM), LORA_G ** -0.5),
        'rw_k_k': 0.85 + nrm((N_EVEN, RWKV_DIM), 0.05),
        'rw_k_a': gain((N_EVEN, RWKV_DIM)),
        'rw_r_k': nrm((N_EVEN, RWKV_DIM), 0.1),
        'rw_gn_w': gain((N_EVEN, RWKV_DIM)),
        'rw_gn_b': nrm((N_EVEN, RWKV_DIM), 0.02),
        'ml_gate_b': jnp.array([-1.0, 4.5, -1.0, 4.5], f32)[None, :, None] + nrm((N_EVEN, 4, MLSTM_HEADS), 0.5),
        'ml_norm': gain((N_EVEN, MLSTM_DIM)),
        'at_w_qkv': nrm((N_ODD, d, D_MODEL + 2 * KV_DIM), d ** -0.5),
        'at_q_norm': gain((N_ODD, ATT_HEAD)),
        'at_k_norm': gain((N_ODD, ATT_HEAD)),
        'at_w_o': nrm((N_ODD, D_MODEL, d), D_MODEL ** -0.5),
    }


def reference(x, c, ctx, c_ctx, mod_w, mod_b, norm_mix, norm_ffn, norm_final, router_w, router_bias,
              exp_w1, exp_w3, exp_w2, ev_w_in, ev_w_out, rw_mu, rw_w0, rw_w_up, rw_a0, rw_a_up, rw_g_up,
              rw_k_k, rw_k_a, rw_r_k, rw_gn_w, rw_gn_b, ml_gate_b, ml_norm, at_w_qkv, at_q_norm, at_k_norm,
              at_w_o):
    batch, n_tok, d = x.shape
    n_ctx = ctx.shape[1]
    xl, xc = x, ctx
    cond_l = jax.nn.silu(c)[:, None, :]
    cond_c = jax.nn.silu(c_ctx)[None, None, :]
    for layer in range(DEPTH):
        ctx_out = layer < DEPTH - 1
        mod_l = jnp.split(cond_l @ mod_w[layer] + mod_b[layer], 6, axis=-1)
        mod_c = jnp.split(cond_c @ mod_w[layer] + mod_b[layer], 6, axis=-1)
        hl = _modulate(_rms_norm(xl, norm_mix[layer]), mod_l[0], mod_l[1])
        hc = _modulate(_rms_norm(xc, norm_mix[layer]), mod_c[0], mod_c[1])
        j = layer // 2
        if layer % 2 == 0:
            yc, yl = _even_mixer(hc, hl, ev_w_in[j], ev_w_out[j], rw_mu[j], rw_w0[j], rw_w_up[j], rw_a0[j],
                                 rw_a_up[j], rw_g_up[j], rw_k_k[j], rw_k_a[j], rw_r_k[j], rw_gn_w[j],
                                 rw_gn_b[j], ml_gate_b[j], ml_norm[j], ctx_out)
        else:
            yc, yl = _attention_mixer(hc, hl, at_w_qkv[j], at_q_norm[j], at_k_norm[j], at_w_o[j], ctx_out)
        xl = xl + mod_l[2] * yl
        if ctx_out:
            xc = xc + mod_c[2] * yc
        hl = _modulate(_rms_norm(xl, norm_ffn[layer]), mod_l[3], mod_l[4]).reshape(-1, d)
        if ctx_out:
            hc = _modulate(_rms_norm(xc, norm_ffn[layer]), mod_c[3], mod_c[4]).reshape(-1, d)
            tokens = jnp.concatenate([hl, hc], 0)
        else:
            tokens = hl
        f = _moe(tokens, router_w, router_bias, exp_w1[layer], exp_w3[layer], exp_w2[layer])
        xl = xl + mod_l[5] * f[:batch * n_tok].reshape(batch, n_tok, d)
        if ctx_out:
            xc = xc + mod_c[5] * f[batch * n_tok:].reshape(batch, n_ctx, d)
    return _rms_norm(xl, norm_final)
```

```python
import functools

import jax
import jax.numpy as jnp
import numpy as np
from jax import lax
from jax.experimental import pallas as pl
from jax.experimental.pallas import tpu as pltpu

F32 = jnp.float32
BF16 = jnp.bfloat16
HIGHEST = lax.Precision.HIGHEST

GRID_W = 64
RMS_EPS = 1e-6
GN_EPS = 64e-5
RWKV_HEAD = 64
CHUNK = 64
N_GROUPS = 4
TOP_K = 2
ROPE_THETA = 10000.0
LANES = 128
MOD_ROWS = 8
VMEM_LIMIT = 56 * 1024 * 1024


def _rup(n, m):
    return (n + m - 1) // m * m


def _tile(n, pref, quantum):
    t = min(pref, n) // quantum * quantum
    while t >= quantum:
        if n % t == 0:
            return t
        t -= quantum
    return n


def _params(*sem):
    return pltpu.CompilerParams(dimension_semantics=sem, vmem_limit_bytes=VMEM_LIMIT)


def _dot(a, b):
    return jnp.dot(a.astype(BF16), b.astype(BF16), preferred_element_type=F32)


def _dot_nt(a, b):
    return lax.dot_general(a.astype(BF16), b.astype(BF16), (((1,), (1,)), ((), ())),
                           preferred_element_type=F32)


def _dot_tn(a, b):
    return lax.dot_general(a.astype(BF16), b.astype(BF16), (((0,), (0,)), ((), ())),
                           preferred_element_type=F32)


def _dot_f32(a, b):
    return jnp.dot(a, b, preferred_element_type=F32, precision=HIGHEST)


def _mod_kernel(c_ref, w_ref, b_ref, o_ref):
    c = c_ref[...]
    c = c * jax.nn.sigmoid(c)
    o_ref[...] = _dot(c, w_ref[...]) + b_ref[...]


def _mod_table(cond, mod_w, mod_b, layer):
    d = cond.shape[1]
    n = mod_w.shape[2]
    tn = _tile(n, 512, LANES)
    return pl.pallas_call(
        _mod_kernel,
        grid=(n // tn,),
        in_specs=[pl.BlockSpec((MOD_ROWS, d), lambda j: (0, 0)),
                  pl.BlockSpec((None, d, tn), lambda j: (layer, 0, j)),
                  pl.BlockSpec((None, 1, tn), lambda j: (layer, 0, j))],
        out_specs=pl.BlockSpec((MOD_ROWS, tn), lambda j: (0, j)),
        out_shape=jax.ShapeDtypeStruct((MOD_ROWS, n), F32),
        compiler_params=_params("parallel"),
        name="mod_table",
    )(cond, mod_w, mod_b.reshape(mod_b.shape[0], 1, n))


def _norm_mod_kernel(x_ref, g_ref, sh_ref, sc_ref, h_ref):
    x = x_ref[...]
    y = x * lax.rsqrt(jnp.mean(x * x, axis=-1, keepdims=True) + RMS_EPS) * g_ref[...]
    h_ref[...] = (y * (1.0 + sc_ref[...]) + sh_ref[...]).astype(h_ref.dtype)


def _norm_mod_router_kernel(x_ref, g_ref, sh_ref, sc_ref, rw_ref, h_ref, lg_ref):
    x = x_ref[...]
    y = x * lax.rsqrt(jnp.mean(x * x, axis=-1, keepdims=True) + RMS_EPS) * g_ref[...]
    h = y * (1.0 + sc_ref[...]) + sh_ref[...]
    h_ref[...] = h.astype(h_ref.dtype)
    lg_ref[...] = _dot_f32(h, rw_ref[...])


def _norm_mod(x, g, mod3, shift_chunk, n_rows, dims, router_w=None):
    d = x.shape[1]
    tm = _tile(dims["ctx_rows"], 256, 8)
    grp = dims["group_of"](tm)
    in_specs = [pl.BlockSpec((tm, d), lambda i: (i, 0)),
                pl.BlockSpec((1, d), lambda i: (0, 0)),
                pl.BlockSpec((None, 1, d), lambda i: (grp(i), 0, shift_chunk)),
                pl.BlockSpec((None, 1, d), lambda i: (grp(i), 0, shift_chunk + 1))]
    args = [x, g.reshape(1, d), mod3, mod3]
    out_specs = [pl.BlockSpec((tm, d), lambda i: (i, 0))]
    out_shape = [jax.ShapeDtypeStruct((n_rows, d), BF16)]
    kern = _norm_mod_kernel
    if router_w is not None:
        in_specs.append(pl.BlockSpec(router_w.shape, lambda i: (0, 0)))
        args.append(router_w)
        out_specs.append(pl.BlockSpec((tm, router_w.shape[1]), lambda i: (i, 0)))
        out_shape.append(jax.ShapeDtypeStruct((n_rows, router_w.shape[1]), F32))
        kern = _norm_mod_router_kernel
    return pl.pallas_call(
        kern, grid=(n_rows // tm,), in_specs=in_specs, out_specs=out_specs, out_shape=out_shape,
        compiler_params=_params("parallel"), name="norm_mod",
    )(*args)


def _proj_kernel(a_ref, w_ref, o_ref):
    o_ref[...] = jnp.dot(a_ref[...], w_ref[...], preferred_element_type=F32).astype(o_ref.dtype)


def _proj(a, w, n_rows, out_dtype, col_off=0, n_cols=None, tm_pref=512, tn_pref=512):
    k = a.shape[1]
    n_cols = w.shape[1] - col_off if n_cols is None else n_cols
    tm = _tile(n_rows, tm_pref, 8)
    tn = _tile(int(np.gcd(n_cols, col_off)) if col_off else n_cols, tn_pref, LANES)
    off = col_off // tn
    return pl.pallas_call(
        _proj_kernel,
        grid=(n_cols // tn, n_rows // tm),
        in_specs=[pl.BlockSpec((tm, k), lambda j, i: (i, 0)),
                  pl.BlockSpec((k, tn), lambda j, i: (0, j + off))],
        out_specs=pl.BlockSpec((tm, tn), lambda j, i: (i, j)),
        out_shape=jax.ShapeDtypeStruct((n_rows, n_cols), out_dtype),
        compiler_params=_params("parallel", "parallel"), name="proj",
    )(a, w)


def _proj_res_kernel(a_ref, w_ref, x_ref, gt_ref, o_ref):
    acc = jnp.dot(a_ref[...], w_ref[...], preferred_element_type=F32)
    o_ref[...] = x_ref[...] + gt_ref[...] * acc


def _proj_residual(a, w, x, mod3, gate_chunk, n_rows, dims, tn_pref=512):
    k = a.shape[1]
    d = w.shape[1]
    tm = _tile(dims["ctx_rows"], 512, 8)
    tn = _tile(d, tn_pref, LANES)
    grp = dims["group_of"](tm)
    gblk = gate_chunk * (d // tn)
    return pl.pallas_call(
        _proj_res_kernel,
        grid=(d // tn, n_rows // tm),
        in_specs=[pl.BlockSpec((tm, k), lambda j, i: (i, 0)),
                  pl.BlockSpec((k, tn), lambda j, i: (0, j)),
                  pl.BlockSpec((tm, tn), lambda j, i: (i, j)),
                  pl.BlockSpec((None, 1, tn), lambda j, i: (grp(i), 0, gblk + j))],
        out_specs=pl.BlockSpec((tm, tn), lambda j, i: (i, j)),
        out_shape=jax.ShapeDtypeStruct((n_rows, d), F32),
        compiler_params=_params("parallel", "parallel"), name="proj_residual",
    )(a, w, x, mod3)


def _proj_qk_kernel(a_ref, w_ref, nw_ref, cos_ref, sin_ref, o_ref, *, head, scale):
    acc = jnp.dot(a_ref[...], w_ref[...], preferred_element_type=F32)
    nw = nw_ref[...]
    cs = cos_ref[...]
    sn = sin_ref[...]
    for s in range(acc.shape[1] // head):
        x = acc[:, s * head:(s + 1) * head]
        xn = x * lax.rsqrt(jnp.mean(x * x, axis=-1, keepdims=True) + RMS_EPS) * nw
        xr = xn * cs + pltpu.roll(xn, head // 2, 1) * sn
        o_ref[:, s * head:(s + 1) * head] = (xr * scale).astype(o_ref.dtype)


def _proj_qk(a, w, col_off, n_cols, norm_w, cos_t, sin_t, scale, n_rows):
    k = a.shape[1]
    head = norm_w.shape[0]
    tm = _tile(n_rows, 512, 8)
    tn = _tile(int(np.gcd(n_cols, col_off)) if col_off else n_cols, 512, head)
    off = col_off // tn
    return pl.pallas_call(
        functools.partial(_proj_qk_kernel, head=head, scale=scale),
        grid=(n_cols // tn, n_rows // tm),
        in_specs=[pl.BlockSpec((tm, k), lambda j, i: (i, 0)),
                  pl.BlockSpec((k, tn), lambda j, i: (0, j + off)),
                  pl.BlockSpec((1, head), lambda j, i: (0, 0)),
                  pl.BlockSpec((tm, head), lambda j, i: (i, 0)),
                  pl.BlockSpec((tm, head), lambda j, i: (i, 0))],
        out_specs=pl.BlockSpec((tm, tn), lambda j, i: (i, j)),
        out_shape=jax.ShapeDtypeStruct((n_rows, n_cols), BF16),
        compiler_params=_params("parallel", "parallel"), name="proj_qk",
    )(a, w, norm_w.reshape(1, head), cos_t, sin_t)


def _seg_sum(x, seg):
    n = x.shape[1]
    lane_blk = LANES if n % LANES == 0 else n
    r = lax.broadcasted_iota(jnp.int32, (lane_blk, lane_blk), 0) // seg
    c = lax.broadcasted_iota(jnp.int32, (lane_blk, lane_blk), 1) // seg
    ones_bd = (r == c).astype(F32)
    parts = [_dot_f32(x[:, s:s + lane_blk], ones_bd) for s in range(0, n, lane_blk)]
    return parts[0] if len(parts) == 1 else jnp.concatenate(parts, axis=1)


def _rwkv_prep_kernel(z_ref, zp_ref, zn_ref, t_ref, tp_ref, tn_ref, mu_ref, mut_ref, w0_ref, wup_ref,
                      a0_ref, aup_ref, gup_ref, kk_ref, ka_ref, rk_ref,
                      r_out, v_out, kkn_out, lw_out, km_out, bb_out, bon_out, gm_out,
                      *, rd, lg, lwp, seq, ctx, n_lat_rows):
    tm = z_ref.shape[0]
    row0 = pl.program_id(0) * tm
    in_lat = row0 < n_lat_rows
    seq_len = jnp.where(in_lat, seq, ctx)
    pos0 = jnp.where(in_lat, row0 % seq, (row0 - n_lat_rows) % ctx)
    has_prev = (pos0 != 0).astype(F32)
    has_next = (pos0 + tm != seq_len).astype(F32)

    def token_shift(cur_ref, prev_ref, next_ref, mix_ref):
        cur = cur_ref[...]
        rows = lax.broadcasted_iota(jnp.int32, cur.shape, 0)
        prev_row = prev_ref[7:8, :] * has_prev
        next_row = next_ref[0:1, :] * has_next
        before = jnp.where(rows == 0, prev_row, pltpu.roll(cur, 1, 0))
        after = jnp.where(rows == tm - 1, next_row, pltpu.roll(cur, tm - 1, 0))
        return cur + mix_ref[...] * (0.5 * (before + after) - cur)

    zs = token_shift(z_ref, zp_ref, zn_ref, mu_ref)
    ts = token_shift(t_ref, tp_ref, tn_ref, mut_ref)
    r = zs[:, 0:rd]
    k = zs[:, rd:2 * rd]
    v = zs[:, 2 * rd:3 * rd]
    g_down = ts[:, 0:lg]
    w_down = ts[:, lg:lg + lwp]
    a_down = ts[:, lg + lwp:lg + 2 * lwp]

    kk = k * kk_ref[...]
    kk = kk * lax.rsqrt(jnp.maximum(_seg_sum(kk * kk, RWKV_HEAD), 1e-24))
    r_out[...] = r
    v_out[...] = v
    kkn_out[...] = kk
    tw = jnp.tanh(w_down)
    bonus = jnp.zeros_like(r)
    for d in range(2):
        lw_out[d] = -np.float32(np.exp(-0.5)) * jax.nn.sigmoid(w0_ref[d] + _dot(tw, wup_ref[d]))
        a = jax.nn.sigmoid(a0_ref[d] + _dot(a_down, aup_ref[d]))
        k_mod = k * (1.0 + (a - 1.0) * ka_ref[...])
        km_out[d] = k_mod
        bb_out[d] = kk * a
        bonus = bonus + _seg_sum(r * k_mod * rk_ref[...], RWKV_HEAD) * v
    bon_out[...] = bonus
    gm_out[...] = _dot(jax.nn.sigmoid(g_down), gup_ref[...])


def _rwkv_prep(z, n_rows, lay, dims, mu_main, mu_tail, w0, w_up_p, a0, a_up_p, g_up, k_k, k_a, r_k):
    rd, lg, lwp = lay["rd"], lay["lg"], lay["lwp"]
    mw = 3 * rd
    tw = lg + 2 * lwp
    tblk = lay["tail"] // tw
    tm = _tile(dims["ctx"], 64, 8)
    nb8 = n_rows // 8
    full = lambda shape: pl.BlockSpec(shape, lambda i: (0,) * len(shape))
    row_spec = pl.BlockSpec((tm, rd), lambda i: (i, 0))
    dir_spec = pl.BlockSpec((2, tm, rd), lambda i: (0, i, 0))
    sds = jax.ShapeDtypeStruct
    kern = functools.partial(_rwkv_prep_kernel, rd=rd, lg=lg, lwp=lwp, seq=dims["seq"], ctx=dims["ctx"],
                             n_lat_rows=dims["lat_rows"])
    return pl.pallas_call(
        kern,
        grid=(n_rows // tm,),
        in_specs=[pl.BlockSpec((tm, mw), lambda i: (i, 0)),
                  pl.BlockSpec((8, mw), lambda i: (jnp.maximum(i * (tm // 8) - 1, 0), 0)),
                  pl.BlockSpec((8, mw), lambda i: (jnp.minimum((i + 1) * (tm // 8), nb8 - 1), 0)),
                  pl.BlockSpec((tm, tw), lambda i: (i, tblk)),
                  pl.BlockSpec((8, tw), lambda i: (jnp.maximum(i * (tm // 8) - 1, 0), tblk)),
                  pl.BlockSpec((8, tw), lambda i: (jnp.minimum((i + 1) * (tm // 8), nb8 - 1), tblk)),
                  full((1, mw)), full((1, tw)), full((2, 1, rd)), full((2, lwp, rd)), full((2, 1, rd)), full((2, lwp, rd)),
                  full((lg, rd)), full((1, rd)), full((1, rd)), full((1, rd))],
        out_specs=[row_spec, row_spec, row_spec, dir_spec, dir_spec, dir_spec, row_spec, row_spec],
        out_shape=[sds((n_rows, rd), F32)] * 3 + [sds((2, n_rows, rd), F32)] * 3 + [sds((n_rows, rd), F32)] * 2,
        compiler_params=_params("parallel"), name="rwkv_prep",
    )(z, z, z, z, z, z, mu_main, mu_tail, w0.reshape(2, 1, rd), w_up_p, a0.reshape(2, 1, rd), a_up_p, g_up,
      k_k.reshape(1, rd), k_a.reshape(1, rd), r_k.reshape(1, rd))


def _unit_lower_inverse(m):
    n = m.shape[0]
    eye = (lax.broadcasted_iota(jnp.int32, (n, n), 0) == lax.broadcasted_iota(jnp.int32, (n, n), 1)).astype(F32)
    p = eye + m
    span = 2
    while span < n:
        m = _dot(m, m)
        p = p + _dot(m, p)
        span *= 2
    return p


def _rwkv_scan_kernel(r_ref, v_ref, kk_ref, lw_ref, km_ref, bb_ref, y_ref, s_ref, *, heads):
    d = pl.program_id(0)
    j = pl.program_id(3)

    @pl.when(j == 0)
    def _():
        s_ref[...] = jnp.zeros_like(s_ref)

    n = RWKV_HEAD
    chunk = lw_ref.shape[0]
    t_idx = lax.broadcasted_iota(jnp.int32, (chunk, chunk), 0)
    s_idx = lax.broadcasted_iota(jnp.int32, (chunk, chunk), 1)
    lead = jnp.where(d == 0, t_idx - s_idx, s_idx - t_idx)
    incl = lead >= 0
    strict = lead > 0

    lw = lw_ref[...]
    cum = _dot_f32(incl.astype(F32), lw)
    tot = jnp.sum(lw, axis=0, keepdims=True)
    e_in = jnp.exp(cum)
    e_neg = jnp.exp(-cum)
    e_last = jnp.exp(tot - cum)
    kkn = kk_ref[...]
    km = km_ref[...]
    bb = bb_ref[...]
    r_t = r_ref[...] * e_in
    a_t = -kkn * jnp.exp(cum - lw)
    b_t = bb * e_neg
    k_t = km * e_neg
    b_l = bb * e_last
    k_l = km * e_last
    w_l = jnp.exp(tot)
    v_all = v_ref[...]

    for h in range(heads):
        sl = slice(h * n, (h + 1) * n)
        ah, rh, bh, kh, vh = a_t[:, sl], r_t[:, sl], b_t[:, sl], k_t[:, sl], v_all[:, sl]
        a_ab = jnp.where(strict, _dot_nt(ah, bh), 0.0)
        a_ak = jnp.where(strict, _dot_nt(ah, kh), 0.0)
        a_rb = jnp.where(incl, _dot_nt(rh, bh), 0.0)
        a_rk = jnp.where(incl, _dot_nt(rh, kh), 0.0)
        t_inv = _unit_lower_inverse(a_ab)
        s0 = s_ref[h]
        u = _dot(t_inv, _dot_nt(ah, s0) + _dot(a_ak, vh))
        y_ref[:, sl] = _dot_nt(rh, s0) + _dot(a_rb, u) + _dot(a_rk, vh)
        s_ref[h] = s0 * w_l[:, sl] + _dot_tn(u, b_l[:, sl]) + _dot_tn(vh, k_l[:, sl])


def _chunk_index(d, b, j, dims):
    nc_ctx, nc_lat, nb = dims["ctx"] // CHUNK, dims["seq"] // CHUNK, dims["batch"]
    jc = jnp.where(d == 0, j, nc_ctx - 1 - j)
    jl = jnp.where(d == 0, j - nc_ctx, nc_lat - 1 - (j - nc_ctx))
    return jnp.where(j < nc_ctx, nb * nc_lat + b * nc_ctx + jc, b * nc_lat + jl)


def _rwkv_scan(r, v, kkn, lw, km, bb, dims, heads_per_step):
    n_rows, rd = r.shape
    gw = heads_per_step * RWKV_HEAD
    n_chunks = (dims["ctx"] + dims["seq"]) // CHUNK
    cidx = lambda d, b, g, j: _chunk_index(d, b, j, dims)
    row_spec = pl.BlockSpec((CHUNK, gw), lambda d, b, g, j: (cidx(d, b, g, j), g))
    dir_spec = pl.BlockSpec((None, CHUNK, gw), lambda d, b, g, j: (d, cidx(d, b, g, j), g))
    return pl.pallas_call(
        functools.partial(_rwkv_scan_kernel, heads=heads_per_step),
        grid=(2, dims["batch"], rd // gw, n_chunks),
        in_specs=[row_spec, row_spec, row_spec, dir_spec, dir_spec, dir_spec],
        out_specs=dir_spec,
        out_shape=jax.ShapeDtypeStruct((2, n_rows, rd), F32),
        scratch_shapes=[pltpu.VMEM((heads_per_step, RWKV_HEAD, RWKV_HEAD), F32)],
        compiler_params=_params("parallel", "parallel", "parallel", "arbitrary"), name="rwkv_scan",
    )(r, v, kkn, lw, km, bb)


def _mlstm_kernel(q_ref, k_ref, v_ref, gc_ref, gr_ref, bc_ref, br_ref, h_ref, c_ref, n_ref, m_ref,
                  *, heads, dqk, dv):
    d = pl.program_id(0)
    j = pl.program_id(2)

    @pl.when(j == 0)
    def _():
        c_ref[...] = jnp.zeros_like(c_ref)
        n_ref[...] = jnp.zeros_like(n_ref)
        m_ref[...] = jnp.zeros_like(m_ref)

    chunk = q_ref.shape[0]
    t_idx = lax.broadcasted_iota(jnp.int32, (chunk, chunk), 0)
    s_idx = lax.broadcasted_iota(jnp.int32, (chunk, chunk), 1)
    lead = jnp.where(d == 0, t_idx - s_idx, s_idx - t_idx)
    incl = lead >= 0
    tri = incl.astype(F32)
    tri_t = (lead <= 0).astype(F32)

    gcol = gc_ref[...] + bc_ref[...]
    grow = gr_ref[...] + br_ref[...]
    i_col = gcol[:, :heads]
    f_col = jax.nn.log_sigmoid(gcol[:, heads:])
    i_row = grow[:heads, :]
    f_row = jax.nn.log_sigmoid(grow[heads:, :])
    b_col = _dot_f32(tri, f_col)
    b_row = _dot_f32(f_row, tri_t)
    b_last = jnp.sum(f_col, axis=0, keepdims=True)
    scale = np.float32(dqk ** -0.5)

    for h in range(heads):
        q = q_ref[:, h * dqk:(h + 1) * dqk] * scale
        k = k_ref[:, h * dqk:(h + 1) * dqk]
        v = v_ref[:, h * dv:(h + 1) * dv]
        bc = b_col[:, h:h + 1]
        ic = i_col[:, h:h + 1]
        m_st = m_ref[h][:, 0:1]
        bl = b_last[:, h:h + 1]
        dmat = jnp.where(incl, bc + (i_row[h:h + 1, :] - b_row[h:h + 1, :]), -jnp.inf)
        inter = bc + m_st
        m_t = jnp.maximum(inter, jnp.max(dmat, axis=-1, keepdims=True))
        w_inter = jnp.exp(inter - m_t)
        s = _dot_nt(q, k) * jnp.exp(dmat - m_t)
        c_st = c_ref[h]
        n_st = n_ref[h]
        num = w_inter * _dot(q, c_st) + _dot(s, v)
        den = w_inter * jnp.sum(q * n_st, axis=-1, keepdims=True) + jnp.sum(s, axis=-1, keepdims=True)
        h_ref[:, h * dv:(h + 1) * dv] = num / jnp.maximum(jnp.abs(den), jnp.exp(-m_t))
        g = bl - bc + ic
        m_new = jnp.maximum(bl + m_st, jnp.max(g, axis=0, keepdims=True))
        decay = jnp.exp(bl + m_st - m_new)
        kw = k * jnp.exp(g - m_new)
        c_ref[h] = decay * c_st + _dot_tn(kw, v)
        n_ref[h] = decay * n_st + jnp.sum(kw, axis=0, keepdims=True)
        m_ref[h] = jnp.broadcast_to(m_new, m_ref.shape[1:])


def _mlstm_scan(z, lay, dims, gates_col, gates_row, bias_col, bias_row, heads, dqk, dv):
    n_rows = z.shape[0]
    n_chunks = (dims["ctx"] + dims["seq"]) // CHUNK
    md = heads * dv
    qw = heads * dqk
    cidx = lambda d, b, j: _chunk_index(d, b, j, dims)
    q_blk, k_blk, v_blk = lay["mq"] // qw, lay["mk"] // qw, lay["mv"] // md
    return pl.pallas_call(
        functools.partial(_mlstm_kernel, heads=heads, dqk=dqk, dv=dv),
        grid=(2, dims["batch"], n_chunks),
        in_specs=[pl.BlockSpec((CHUNK, qw), lambda d, b, j: (cidx(d, b, j), q_blk)),
                  pl.BlockSpec((CHUNK, qw), lambda d, b, j: (cidx(d, b, j), k_blk)),
                  pl.BlockSpec((CHUNK, md), lambda d, b, j: (cidx(d, b, j), v_blk)),
                  pl.BlockSpec((None, None, CHUNK, 2 * heads), lambda d, b, j: (d, cidx(d, b, j), 0, 0)),
                  pl.BlockSpec((None, None, 2 * heads, CHUNK), lambda d, b, j: (d, cidx(d, b, j), 0, 0)),
                  pl.BlockSpec((None, 1, 2 * heads), lambda d, b, j: (d, 0, 0)),
                  pl.BlockSpec((None, 2 * heads, 1), lambda d, b, j: (d, 0, 0))],
        out_specs=pl.BlockSpec((None, CHUNK, md), lambda d, b, j: (d, cidx(d, b, j), 0)),
        out_shape=jax.ShapeDtypeStruct((2, n_rows, md), F32),
        scratch_shapes=[pltpu.VMEM((heads, dqk, dv), F32), pltpu.VMEM((heads, 1, dqk), F32),
                        pltpu.VMEM((heads, 1, LANES), F32)],
        compiler_params=_params("parallel", "parallel", "arbitrary"), name="mlstm_scan",
    )(z, z, z, gates_col, gates_row, bias_col, bias_row)


def _even_finish_kernel(y_ref, bon_ref, gm_ref, gnw_ref, gnb_ref, h_ref, o_ref, nw_ref, out_ref,
                        *, rd, heads, dv):
    y = y_ref[0] + y_ref[1]
    inv_n = np.float32(1.0 / RWKV_HEAD)
    mean = _seg_sum(y, RWKV_HEAD) * inv_n
    yc = y - mean
    var = _seg_sum(yc * yc, RWKV_HEAD) * inv_n
    yn = yc * lax.rsqrt(var + GN_EPS) * gnw_ref[...] + gnb_ref[...] + bon_ref[...]
    out_ref[:, 0:rd] = (yn * gm_ref[...]).astype(out_ref.dtype)
    hm = h_ref[0] + h_ref[1]
    for h in range(heads):
        sl = slice(h * dv, (h + 1) * dv)
        x = hm[:, sl]
        xn = x * lax.rsqrt(jnp.mean(x * x, axis=-1, keepdims=True) + RMS_EPS) * nw_ref[:, sl]
        out_ref[:, rd + h * dv:rd + (h + 1) * dv] = (jax.nn.sigmoid(o_ref[:, sl]) * xn).astype(out_ref.dtype)


def _even_finish(y, bonus, gmul, gn_w, gn_b, hm, z, lay, ml_norm, n_rows, dims, heads, dv):
    rd = y.shape[2]
    md = hm.shape[2]
    tm = _tile(dims["ctx_rows"], 256, 8)
    o_blk = lay["mo"] // md
    return pl.pallas_call(
        functools.partial(_even_finish_kernel, rd=rd, heads=heads, dv=dv),
        grid=(n_rows // tm,),
        in_specs=[pl.BlockSpec((2, tm, rd), lambda i: (0, i, 0)),
                  pl.BlockSpec((tm, rd), lambda i: (i, 0)),
                  pl.BlockSpec((tm, rd), lambda i: (i, 0)),
                  pl.BlockSpec((1, rd), lambda i: (0, 0)),
                  pl.BlockSpec((1, rd), lambda i: (0, 0)),
                  pl.BlockSpec((2, tm, md), lambda i: (0, i, 0)),
                  pl.BlockSpec((tm, md), lambda i: (i, o_blk)),
                  pl.BlockSpec((1, md), lambda i: (0, 0))],
        out_specs=pl.BlockSpec((tm, rd + md), lambda i: (i, 0)),
        out_shape=jax.ShapeDtypeStruct((n_rows, rd + md), BF16),
        compiler_params=_params("parallel"), name="even_finish",
    )(y, bonus, gmul, gn_w.reshape(1, rd), gn_b.reshape(1, rd), hm, z, ml_norm.reshape(1, md))


def _attn_kernel(q_ref, kl_ref, kc_ref, vl_ref, vc_ref, o_ref, *, group, head):
    kl = kl_ref[...]
    kc = kc_ref[...]
    vl = vl_ref[...]
    vc = vc_ref[...]
    for g in range(group):
        sl = slice(g * head, (g + 1) * head)
        q = q_ref[:, sl]
        s_l = _dot_nt(q, kl)
        s_c = _dot_nt(q, kc)
        m = jnp.maximum(jnp.max(s_l, axis=-1, keepdims=True), jnp.max(s_c, axis=-1, keepdims=True))
        p_l = jnp.exp(s_l - m)
        p_c = jnp.exp(s_c - m)
        den = jnp.sum(p_l, axis=-1, keepdims=True) + jnp.sum(p_c, axis=-1, keepdims=True)
        o = _dot(p_l, vl) + _dot(p_c, vc)
        o_ref[:, sl] = (o / den).astype(o_ref.dtype)


def _attention(q, k, v, dims, head, group):
    seq, ctx, nb = dims["seq"], dims["ctx"], dims["batch"]
    kvh = k.shape[1] // head
    tq = _tile(seq, 256, 8)
    nq = seq // tq
    gw = group * head
    return pl.pallas_call(
        functools.partial(_attn_kernel, group=group, head=head),
        grid=(nb, kvh, nq),
        in_specs=[pl.BlockSpec((tq, gw), lambda b, h, i: (b * nq + i, h)),
                  pl.BlockSpec((seq, head), lambda b, h, i: (b, h)),
                  pl.BlockSpec((ctx, head), lambda b, h, i: (nb * (seq // ctx) + b, h)),
                  pl.BlockSpec((seq, head), lambda b, h, i: (b, h)),
                  pl.BlockSpec((ctx, head), lambda b, h, i: (nb * (seq // ctx) + b, h))],
        out_specs=pl.BlockSpec((tq, gw), lambda b, h, i: (b * nq + i, h)),
        out_shape=jax.ShapeDtypeStruct(q.shape, BF16),
        compiler_params=_params("parallel", "parallel", "parallel"), name="attention",
    )(q, k, k, v, v)


def _moe_kernel(be_ref, nu_ref, x_ref, w1_ref, w3_ref, w2_ref, o_ref):
    i = pl.program_id(0)
    kt = pl.program_id(1)
    used = i < nu_ref[0]

    @pl.when(used)
    def _():
        x = x_ref[...]
        a = jnp.dot(x, w1_ref[...], preferred_element_type=F32)
        b = jnp.dot(x, w3_ref[...], preferred_element_type=F32)
        mid = (a * jax.nn.sigmoid(a) * b).astype(BF16)
        part = jnp.dot(mid, w2_ref[...], preferred_element_type=F32)

        @pl.when(kt == 0)
        def _():
            o_ref[...] = part

        @pl.when(kt > 0)
        def _():
            o_ref[...] += part

    @pl.when(jnp.logical_and(jnp.logical_not(used), kt == 0))
    def _():
        o_ref[...] = jnp.zeros_like(o_ref)


def _moe_ffn(xs, block_expert, n_used, w1, w3, w2, bm):
    n_rows, d = xs.shape
    de = w1.shape[2]
    tde = _tile(de, 512, LANES)
    grid_spec = pltpu.PrefetchScalarGridSpec(
        num_scalar_prefetch=2,
        grid=(n_rows // bm, de // tde),
        in_specs=[pl.BlockSpec((bm, d), lambda i, k, be, nu: (i, 0)),
                  pl.BlockSpec((None, d, tde), lambda i, k, be, nu: (be[i], 0, k)),
                  pl.BlockSpec((None, d, tde), lambda i, k, be, nu: (be[i], 0, k)),
                  pl.BlockSpec((None, tde, d), lambda i, k, be, nu: (be[i], k, 0))],
        out_specs=pl.BlockSpec((bm, d), lambda i, k, be, nu: (i, 0)),
    )
    return pl.pallas_call(
        _moe_kernel, grid_spec=grid_spec,
        out_shape=jax.ShapeDtypeStruct((n_rows, d), F32),
        compiler_params=_params("parallel", "arbitrary"), name="moe_ffn",
    )(block_expert, n_used, xs, w1, w3, w2)


def _route(logits, router_bias, n_experts, bm):
    n_tok = logits.shape[0]
    per_group = n_experts // N_GROUPS
    aff = jax.nn.sigmoid(logits[:, :n_experts])
    sel = aff + router_bias.astype(F32)
    grp_score = jnp.sum(lax.top_k(sel.reshape(n_tok, N_GROUPS, per_group), 2)[0], -1)
    best = jnp.argmax(grp_score, -1)
    in_group = (jnp.arange(n_experts) // per_group)[None, :] == best[:, None]
    _, idx = lax.top_k(jnp.where(in_group, sel, -jnp.inf), TOP_K)
    gate = jnp.take_along_axis(aff, idx, -1)
    gate = gate / jnp.sum(gate, -1, keepdims=True)

    n_assign = n_tok * TOP_K
    expert = idx.reshape(-1)
    onehot = (expert[:, None] == jnp.arange(n_experts)[None, :]).astype(jnp.int32)
    rank = jnp.take_along_axis(jnp.cumsum(onehot, axis=0) - onehot, expert[:, None], 1)[:, 0]
    counts = jnp.sum(onehot, axis=0)
    padded = (counts + bm - 1) // bm * bm
    pad_end = jnp.cumsum(padded)
    pos = (pad_end - padded)[expert] + rank
    n_blocks = -(-n_assign // bm) + n_experts
    token = jnp.repeat(jnp.arange(n_tok, dtype=jnp.int32), TOP_K)
    row_token = jnp.zeros((n_blocks * bm,), jnp.int32).at[pos].set(token)
    block_expert = jnp.minimum(jnp.searchsorted(pad_end, jnp.arange(n_blocks) * bm, side='right'),
                               n_experts - 1).astype(jnp.int32)
    n_used = (pad_end[-1] // bm).astype(jnp.int32).reshape(1)
    block_expert = jnp.where(jnp.arange(n_blocks) < n_used[0], block_expert,
                             block_expert[jnp.maximum(n_used[0] - 1, 0)])
    return gate, pos.reshape(n_tok, TOP_K), row_token, block_expert, n_used


def _ffn_res_kernel(x_ref, f0_ref, f1_ref, g_ref, gt_ref, o_ref):
    g = g_ref[...]
    f = g[:, 0:1] * f0_ref[...] + g[:, 1:2] * f1_ref[...]
    o_ref[...] = x_ref[...] + gt_ref[...] * f


def _ffn_res_norm_kernel(x_ref, f0_ref, f1_ref, g_ref, gt_ref, nw_ref, o_ref):
    g = g_ref[...]
    f = g[:, 0:1] * f0_ref[...] + g[:, 1:2] * f1_ref[...]
    x = x_ref[...] + gt_ref[...] * f
    o_ref[...] = x * lax.rsqrt(jnp.mean(x * x, axis=-1, keepdims=True) + RMS_EPS) * nw_ref[...]


def _ffn_residual(x, f0, f1, gate, mod3, gate_chunk, n_rows, dims, final_norm=None):
    d = x.shape[1]
    tm = _tile(dims["ctx_rows"], 256, 8)
    grp = dims["group_of"](tm)
    row = pl.BlockSpec((tm, d), lambda i: (i, 0))
    in_specs = [row, row, row, pl.BlockSpec((tm, TOP_K), lambda i: (i, 0)),
                pl.BlockSpec((None, 1, d), lambda i: (grp(i), 0, gate_chunk))]
    args = [x, f0, f1, gate, mod3]
    kern = _ffn_res_kernel
    if final_norm is not None:
        in_specs.append(pl.BlockSpec((1, d), lambda i: (0, 0)))
        args.append(final_norm.reshape(1, d))
        kern = _ffn_res_norm_kernel
    return pl.pallas_call(
        kern, grid=(n_rows // tm,), in_specs=in_specs, out_specs=row,
        out_shape=jax.ShapeDtypeStruct((n_rows, d), F32),
        compiler_params=_params("parallel"), name="ffn_residual",
    )(*args)


def _even_layout(rd, lw, la, lg, mh, dqk, md):
    lwp = _rup(lw + 4 * mh, LANES)
    assert _rup(la, LANES) == lwp
    lay = {"rd": rd, "lg": lg, "lwp": lwp}
    off = 3 * rd
    for name, width in (("mq", mh * dqk), ("mk", mh * dqk), ("mv", md), ("mo", md), ("tail", lg + 2 * lwp)):
        assert off % width == 0
        lay[name] = off
        off += width
    lay["width"] = off
    return lay


def _pad_cols(seg, width):
    return jnp.pad(seg, ((0, 0), (0, width - seg.shape[1])))


def _pack_even_weights(w_in, mu, lay, lw, la, mh, dqk, md):
    rd, lg, lwp = lay["rd"], lay["lg"], lay["lwp"]
    rww = 3 * rd + lw + la + lg
    o = 3 * rd
    w_down, a_down, g_down = w_in[:, o:o + lw], w_in[:, o + lw:o + lw + la], w_in[:, o + lw + la:rww]
    m_end = rww + 2 * mh * dqk + 2 * md
    w_p = jnp.concatenate([w_in[:, :3 * rd], w_in[:, rww:m_end], g_down,
                           _pad_cols(jnp.concatenate([w_down, w_in[:, m_end:]], 1), lwp),
                           _pad_cols(a_down, lwp)], axis=1).astype(BF16)
    mu2 = mu.reshape(1, -1)
    mu_tail = jnp.concatenate([mu2[:, o + lw + la:rww], _pad_cols(mu2[:, o:o + lw], lwp),
                               _pad_cols(mu2[:, o + lw:o + lw + la], lwp)], axis=1)
    return w_p, mu2[:, :3 * rd], mu_tail


def _pad_rows(w, rows):
    return jnp.pad(w, ((0, 0), (0, rows - w.shape[1]), (0, 0)))


def _rope_tables(dims, head):
    seq, nb = dims["seq"], dims["batch"]
    pairs = head // 4
    rows = seq // GRID_W
    row = jnp.repeat(jnp.arange(rows), GRID_W).astype(F32)
    col = jnp.tile(jnp.arange(GRID_W), rows).astype(F32)
    inv = ROPE_THETA ** (-jnp.arange(pairs, dtype=F32) / pairs)
    ang = jnp.concatenate([row[:, None] * inv, col[:, None] * inv], -1)
    cos, sin = jnp.cos(ang), jnp.sin(ang)
    cos_t = jnp.tile(jnp.concatenate([cos, cos], -1), (nb, 1))
    sin_t = jnp.tile(jnp.concatenate([-sin, sin], -1), (nb, 1))
    n_ctx = dims["ctx_rows"]
    return (jnp.concatenate([cos_t, jnp.ones((n_ctx, head), F32)], 0),
            jnp.concatenate([sin_t, jnp.zeros((n_ctx, head), F32)], 0))


def kernel(x, c, ctx, c_ctx, mod_w, mod_b, norm_mix, norm_ffn, norm_final, router_w, router_bias,
           exp_w1, exp_w3, exp_w2, ev_w_in, ev_w_out, rw_mu, rw_w0, rw_w_up, rw_a0, rw_a_up, rw_g_up,
           rw_k_k, rw_k_a, rw_r_k, rw_gn_w, rw_gn_b, ml_gate_b, ml_norm, at_w_qkv, at_q_norm, at_k_norm,
           at_w_o):
    nb, seq, d = x.shape
    n_ctx = ctx.shape[1]
    depth = mod_w.shape[0]
    lat_rows, ctx_rows = nb * seq, nb * n_ctx
    all_rows = lat_rows + ctx_rows
    assert nb + 1 <= MOD_ROWS and seq % n_ctx == 0 and n_ctx % CHUNK == 0
    dims = {"batch": nb, "seq": seq, "ctx": n_ctx, "lat_rows": lat_rows, "ctx_rows": ctx_rows,
            "group_of": lambda tm: (lambda i: jnp.minimum(i * tm // seq, nb))}

    n_experts = router_w.shape[1]
    rd = rw_w0.shape[-1]
    lw, la, lg = rw_w_up.shape[2], rw_a_up.shape[2], rw_g_up.shape[1]
    mh = ml_gate_b.shape[-1]
    md = ml_norm.shape[-1]
    dv = md // mh
    dqk = dv // 2
    head = at_q_norm.shape[-1]
    kv_dim = (at_w_qkv.shape[-1] - d) // 2
    group = (d // head) // (kv_dim // head)
    moe_bm = 256

    xa = jnp.concatenate([x.reshape(lat_rows, d), ctx.reshape(ctx_rows, d)], axis=0)
    cond = jnp.zeros((MOD_ROWS, d), F32).at[:nb].set(c).at[nb].set(c_ctx)
    router_p = _pad_cols(router_w, _rup(n_experts, LANES))
    lay = _even_layout(rd, lw, la, lg, mh, dqk, md)
    out = None

    for layer in range(depth):
        ctx_out = layer < depth - 1
        j = layer // 2
        rows_out = all_rows if ctx_out else lat_rows
        mod3 = _mod_table(cond, mod_w, mod_b, layer).reshape(MOD_ROWS, 1, 6 * d)

        h = _norm_mod(xa, norm_mix[layer], mod3, 0, all_rows, dims)[0]
        if layer % 2 == 0:
            w_p, mu_main, mu_tail = _pack_even_weights(ev_w_in[j], rw_mu[j], lay, lw, la, mh, dqk, md)
            z = _proj(h, w_p, all_rows, F32)
            r, v, kkn, lwd, km, bb, bonus, gmul = _rwkv_prep(
                z, all_rows, lay, dims, mu_main, mu_tail, rw_w0[j], _pad_rows(rw_w_up[j], lay["lwp"]).astype(BF16),
                rw_a0[j], _pad_rows(rw_a_up[j], lay["lwp"]).astype(BF16), rw_g_up[j].astype(BF16),
                rw_k_k[j], rw_k_a[j], rw_r_k[j])
            y = _rwkv_scan(r, v, kkn, lwd, km, bb, dims, heads_per_step=min(4, rd // RWKV_HEAD))
            g_off = lay["tail"] + lay["lg"] + lw
            gates = z[:, g_off:g_off + 4 * mh].reshape(all_rows, 2, 2 * mh)
            gates_col = jnp.moveaxis(gates, 1, 0).reshape(2, all_rows // CHUNK, CHUNK, 2 * mh)
            gates_row = jnp.swapaxes(gates_col, 2, 3)
            bias = ml_gate_b[j].reshape(2, 2 * mh)
            hm = _mlstm_scan(z, lay, dims, gates_col, gates_row, bias.reshape(2, 1, 2 * mh),
                             bias.reshape(2, 2 * mh, 1), mh, dqk, dv)
            mix = _even_finish(y, bonus, gmul, rw_gn_w[j], rw_gn_b[j], hm, z, lay, ml_norm[j],
                               rows_out, dims, mh, dv)
            xa_new = _proj_residual(mix, ev_w_out[j].astype(BF16), xa, mod3, 2, rows_out, dims)
        else:
            w_qkv = at_w_qkv[j].astype(BF16)
            cos_t, sin_t = _rope_tables(dims, head)
            q = _proj_qk(h, w_qkv, 0, d, at_q_norm[j], cos_t, sin_t, np.float32(head ** -0.5), lat_rows)
            k = _proj_qk(h, w_qkv, d, kv_dim, at_k_norm[j], cos_t, sin_t, np.float32(1.0), all_rows)
            v = _proj(h, w_qkv, all_rows, BF16, col_off=d + kv_dim, n_cols=kv_dim)
            att = _attention(q, k, v, dims, head, group)
            xa_new = _proj_residual(att, at_w_o[j].astype(BF16), xa, mod3, 2, lat_rows, dims)
            if ctx_out:
                raise NotImplementedError("context output of an attention layer")
        xa = xa_new

        h2, logits = _norm_mod(xa, norm_ffn[layer], mod3, 3, rows_out, dims, router_w=router_p)
        gate, pos, row_token, block_expert, n_used = _route(logits, router_bias, n_experts, moe_bm)
        xs = jnp.take(h2, row_token, axis=0)
        ys = _moe_ffn(xs, block_expert, n_used, exp_w1[layer].astype(BF16), exp_w3[layer].astype(BF16),
                      exp_w2[layer].astype(BF16), moe_bm)
        f0 = jnp.take(ys, pos[:, 0], axis=0)
        f1 = jnp.take(ys, pos[:, 1], axis=0)
        last = layer == depth - 1
        xa = _ffn_residual(xa, f0, f1, gate, mod3, 5, rows_out, dims,
                           final_norm=norm_final if last else None)
        if last:
            out = xa[:lat_rows].reshape(nb, seq, d)
    return out
```

```python
import functools

import jax
import jax.numpy as jnp
import numpy as np
from jax import lax
from jax.experimental import pallas as pl
from jax.experimental.pallas import tpu as pltpu

F32 = jnp.float32
BF16 = jnp.bfloat16
HIGHEST = lax.Precision.HIGHEST

GRID_W = 64
RMS_EPS = 1e-6
GN_EPS = 64e-5
RWKV_HEAD = 64
CHUNK = 64
N_GROUPS = 4
TOP_K = 2
ROPE_THETA = 10000.0
LANES = 128
MOD_ROWS = 8
VMEM_LIMIT = 56 * 1024 * 1024


def _rup(n, m):
    return (n + m - 1) // m * m


def _tile(n, pref, quantum):
    t = min(pref, n) // quantum * quantum
    while t >= quantum:
        if n % t == 0:
            return t
        t -= quantum
    return n


def _params(*sem):
    return pltpu.CompilerParams(dimension_semantics=sem, vmem_limit_bytes=VMEM_LIMIT)


def _dot(a, b):
    return jnp.dot(a.astype(BF16), b.astype(BF16), preferred_element_type=F32)


def _dot_nt(a, b):
    return lax.dot_general(a.astype(BF16), b.astype(BF16), (((1,), (1,)), ((), ())),
                           preferred_element_type=F32)


def _dot_tn(a, b):
    return lax.dot_general(a.astype(BF16), b.astype(BF16), (((0,), (0,)), ((), ())),
                           preferred_element_type=F32)


def _dot_f32(a, b):
    return jnp.dot(a, b, preferred_element_type=F32, precision=HIGHEST)


def _mod_kernel(c_ref, w_ref, b_ref, o_ref):
    c = c_ref[...]
    c = c * jax.nn.sigmoid(c)
    o_ref[...] = _dot(c, w_ref[...]) + b_ref[...]


def _mod_table(cond, mod_w, mod_b, layer):
    d = cond.shape[1]
    n = mod_w.shape[2]
    tn = _tile(n, 512, LANES)
    return pl.pallas_call(
        _mod_kernel,
        grid=(n // tn,),
        in_specs=[pl.BlockSpec((MOD_ROWS, d), lambda j: (0, 0)),
                  pl.BlockSpec((None, d, tn), lambda j: (layer, 0, j)),
                  pl.BlockSpec((None, 1, tn), lambda j: (layer, 0, j))],
        out_specs=pl.BlockSpec((MOD_ROWS, tn), lambda j: (0, j)),
        out_shape=jax.ShapeDtypeStruct((MOD_ROWS, n), F32),
        compiler_params=_params("parallel"),
        name="mod_table",
    )(cond, mod_w, mod_b.reshape(mod_b.shape[0], 1, n))


def _norm_mod_kernel(x_ref, g_ref, sh_ref, sc_ref, h_ref):
    x = x_ref[...]
    y = x * lax.rsqrt(jnp.mean(x * x, axis=-1, keepdims=True) + RMS_EPS) * g_ref[...]
    h_ref[...] = (y * (1.0 + sc_ref[...]) + sh_ref[...]).astype(h_ref.dtype)


def _top2_sum(a, b, c, d):
    hi1, lo1 = jnp.maximum(a, b), jnp.minimum(a, b)
    hi2, lo2 = jnp.maximum(c, d), jnp.minimum(c, d)
    return jnp.maximum(hi1, hi2) + jnp.maximum(jnp.minimum(hi1, hi2), jnp.maximum(lo1, lo2))


def _first_argmax(vals):
    best_v = vals[0]
    best_i = jnp.zeros(vals[0].shape, jnp.int32)
    for i in range(1, len(vals)):
        better = vals[i] > best_v
        best_i = jnp.where(better, i, best_i)
        best_v = jnp.where(better, vals[i], best_v)
    return best_i, best_v


def _pick(rows, index):
    out = rows[0]
    for i in range(1, len(rows)):
        out = jnp.where(index == i, rows[i], out)
    return out


def _norm_mod_router_kernel(x_ref, g_ref, sh_ref, sc_ref, rwt_ref, rb_ref, h_ref, idx_ref, gate_ref,
                            *, n_experts):
    x = x_ref[...]
    y = x * lax.rsqrt(jnp.mean(x * x, axis=-1, keepdims=True) + RMS_EPS) * g_ref[...]
    h = y * (1.0 + sc_ref[...]) + sh_ref[...]
    h_ref[...] = h.astype(h_ref.dtype)
    logits = lax.dot_general(rwt_ref[...], h, (((1,), (1,)), ((), ())), preferred_element_type=F32,
                             precision=HIGHEST)
    aff_all = jax.nn.sigmoid(logits)
    sel_all = aff_all + rb_ref[...]
    per_group = n_experts // N_GROUPS
    aff = [aff_all[e:e + 1, :] for e in range(n_experts)]
    sel = [sel_all[e:e + 1, :] for e in range(n_experts)]
    assert per_group == 4 and TOP_K == 2
    best, _ = _first_argmax([_top2_sum(*sel[g * per_group:(g + 1) * per_group]) for g in range(N_GROUPS)])
    cand = [_pick([sel[g * per_group + i] for g in range(N_GROUPS)], best) for i in range(per_group)]
    cand_aff = [_pick([aff[g * per_group + i] for g in range(N_GROUPS)], best) for i in range(per_group)]
    i1, _ = _first_argmax(cand)
    i2, _ = _first_argmax([jnp.where(i1 == i, -jnp.inf, cand[i]) for i in range(per_group)])
    g1 = _pick(cand_aff, i1)
    g2 = _pick(cand_aff, i2)
    idx_ref[0:1, :] = best * per_group + i1
    idx_ref[1:2, :] = best * per_group + i2
    gate_ref[0:1, :] = g1 / (g1 + g2)
    gate_ref[1:2, :] = g2 / (g1 + g2)


def _norm_mod(x, g, mod3, shift_chunk, n_rows, dims, router=None):
    d = x.shape[1]
    tm = _tile(dims["ctx_rows"], 256, 8)
    grp = dims["group_of"](tm)
    in_specs = [pl.BlockSpec((tm, d), lambda i: (i, 0)),
                pl.BlockSpec((1, d), lambda i: (0, 0)),
                pl.BlockSpec((None, 1, d), lambda i: (grp(i), 0, shift_chunk)),
                pl.BlockSpec((None, 1, d), lambda i: (grp(i), 0, shift_chunk + 1))]
    args = [x, g.reshape(1, d), mod3, mod3]
    out_specs = [pl.BlockSpec((tm, d), lambda i: (i, 0))]
    out_shape = [jax.ShapeDtypeStruct((n_rows, d), BF16)]
    kern = _norm_mod_kernel
    if router is not None:
        router_w, router_bias = router
        n_experts = router_w.shape[1]
        in_specs += [pl.BlockSpec((n_experts, d), lambda i: (0, 0)),
                     pl.BlockSpec((n_experts, 1), lambda i: (0, 0))]
        args += [router_w.T, router_bias.astype(F32).reshape(n_experts, 1)]
        out_specs += [pl.BlockSpec((TOP_K, tm), lambda i: (0, i))] * 2
        out_shape += [jax.ShapeDtypeStruct((TOP_K, n_rows), jnp.int32),
                      jax.ShapeDtypeStruct((TOP_K, n_rows), F32)]
        kern = functools.partial(_norm_mod_router_kernel, n_experts=n_experts)
    return pl.pallas_call(
        kern, grid=(n_rows // tm,), in_specs=in_specs, out_specs=out_specs, out_shape=out_shape,
        compiler_params=_params("parallel"), name="norm_mod",
    )(*args)


def _proj_kernel(a_ref, w_ref, o_ref):
    o_ref[...] = jnp.dot(a_ref[...], w_ref[...], preferred_element_type=F32).astype(o_ref.dtype)


def _proj(a, w, n_rows, out_dtype, col_off=0, n_cols=None, tm_pref=512, tn_pref=512):
    k = a.shape[1]
    n_cols = w.shape[1] - col_off if n_cols is None else n_cols
    tm = _tile(n_rows, tm_pref, 8)
    tn = _tile(int(np.gcd(n_cols, col_off)) if col_off else n_cols, tn_pref, LANES)
    off = col_off // tn
    return pl.pallas_call(
        _proj_kernel,
        grid=(n_cols // tn, n_rows // tm),
        in_specs=[pl.BlockSpec((tm, k), lambda j, i: (i, 0)),
                  pl.BlockSpec((k, tn), lambda j, i: (0, j + off))],
        out_specs=pl.BlockSpec((tm, tn), lambda j, i: (i, j)),
        out_shape=jax.ShapeDtypeStruct((n_rows, n_cols), out_dtype),
        compiler_params=_params("parallel", "parallel"), name="proj",
    )(a, w)


def _proj_res_kernel(a_ref, w_ref, x_ref, gt_ref, o_ref):
    acc = jnp.dot(a_ref[...], w_ref[...], preferred_element_type=F32)
    o_ref[...] = x_ref[...] + gt_ref[...] * acc


def _proj_residual(a, w, x, mod3, gate_chunk, n_rows, dims, tn_pref=512):
    k = a.shape[1]
    d = w.shape[1]
    tm = _tile(dims["ctx_rows"], 512, 8)
    tn = _tile(d, tn_pref, LANES)
    grp = dims["group_of"](tm)
    gblk = gate_chunk * (d // tn)
    return pl.pallas_call(
        _proj_res_kernel,
        grid=(d // tn, n_rows // tm),
        in_specs=[pl.BlockSpec((tm, k), lambda j, i: (i, 0)),
                  pl.BlockSpec((k, tn), lambda j, i: (0, j)),
                  pl.BlockSpec((tm, tn), lambda j, i: (i, j)),
                  pl.BlockSpec((None, 1, tn), lambda j, i: (grp(i), 0, gblk + j))],
        out_specs=pl.BlockSpec((tm, tn), lambda j, i: (i, j)),
        out_shape=jax.ShapeDtypeStruct((n_rows, d), F32),
        compiler_params=_params("parallel", "parallel"), name="proj_residual",
    )(a, w, x, mod3)


def _proj_qk_kernel(a_ref, w_ref, nw_ref, cos_ref, sin_ref, o_ref, *, head, scale):
    acc = jnp.dot(a_ref[...], w_ref[...], preferred_element_type=F32)
    nw = nw_ref[...]
    cs = cos_ref[...]
    sn = sin_ref[...]
    for s in range(acc.shape[1] // head):
        x = acc[:, s * head:(s + 1) * head]
        xn = x * lax.rsqrt(jnp.mean(x * x, axis=-1, keepdims=True) + RMS_EPS) * nw
        xr = xn * cs + pltpu.roll(xn, head // 2, 1) * sn
        o_ref[:, s * head:(s + 1) * head] = (xr * scale).astype(o_ref.dtype)


def _proj_qk(a, w, col_off, n_cols, norm_w, cos_t, sin_t, scale, n_rows):
    k = a.shape[1]
    head = norm_w.shape[0]
    tm = _tile(n_rows, 512, 8)
    tn = _tile(int(np.gcd(n_cols, col_off)) if col_off else n_cols, 512, head)
    off = col_off // tn
    return pl.pallas_call(
        functools.partial(_proj_qk_kernel, head=head, scale=scale),
        grid=(n_cols // tn, n_rows // tm),
        in_specs=[pl.BlockSpec((tm, k), lambda j, i: (i, 0)),
                  pl.BlockSpec((k, tn), lambda j, i: (0, j + off)),
                  pl.BlockSpec((1, head), lambda j, i: (0, 0)),
                  pl.BlockSpec((tm, head), lambda j, i: (i, 0)),
                  pl.BlockSpec((tm, head), lambda j, i: (i, 0))],
        out_specs=pl.BlockSpec((tm, tn), lambda j, i: (i, j)),
        out_shape=jax.ShapeDtypeStruct((n_rows, n_cols), BF16),
        compiler_params=_params("parallel", "parallel"), name="proj_qk",
    )(a, w, norm_w.reshape(1, head), cos_t, sin_t)


def _seg_sum(x, seg):
    n = x.shape[1]
    lane_blk = LANES if n % LANES == 0 else n
    r = lax.broadcasted_iota(jnp.int32, (lane_blk, lane_blk), 0) // seg
    c = lax.broadcasted_iota(jnp.int32, (lane_blk, lane_blk), 1) // seg
    ones_bd = (r == c).astype(F32)
    parts = [_dot_f32(x[:, s:s + lane_blk], ones_bd) for s in range(0, n, lane_blk)]
    return parts[0] if len(parts) == 1 else jnp.concatenate(parts, axis=1)


def _rwkv_prep_kernel(z_ref, zp_ref, zn_ref, t_ref, tp_ref, tn_ref, mu_ref, mut_ref, w0_ref, wup_ref,
                      a0_ref, aup_ref, gup_ref, kk_ref, ka_ref, rk_ref,
                      r_out, v_out, kkn_out, lw_out, km_out, bb_out, bon_out, gm_out,
                      *, rd, lg, lwp, seq, ctx, n_lat_rows):
    tm = z_ref.shape[0]
    row0 = pl.program_id(0) * tm
    in_lat = row0 < n_lat_rows
    seq_len = jnp.where(in_lat, seq, ctx)
    pos0 = jnp.where(in_lat, row0 % seq, (row0 - n_lat_rows) % ctx)
    has_prev = (pos0 != 0).astype(F32)
    has_next = (pos0 + tm != seq_len).astype(F32)

    def token_shift(cur_ref, prev_ref, next_ref, mix_ref):
        cur = cur_ref[...]
        rows = lax.broadcasted_iota(jnp.int32, cur.shape, 0)
        prev_row = prev_ref[7:8, :] * has_prev
        next_row = next_ref[0:1, :] * has_next
        before = jnp.where(rows == 0, prev_row, pltpu.roll(cur, 1, 0))
        after = jnp.where(rows == tm - 1, next_row, pltpu.roll(cur, tm - 1, 0))
        return cur + mix_ref[...] * (0.5 * (before + after) - cur)

    zs = token_shift(z_ref, zp_ref, zn_ref, mu_ref)
    ts = token_shift(t_ref, tp_ref, tn_ref, mut_ref)
    r = zs[:, 0:rd]
    k = zs[:, rd:2 * rd]
    v = zs[:, 2 * rd:3 * rd]
    g_down = ts[:, 0:lg]
    w_down = ts[:, lg:lg + lwp]
    a_down = ts[:, lg + lwp:lg + 2 * lwp]

    kk = k * kk_ref[...]
    kk = kk * lax.rsqrt(jnp.maximum(_seg_sum(kk * kk, RWKV_HEAD), 1e-24))
    r_out[...] = r
    v_out[...] = v
    kkn_out[...] = kk
    tw = jnp.tanh(w_down)
    bonus = jnp.zeros_like(r)
    for d in range(2):
        lw_out[d] = -np.float32(np.exp(-0.5)) * jax.nn.sigmoid(w0_ref[d] + _dot(tw, wup_ref[d]))
        a = jax.nn.sigmoid(a0_ref[d] + _dot(a_down, aup_ref[d]))
        k_mod = k * (1.0 + (a - 1.0) * ka_ref[...])
        km_out[d] = k_mod
        bb_out[d] = kk * a
        bonus = bonus + _seg_sum(r * k_mod * rk_ref[...], RWKV_HEAD) * v
    bon_out[...] = bonus
    gm_out[...] = _dot(jax.nn.sigmoid(g_down), gup_ref[...])


def _rwkv_prep(z, n_rows, lay, dims, mu_main, mu_tail, w0, w_up_p, a0, a_up_p, g_up, k_k, k_a, r_k):
    rd, lg, lwp = lay["rd"], lay["lg"], lay["lwp"]
    mw = 3 * rd
    tw = lg + 2 * lwp
    tblk = lay["tail"] // tw
    tm = _tile(dims["ctx"], 64, 8)
    nb8 = n_rows // 8
    full = lambda shape: pl.BlockSpec(shape, lambda i: (0,) * len(shape))
    row_spec = pl.BlockSpec((tm, rd), lambda i: (i, 0))
    dir_spec = pl.BlockSpec((2, tm, rd), lambda i: (0, i, 0))
    sds = jax.ShapeDtypeStruct
    kern = functools.partial(_rwkv_prep_kernel, rd=rd, lg=lg, lwp=lwp, seq=dims["seq"], ctx=dims["ctx"],
                             n_lat_rows=dims["lat_rows"])
    return pl.pallas_call(
        kern,
        grid=(n_rows // tm,),
        in_specs=[pl.BlockSpec((tm, mw), lambda i: (i, 0)),
                  pl.BlockSpec((8, mw), lambda i: (jnp.maximum(i * (tm // 8) - 1, 0), 0)),
                  pl.BlockSpec((8, mw), lambda i: (jnp.minimum((i + 1) * (tm // 8), nb8 - 1), 0)),
                  pl.BlockSpec((tm, tw), lambda i: (i, tblk)),
                  pl.BlockSpec((8, tw), lambda i: (jnp.maximum(i * (tm // 8) - 1, 0), tblk)),
                  pl.BlockSpec((8, tw), lambda i: (jnp.minimum((i + 1) * (tm // 8), nb8 - 1), tblk)),
                  full((1, mw)), full((1, tw)), full((2, 1, rd)), full((2, lwp, rd)), full((2, 1, rd)), full((2, lwp, rd)),
                  full((lg, rd)), full((1, rd)), full((1, rd)), full((1, rd))],
        out_specs=[row_spec, row_spec, row_spec, dir_spec, dir_spec, dir_spec, row_spec, row_spec],
        out_shape=[sds((n_rows, rd), F32)] * 3 + [sds((2, n_rows, rd), F32)] * 3 + [sds((n_rows, rd), F32)] * 2,
        compiler_params=_params("parallel"), name="rwkv_prep",
    )(z, z, z, z, z, z, mu_main, mu_tail, w0.reshape(2, 1, rd), w_up_p, a0.reshape(2, 1, rd), a_up_p, g_up,
      k_k.reshape(1, rd), k_a.reshape(1, rd), r_k.reshape(1, rd))


def _rwkv_scan_kernel(r_ref, v_ref, kk_ref, lw_ref, km_ref, bb_ref, y_ref, s_ref, *, heads):
    d = pl.program_id(0)
    j = pl.program_id(3)

    @pl.when(j == 0)
    def _():
        s_ref[...] = jnp.zeros_like(s_ref)

    n = RWKV_HEAD
    chunk = lw_ref.shape[0]
    t_idx = lax.broadcasted_iota(jnp.int32, (chunk, chunk), 0)
    s_idx = lax.broadcasted_iota(jnp.int32, (chunk, chunk), 1)
    lead = jnp.where(d == 0, t_idx - s_idx, s_idx - t_idx)
    incl = lead >= 0
    strict = lead > 0

    lw = lw_ref[...]
    cum = _dot_f32(incl.astype(F32), lw)
    tot = jnp.sum(lw, axis=0, keepdims=True)
    e_in = jnp.exp(cum)
    e_neg = jnp.exp(-cum)
    e_last = jnp.exp(tot - cum)
    kkn = kk_ref[...]
    km = km_ref[...]
    bb = bb_ref[...]
    r_t = r_ref[...] * e_in
    a_t = -kkn * jnp.exp(cum - lw)
    b_t = bb * e_neg
    k_t = km * e_neg
    b_l = bb * e_last
    k_l = km * e_last
    w_l = jnp.exp(tot)
    v_all = v_ref[...]

    hs = range(heads)
    sl = [slice(h * n, (h + 1) * n) for h in hs]
    eye = (t_idx == s_idx).astype(F32)
    s0 = [s_ref[h] for h in hs]
    m = [jnp.where(strict, _dot_nt(a_t[:, sl[h]], b_t[:, sl[h]]), 0.0) for h in hs]
    a_ak = [jnp.where(strict, _dot_nt(a_t[:, sl[h]], k_t[:, sl[h]]), 0.0) for h in hs]
    a_rb = [jnp.where(incl, _dot_nt(r_t[:, sl[h]], b_t[:, sl[h]]), 0.0) for h in hs]
    a_rk = [jnp.where(incl, _dot_nt(r_t[:, sl[h]], k_t[:, sl[h]]), 0.0) for h in hs]
    x = [_dot_nt(a_t[:, sl[h]], s0[h]) + _dot(a_ak[h], v_all[:, sl[h]]) for h in hs]
    y0 = [_dot_nt(r_t[:, sl[h]], s0[h]) + _dot(a_rk[h], v_all[:, sl[h]]) for h in hs]
    p = [eye + m[h] for h in hs]
    span = 2
    while span < chunk:
        m = [_dot(m[h], m[h]) for h in hs]
        p = [p[h] + _dot(m[h], p[h]) for h in hs]
        span *= 2
    u = [_dot(p[h], x[h]) for h in hs]
    y_ref[...] = jnp.concatenate([y0[h] + _dot(a_rb[h], u[h]) for h in hs], axis=1)
    for h in hs:
        s_ref[h] = (s0[h] * w_l[:, sl[h]] + _dot_tn(u[h], b_l[:, sl[h]])
                    + _dot_tn(v_all[:, sl[h]], k_l[:, sl[h]]))


def _chunk_index(d, b, j, dims):
    nc_ctx, nc_lat, nb = dims["ctx"] // CHUNK, dims["seq"] // CHUNK, dims["batch"]
    jc = jnp.where(d == 0, j, nc_ctx - 1 - j)
    jl = jnp.where(d == 0, j - nc_ctx, nc_lat - 1 - (j - nc_ctx))
    return jnp.where(j < nc_ctx, nb * nc_lat + b * nc_ctx + jc, b * nc_lat + jl)


def _rwkv_scan(r, v, kkn, lw, km, bb, dims, heads_per_step):
    n_rows, rd = r.shape
    gw = heads_per_step * RWKV_HEAD
    n_chunks = (dims["ctx"] + dims["seq"]) // CHUNK
    cidx = lambda d, b, g, j: _chunk_index(d, b, j, dims)
    row_spec = pl.BlockSpec((CHUNK, gw), lambda d, b, g, j: (cidx(d, b, g, j), g))
    dir_spec = pl.BlockSpec((None, CHUNK, gw), lambda d, b, g, j: (d, cidx(d, b, g, j), g))
    return pl.pallas_call(
        functools.partial(_rwkv_scan_kernel, heads=heads_per_step),
        grid=(2, dims["batch"], rd // gw, n_chunks),
        in_specs=[row_spec, row_spec, row_spec, dir_spec, dir_spec, dir_spec],
        out_specs=dir_spec,
        out_shape=jax.ShapeDtypeStruct((2, n_rows, rd), F32),
        scratch_shapes=[pltpu.VMEM((heads_per_step, RWKV_HEAD, RWKV_HEAD), F32)],
        compiler_params=_params("parallel", "parallel", "parallel", "arbitrary"), name="rwkv_scan",
    )(r, v, kkn, lw, km, bb)


def _mlstm_kernel(q_ref, k_ref, v_ref, gc_ref, gr_ref, bc_ref, br_ref, h_ref, c_ref, n_ref, m_ref,
                  *, heads, dqk, dv):
    d = pl.program_id(0)
    j = pl.program_id(2)

    @pl.when(j == 0)
    def _():
        c_ref[...] = jnp.zeros_like(c_ref)
        n_ref[...] = jnp.zeros_like(n_ref)
        m_ref[...] = jnp.zeros_like(m_ref)

    chunk = q_ref.shape[0]
    t_idx = lax.broadcasted_iota(jnp.int32, (chunk, chunk), 0)
    s_idx = lax.broadcasted_iota(jnp.int32, (chunk, chunk), 1)
    lead = jnp.where(d == 0, t_idx - s_idx, s_idx - t_idx)
    incl = lead >= 0
    tri = incl.astype(F32)
    tri_t = (lead <= 0).astype(F32)

    gcol = gc_ref[...] + bc_ref[...]
    grow = gr_ref[...] + br_ref[...]
    i_col = gcol[:, :heads]
    f_col = jax.nn.log_sigmoid(gcol[:, heads:])
    i_row = grow[:heads, :]
    f_row = jax.nn.log_sigmoid(grow[heads:, :])
    b_col = _dot_f32(tri, f_col)
    b_row = _dot_f32(f_row, tri_t)
    b_last = jnp.sum(f_col, axis=0, keepdims=True)
    scale = np.float32(dqk ** -0.5)

    for h in range(heads):
        q = q_ref[:, h * dqk:(h + 1) * dqk] * scale
        k = k_ref[:, h * dqk:(h + 1) * dqk]
        v = v_ref[:, h * dv:(h + 1) * dv]
        bc = b_col[:, h:h + 1]
        ic = i_col[:, h:h + 1]
        m_st = m_ref[h][:, 0:1]
        bl = b_last[:, h:h + 1]
        dmat = jnp.where(incl, bc + (i_row[h:h + 1, :] - b_row[h:h + 1, :]), -jnp.inf)
        inter = bc + m_st
        m_t = jnp.maximum(inter, jnp.max(dmat, axis=-1, keepdims=True))
        w_inter = jnp.exp(inter - m_t)
        s = _dot_nt(q, k) * jnp.exp(dmat - m_t)
        c_st = c_ref[h]
        n_st = n_ref[h]
        num = w_inter * _dot(q, c_st) + _dot(s, v)
        den = w_inter * jnp.sum(q * n_st, axis=-1, keepdims=True) + jnp.sum(s, axis=-1, keepdims=True)
        h_ref[:, h * dv:(h + 1) * dv] = num / jnp.maximum(jnp.abs(den), jnp.exp(-m_t))
        g = bl - bc + ic
        m_new = jnp.maximum(bl + m_st, jnp.max(g, axis=0, keepdims=True))
        decay = jnp.exp(bl + m_st - m_new)
        kw = k * jnp.exp(g - m_new)
        c_ref[h] = decay * c_st + _dot_tn(kw, v)
        n_ref[h] = decay * n_st + jnp.sum(kw, axis=0, keepdims=True)
        m_ref[h] = jnp.broadcast_to(m_new, m_ref.shape[1:])


def _mlstm_scan(z, lay, dims, gates_col, gates_row, bias_col, bias_row, heads, dqk, dv):
    n_rows = z.shape[0]
    n_chunks = (dims["ctx"] + dims["seq"]) // CHUNK
    md = heads * dv
    qw = heads * dqk
    cidx = lambda d, b, j: _chunk_index(d, b, j, dims)
    q_blk, k_blk, v_blk = lay["mq"] // qw, lay["mk"] // qw, lay["mv"] // md
    return pl.pallas_call(
        functools.partial(_mlstm_kernel, heads=heads, dqk=dqk, dv=dv),
        grid=(2, dims["batch"], n_chunks),
        in_specs=[pl.BlockSpec((CHUNK, qw), lambda d, b, j: (cidx(d, b, j), q_blk)),
                  pl.BlockSpec((CHUNK, qw), lambda d, b, j: (cidx(d, b, j), k_blk)),
                  pl.BlockSpec((CHUNK, md), lambda d, b, j: (cidx(d, b, j), v_blk)),
                  pl.BlockSpec((None, None, CHUNK, 2 * heads), lambda d, b, j: (d, cidx(d, b, j), 0, 0)),
                  pl.BlockSpec((None, None, 2 * heads, CHUNK), lambda d, b, j: (d, cidx(d, b, j), 0, 0)),
                  pl.BlockSpec((None, 1, 2 * heads), lambda d, b, j: (d, 0, 0)),
                  pl.BlockSpec((None, 2 * heads, 1), lambda d, b, j: (d, 0, 0))],
        out_specs=pl.BlockSpec((None, CHUNK, md), lambda d, b, j: (d, cidx(d, b, j), 0)),
        out_shape=jax.ShapeDtypeStruct((2, n_rows, md), F32),
        scratch_shapes=[pltpu.VMEM((heads, dqk, dv), F32), pltpu.VMEM((heads, 1, dqk), F32),
                        pltpu.VMEM((heads, 1, LANES), F32)],
        compiler_params=_params("parallel", "parallel", "arbitrary"), name="mlstm_scan",
    )(z, z, z, gates_col, gates_row, bias_col, bias_row)


def _even_finish_kernel(y_ref, bon_ref, gm_ref, gnw_ref, gnb_ref, h_ref, o_ref, nw_ref, out_ref,
                        *, rd, heads, dv):
    y = y_ref[0] + y_ref[1]
    inv_n = np.float32(1.0 / RWKV_HEAD)
    mean = _seg_sum(y, RWKV_HEAD) * inv_n
    yc = y - mean
    var = _seg_sum(yc * yc, RWKV_HEAD) * inv_n
    yn = yc * lax.rsqrt(var + GN_EPS) * gnw_ref[...] + gnb_ref[...] + bon_ref[...]
    out_ref[:, 0:rd] = (yn * gm_ref[...]).astype(out_ref.dtype)
    hm = h_ref[0] + h_ref[1]
    for h in range(heads):
        sl = slice(h * dv, (h + 1) * dv)
        x = hm[:, sl]
        xn = x * lax.rsqrt(jnp.mean(x * x, axis=-1, keepdims=True) + RMS_EPS) * nw_ref[:, sl]
        out_ref[:, rd + h * dv:rd + (h + 1) * dv] = (jax.nn.sigmoid(o_ref[:, sl]) * xn).astype(out_ref.dtype)


def _even_finish(y, bonus, gmul, gn_w, gn_b, hm, z, lay, ml_norm, n_rows, dims, heads, dv):
    rd = y.shape[2]
    md = hm.shape[2]
    tm = _tile(dims["ctx_rows"], 256, 8)
    o_blk = lay["mo"] // md
    return pl.pallas_call(
        functools.partial(_even_finish_kernel, rd=rd, heads=heads, dv=dv),
        grid=(n_rows // tm,),
        in_specs=[pl.BlockSpec((2, tm, rd), lambda i: (0, i, 0)),
                  pl.BlockSpec((tm, rd), lambda i: (i, 0)),
                  pl.BlockSpec((tm, rd), lambda i: (i, 0)),
                  pl.BlockSpec((1, rd), lambda i: (0, 0)),
                  pl.BlockSpec((1, rd), lambda i: (0, 0)),
                  pl.BlockSpec((2, tm, md), lambda i: (0, i, 0)),
                  pl.BlockSpec((tm, md), lambda i: (i, o_blk)),
                  pl.BlockSpec((1, md), lambda i: (0, 0))],
        out_specs=pl.BlockSpec((tm, rd + md), lambda i: (i, 0)),
        out_shape=jax.ShapeDtypeStruct((n_rows, rd + md), BF16),
        compiler_params=_params("parallel"), name="even_finish",
    )(y, bonus, gmul, gn_w.reshape(1, rd), gn_b.reshape(1, rd), hm, z, ml_norm.reshape(1, md))


def _attn_kernel(q_ref, kl_ref, kc_ref, vl_ref, vc_ref, o_ref, *, group, head):
    kl = kl_ref[...]
    kc = kc_ref[...]
    vl = vl_ref[...]
    vc = vc_ref[...]
    for g in range(group):
        sl = slice(g * head, (g + 1) * head)
        q = q_ref[:, sl]
        s_l = _dot_nt(q, kl)
        s_c = _dot_nt(q, kc)
        m = jnp.maximum(jnp.max(s_l, axis=-1, keepdims=True), jnp.max(s_c, axis=-1, keepdims=True))
        p_l = jnp.exp(s_l - m)
        p_c = jnp.exp(s_c - m)
        den = jnp.sum(p_l, axis=-1, keepdims=True) + jnp.sum(p_c, axis=-1, keepdims=True)
        o = _dot(p_l, vl) + _dot(p_c, vc)
        o_ref[:, sl] = (o / den).astype(o_ref.dtype)


def _attention(q, k, v, dims, head, group):
    seq, ctx, nb = dims["seq"], dims["ctx"], dims["batch"]
    kvh = k.shape[1] // head
    tq = _tile(seq, 256, 8)
    nq = seq // tq
    gw = group * head
    return pl.pallas_call(
        functools.partial(_attn_kernel, group=group, head=head),
        grid=(nb, kvh, nq),
        in_specs=[pl.BlockSpec((tq, gw), lambda b, h, i: (b * nq + i, h)),
                  pl.BlockSpec((seq, head), lambda b, h, i: (b, h)),
                  pl.BlockSpec((ctx, head), lambda b, h, i: (nb * (seq // ctx) + b, h)),
                  pl.BlockSpec((seq, head), lambda b, h, i: (b, h)),
                  pl.BlockSpec((ctx, head), lambda b, h, i: (nb * (seq // ctx) + b, h))],
        out_specs=pl.BlockSpec((tq, gw), lambda b, h, i: (b * nq + i, h)),
        out_shape=jax.ShapeDtypeStruct(q.shape, BF16),
        compiler_params=_params("parallel", "parallel", "parallel"), name="attention",
    )(q, k, k, v, v)


def _moe_kernel(be_ref, nu_ref, x_ref, w1_ref, w3_ref, w2_ref, o_ref):
    i = pl.program_id(0)
    kt = pl.program_id(1)
    used = i < nu_ref[0]

    @pl.when(used)
    def _():
        x = x_ref[...]
        a = jnp.dot(x, w1_ref[...], preferred_element_type=F32)
        b = jnp.dot(x, w3_ref[...], preferred_element_type=F32)
        mid = (a * jax.nn.sigmoid(a) * b).astype(BF16)
        part = jnp.dot(mid, w2_ref[...], preferred_element_type=F32)

        @pl.when(kt == 0)
        def _():
            o_ref[...] = part

        @pl.when(kt > 0)
        def _():
            o_ref[...] += part

    @pl.when(jnp.logical_and(jnp.logical_not(used), kt == 0))
    def _():
        o_ref[...] = jnp.zeros_like(o_ref)


def _moe_ffn(xs, block_expert, n_used, w1, w3, w2, bm):
    n_rows, d = xs.shape
    de = w1.shape[2]
    tde = _tile(de, 512, LANES)
    grid_spec = pltpu.PrefetchScalarGridSpec(
        num_scalar_prefetch=2,
        grid=(n_rows // bm, de // tde),
        in_specs=[pl.BlockSpec((bm, d), lambda i, k, be, nu: (i, 0)),
                  pl.BlockSpec((None, d, tde), lambda i, k, be, nu: (be[i], 0, k)),
                  pl.BlockSpec((None, d, tde), lambda i, k, be, nu: (be[i], 0, k)),
                  pl.BlockSpec((None, tde, d), lambda i, k, be, nu: (be[i], k, 0))],
        out_specs=pl.BlockSpec((bm, d), lambda i, k, be, nu: (i, 0)),
    )
    return pl.pallas_call(
        _moe_kernel, grid_spec=grid_spec,
        out_shape=jax.ShapeDtypeStruct((n_rows, d), F32),
        compiler_params=_params("parallel", "arbitrary"), name="moe_ffn",
    )(block_expert, n_used, xs, w1, w3, w2)


def _dispatch_plan(idx, n_experts, bm):
    n_tok = idx.shape[1]
    n_assign = n_tok * TOP_K
    expert = idx.T.reshape(-1)
    onehot = (expert[:, None] == jnp.arange(n_experts)[None, :]).astype(jnp.int32)
    rank = jnp.take_along_axis(jnp.cumsum(onehot, axis=0) - onehot, expert[:, None], 1)[:, 0]
    counts = jnp.sum(onehot, axis=0)
    padded = (counts + bm - 1) // bm * bm
    pad_end = jnp.cumsum(padded)
    pos = (pad_end - padded)[expert] + rank
    n_blocks = -(-n_assign // bm) + n_experts
    token = jnp.repeat(jnp.arange(n_tok, dtype=jnp.int32), TOP_K)
    row_token = jnp.zeros((n_blocks * bm,), jnp.int32).at[pos].set(token)
    block_expert = jnp.minimum(jnp.searchsorted(pad_end, jnp.arange(n_blocks) * bm, side='right'),
                               n_experts - 1).astype(jnp.int32)
    n_used = (pad_end[-1] // bm).astype(jnp.int32).reshape(1)
    block_expert = jnp.where(jnp.arange(n_blocks) < n_used[0], block_expert,
                             block_expert[jnp.maximum(n_used[0] - 1, 0)])
    return pos.reshape(n_tok, TOP_K), row_token, block_expert, n_used


def _ffn_res_kernel(x_ref, f0_ref, f1_ref, g_ref, gt_ref, o_ref):
    g = g_ref[...]
    f = g[:, 0:1] * f0_ref[...] + g[:, 1:2] * f1_ref[...]
    o_ref[...] = x_ref[...] + gt_ref[...] * f


def _ffn_res_norm_kernel(x_ref, f0_ref, f1_ref, g_ref, gt_ref, nw_ref, o_ref):
    g = g_ref[...]
    f = g[:, 0:1] * f0_ref[...] + g[:, 1:2] * f1_ref[...]
    x = x_ref[...] + gt_ref[...] * f
    o_ref[...] = x * lax.rsqrt(jnp.mean(x * x, axis=-1, keepdims=True) + RMS_EPS) * nw_ref[...]


def _ffn_residual(x, f0, f1, gate, mod3, gate_chunk, n_rows, dims, final_norm=None):
    d = x.shape[1]
    tm = _tile(dims["ctx_rows"], 256, 8)
    grp = dims["group_of"](tm)
    row = pl.BlockSpec((tm, d), lambda i: (i, 0))
    in_specs = [row, row, row, pl.BlockSpec((tm, TOP_K), lambda i: (i, 0)),
                pl.BlockSpec((None, 1, d), lambda i: (grp(i), 0, gate_chunk))]
    args = [x, f0, f1, gate, mod3]
    kern = _ffn_res_kernel
    if final_norm is not None:
        in_specs.append(pl.BlockSpec((1, d), lambda i: (0, 0)))
        args.append(final_norm.reshape(1, d))
        kern = _ffn_res_norm_kernel
    return pl.pallas_call(
        kern, grid=(n_rows // tm,), in_specs=in_specs, out_specs=row,
        out_shape=jax.ShapeDtypeStruct((n_rows, d), F32),
        compiler_params=_params("parallel"), name="ffn_residual",
    )(*args)


def _even_layout(rd, lw, la, lg, mh, dqk, md):
    lwp = _rup(lw + 4 * mh, LANES)
    assert _rup(la, LANES) == lwp
    lay = {"rd": rd, "lg": lg, "lwp": lwp}
    off = 3 * rd
    for name, width in (("mq", mh * dqk), ("mk", mh * dqk), ("mv", md), ("mo", md), ("tail", lg + 2 * lwp)):
        assert off % width == 0
        lay[name] = off
        off += width
    lay["width"] = off
    return lay


def _pad_cols(seg, width):
    return jnp.pad(seg, ((0, 0), (0, width - seg.shape[1])))


def _pack_even_weights(w_in, mu, lay, lw, la, mh, dqk, md):
    rd, lg, lwp = lay["rd"], lay["lg"], lay["lwp"]
    rww = 3 * rd + lw + la + lg
    o = 3 * rd
    w_down, a_down, g_down = w_in[:, o:o + lw], w_in[:, o + lw:o + lw + la], w_in[:, o + lw + la:rww]
    m_end = rww + 2 * mh * dqk + 2 * md
    w_p = jnp.concatenate([w_in[:, :3 * rd], w_in[:, rww:m_end], g_down,
                           _pad_cols(jnp.concatenate([w_down, w_in[:, m_end:]], 1), lwp),
                           _pad_cols(a_down, lwp)], axis=1).astype(BF16)
    mu2 = mu.reshape(1, -1)
    mu_tail = jnp.concatenate([mu2[:, o + lw + la:rww], _pad_cols(mu2[:, o:o + lw], lwp),
                               _pad_cols(mu2[:, o + lw:o + lw + la], lwp)], axis=1)
    return w_p, mu2[:, :3 * rd], mu_tail


def _pad_rows(w, rows):
    return jnp.pad(w, ((0, 0), (0, rows - w.shape[1]), (0, 0)))


def _rope_tables(dims, head):
    seq, nb = dims["seq"], dims["batch"]
    pairs = head // 4
    rows = seq // GRID_W
    row = jnp.repeat(jnp.arange(rows), GRID_W).astype(F32)
    col = jnp.tile(jnp.arange(GRID_W), rows).astype(F32)
    inv = ROPE_THETA ** (-jnp.arange(pairs, dtype=F32) / pairs)
    ang = jnp.concatenate([row[:, None] * inv, col[:, None] * inv], -1)
    cos, sin = jnp.cos(ang), jnp.sin(ang)
    cos_t = jnp.tile(jnp.concatenate([cos, cos], -1), (nb, 1))
    sin_t = jnp.tile(jnp.concatenate([-sin, sin], -1), (nb, 1))
    n_ctx = dims["ctx_rows"]
    return (jnp.concatenate([cos_t, jnp.ones((n_ctx, head), F32)], 0),
            jnp.concatenate([sin_t, jnp.zeros((n_ctx, head), F32)], 0))


def kernel(x, c, ctx, c_ctx, mod_w, mod_b, norm_mix, norm_ffn, norm_final, router_w, router_bias,
           exp_w1, exp_w3, exp_w2, ev_w_in, ev_w_out, rw_mu, rw_w0, rw_w_up, rw_a0, rw_a_up, rw_g_up,
           rw_k_k, rw_k_a, rw_r_k, rw_gn_w, rw_gn_b, ml_gate_b, ml_norm, at_w_qkv, at_q_norm, at_k_norm,
           at_w_o):
    nb, seq, d = x.shape
    n_ctx = ctx.shape[1]
    depth = mod_w.shape[0]
    lat_rows, ctx_rows = nb * seq, nb * n_ctx
    all_rows = lat_rows + ctx_rows
    assert nb + 1 <= MOD_ROWS and seq % n_ctx == 0 and n_ctx % CHUNK == 0
    dims = {"batch": nb, "seq": seq, "ctx": n_ctx, "lat_rows": lat_rows, "ctx_rows": ctx_rows,
            "group_of": lambda tm: (lambda i: jnp.minimum(i * tm // seq, nb))}

    n_experts = router_w.shape[1]
    rd = rw_w0.shape[-1]
    lw, la, lg = rw_w_up.shape[2], rw_a_up.shape[2], rw_g_up.shape[1]
    mh = ml_gate_b.shape[-1]
    md = ml_norm.shape[-1]
    dv = md // mh
    dqk = dv // 2
    head = at_q_norm.shape[-1]
    kv_dim = (at_w_qkv.shape[-1] - d) // 2
    group = (d // head) // (kv_dim // head)
    moe_bm = 256

    xa = jnp.concatenate([x.reshape(lat_rows, d), ctx.reshape(ctx_rows, d)], axis=0)
    cond = jnp.zeros((MOD_ROWS, d), F32).at[:nb].set(c).at[nb].set(c_ctx)
    lay = _even_layout(rd, lw, la, lg, mh, dqk, md)
    out = None

    for layer in range(depth):
        ctx_out = layer < depth - 1
        j = layer // 2
        rows_out = all_rows if ctx_out else lat_rows
        mod3 = _mod_table(cond, mod_w, mod_b, layer).reshape(MOD_ROWS, 1, 6 * d)

        h = _norm_mod(xa, norm_mix[layer], mod3, 0, all_rows, dims)[0]
        if layer % 2 == 0:
            w_p, mu_main, mu_tail = _pack_even_weights(ev_w_in[j], rw_mu[j], lay, lw, la, mh, dqk, md)
            z = _proj(h, w_p, all_rows, F32)
            r, v, kkn, lwd, km, bb, bonus, gmul = _rwkv_prep(
                z, all_rows, lay, dims, mu_main, mu_tail, rw_w0[j], _pad_rows(rw_w_up[j], lay["lwp"]).astype(BF16),
                rw_a0[j], _pad_rows(rw_a_up[j], lay["lwp"]).astype(BF16), rw_g_up[j].astype(BF16),
                rw_k_k[j], rw_k_a[j], rw_r_k[j])
            y = _rwkv_scan(r, v, kkn, lwd, km, bb, dims, heads_per_step=min(16, rd // RWKV_HEAD))
            g_off = lay["tail"] + lay["lg"] + lw
            gates = z[:, g_off:g_off + 4 * mh].reshape(all_rows, 2, 2 * mh)
            gates_col = jnp.moveaxis(gates, 1, 0).reshape(2, all_rows // CHUNK, CHUNK, 2 * mh)
            gates_row = jnp.swapaxes(gates_col, 2, 3)
            bias = ml_gate_b[j].reshape(2, 2 * mh)
            hm = _mlstm_scan(z, lay, dims, gates_col, gates_row, bias.reshape(2, 1, 2 * mh),
                             bias.reshape(2, 2 * mh, 1), mh, dqk, dv)
            mix = _even_finish(y, bonus, gmul, rw_gn_w[j], rw_gn_b[j], hm, z, lay, ml_norm[j],
                               rows_out, dims, mh, dv)
            xa_new = _proj_residual(mix, ev_w_out[j].astype(BF16), xa, mod3, 2, rows_out, dims)
        else:
            w_qkv = at_w_qkv[j].astype(BF16)
            cos_t, sin_t = _rope_tables(dims, head)
            q = _proj_qk(h, w_qkv, 0, d, at_q_norm[j], cos_t, sin_t, np.float32(head ** -0.5), lat_rows)
            k = _proj_qk(h, w_qkv, d, kv_dim, at_k_norm[j], cos_t, sin_t, np.float32(1.0), all_rows)
            v = _proj(h, w_qkv, all_rows, BF16, col_off=d + kv_dim, n_cols=kv_dim)
            att = _attention(q, k, v, dims, head, group)
            xa_new = _proj_residual(att, at_w_o[j].astype(BF16), xa, mod3, 2, lat_rows, dims)
            if ctx_out:
                raise NotImplementedError("context output of an attention layer")
        xa = xa_new

        h2, idx, gate_t = _norm_mod(xa, norm_ffn[layer], mod3, 3, rows_out, dims,
                                    router=(router_w, router_bias))
        gate = gate_t.T
        pos, row_token, block_expert, n_used = _dispatch_plan(idx, n_experts, moe_bm)
        xs = jnp.take(h2, row_token, axis=0)
        ys = _moe_ffn(xs, block_expert, n_used, exp_w1[layer].astype(BF16), exp_w3[layer].astype(BF16),
                      exp_w2[layer].astype(BF16), moe_bm)
        f0 = jnp.take(ys, pos[:, 0], axis=0)
        f1 = jnp.take(ys, pos[:, 1], axis=0)
        last = layer == depth - 1
        xa = _ffn_residual(xa, f0, f1, gate, mod3, 5, rows_out, dims,
                           final_norm=norm_final if last else None)
        if last:
            out = xa[:lat_rows].reshape(nb, seq, d)
    return out
```

```python
import functools

import jax
import jax.numpy as jnp
import numpy as np
from jax import lax
from jax.experimental import pallas as pl
from jax.experimental.pallas import tpu as pltpu

F32 = jnp.float32
BF16 = jnp.bfloat16
HIGHEST = lax.Precision.HIGHEST

GRID_W = 64
RMS_EPS = 1e-6
GN_EPS = 64e-5
RWKV_HEAD = 64
CHUNK = 64
N_GROUPS = 4
TOP_K = 2
ROPE_THETA = 10000.0
LANES = 128
MOD_ROWS = 8
VMEM_LIMIT = 56 * 1024 * 1024


def _rup(n, m):
    return (n + m - 1) // m * m


def _tile(n, pref, quantum):
    t = min(pref, n) // quantum * quantum
    while t >= quantum:
        if n % t == 0:
            return t
        t -= quantum
    return n


def _params(*sem):
    return pltpu.CompilerParams(dimension_semantics=sem, vmem_limit_bytes=VMEM_LIMIT)


def _dot(a, b):
    return jnp.dot(a.astype(BF16), b.astype(BF16), preferred_element_type=F32)


def _dot_nt(a, b):
    return lax.dot_general(a.astype(BF16), b.astype(BF16), (((1,), (1,)), ((), ())),
                           preferred_element_type=F32)


def _dot_tn(a, b):
    return lax.dot_general(a.astype(BF16), b.astype(BF16), (((0,), (0,)), ((), ())),
                           preferred_element_type=F32)


def _dot_f32(a, b):
    return jnp.dot(a, b, preferred_element_type=F32, precision=HIGHEST)


def _mod_kernel(c_ref, w_ref, b_ref, o_ref):
    c = c_ref[...]
    c = c * jax.nn.sigmoid(c)
    o_ref[...] = _dot(c, w_ref[...]) + b_ref[...]


def _mod_table(cond, mod_w, mod_b, layer):
    d = cond.shape[1]
    n = mod_w.shape[2]
    tn = _tile(n, 512, LANES)
    return pl.pallas_call(
        _mod_kernel,
        grid=(n // tn,),
        in_specs=[pl.BlockSpec((MOD_ROWS, d), lambda j: (0, 0)),
                  pl.BlockSpec((None, d, tn), lambda j: (layer, 0, j)),
                  pl.BlockSpec((None, 1, tn), lambda j: (layer, 0, j))],
        out_specs=pl.BlockSpec((MOD_ROWS, tn), lambda j: (0, j)),
        out_shape=jax.ShapeDtypeStruct((MOD_ROWS, n), F32),
        compiler_params=_params("parallel"),
        name="mod_table",
    )(cond, mod_w, mod_b.reshape(mod_b.shape[0], 1, n))


def _norm_mod_kernel(x_ref, g_ref, sh_ref, sc_ref, h_ref):
    x = x_ref[...]
    y = x * lax.rsqrt(jnp.mean(x * x, axis=-1, keepdims=True) + RMS_EPS) * g_ref[...]
    h_ref[...] = (y * (1.0 + sc_ref[...]) + sh_ref[...]).astype(h_ref.dtype)


def _top2_sum(a, b, c, d):
    hi1, lo1 = jnp.maximum(a, b), jnp.minimum(a, b)
    hi2, lo2 = jnp.maximum(c, d), jnp.minimum(c, d)
    return jnp.maximum(hi1, hi2) + jnp.maximum(jnp.minimum(hi1, hi2), jnp.maximum(lo1, lo2))


def _first_argmax(vals):
    best_v = vals[0]
    best_i = jnp.zeros(vals[0].shape, jnp.int32)
    for i in range(1, len(vals)):
        better = vals[i] > best_v
        best_i = jnp.where(better, i, best_i)
        best_v = jnp.where(better, vals[i], best_v)
    return best_i, best_v


def _pick(rows, index):
    out = rows[0]
    for i in range(1, len(rows)):
        out = jnp.where(index == i, rows[i], out)
    return out


def _norm_mod_router_kernel(x_ref, g_ref, sh_ref, sc_ref, rwt_ref, rb_ref, h_ref, idx_ref, gate_ref,
                            *, n_experts):
    x = x_ref[...]
    y = x * lax.rsqrt(jnp.mean(x * x, axis=-1, keepdims=True) + RMS_EPS) * g_ref[...]
    h = y * (1.0 + sc_ref[...]) + sh_ref[...]
    h_ref[...] = h.astype(h_ref.dtype)
    logits = lax.dot_general(rwt_ref[...], h, (((1,), (1,)), ((), ())), preferred_element_type=F32,
                             precision=HIGHEST)
    aff_all = jax.nn.sigmoid(logits)
    sel_all = aff_all + rb_ref[...]
    per_group = n_experts // N_GROUPS
    aff = [aff_all[e:e + 1, :] for e in range(n_experts)]
    sel = [sel_all[e:e + 1, :] for e in range(n_experts)]
    assert per_group == 4 and TOP_K == 2
    best, _ = _first_argmax([_top2_sum(*sel[g * per_group:(g + 1) * per_group]) for g in range(N_GROUPS)])
    cand = [_pick([sel[g * per_group + i] for g in range(N_GROUPS)], best) for i in range(per_group)]
    cand_aff = [_pick([aff[g * per_group + i] for g in range(N_GROUPS)], best) for i in range(per_group)]
    i1, _ = _first_argmax(cand)
    i2, _ = _first_argmax([jnp.where(i1 == i, -jnp.inf, cand[i]) for i in range(per_group)])
    g1 = _pick(cand_aff, i1)
    g2 = _pick(cand_aff, i2)
    idx_ref[0:1, :] = best * per_group + i1
    idx_ref[1:2, :] = best * per_group + i2
    gate_ref[0:1, :] = g1 / (g1 + g2)
    gate_ref[1:2, :] = g2 / (g1 + g2)


def _norm_mod(x, g, mod3, shift_chunk, n_rows, dims, router=None):
    d = x.shape[1]
    tm = _tile(dims["ctx_rows"], 256, 8)
    grp = dims["group_of"](tm)
    in_specs = [pl.BlockSpec((tm, d), lambda i: (i, 0)),
                pl.BlockSpec((1, d), lambda i: (0, 0)),
                pl.BlockSpec((None, 1, d), lambda i: (grp(i), 0, shift_chunk)),
                pl.BlockSpec((None, 1, d), lambda i: (grp(i), 0, shift_chunk + 1))]
    args = [x, g.reshape(1, d), mod3, mod3]
    out_specs = [pl.BlockSpec((tm, d), lambda i: (i, 0))]
    out_shape = [jax.ShapeDtypeStruct((n_rows, d), BF16)]
    kern = _norm_mod_kernel
    if router is not None:
        router_w, router_bias = router
        n_experts = router_w.shape[1]
        in_specs += [pl.BlockSpec((n_experts, d), lambda i: (0, 0)),
                     pl.BlockSpec((n_experts, 1), lambda i: (0, 0))]
        args += [router_w.T, router_bias.astype(F32).reshape(n_experts, 1)]
        out_specs += [pl.BlockSpec((TOP_K, tm), lambda i: (0, i))] * 2
        out_shape += [jax.ShapeDtypeStruct((TOP_K, n_rows), jnp.int32),
                      jax.ShapeDtypeStruct((TOP_K, n_rows), F32)]
        kern = functools.partial(_norm_mod_router_kernel, n_experts=n_experts)
    return pl.pallas_call(
        kern, grid=(n_rows // tm,), in_specs=in_specs, out_specs=out_specs, out_shape=out_shape,
        compiler_params=_params("parallel"), name="norm_mod",
    )(*args)


def _proj_kernel(a_ref, w_ref, o_ref):
    o_ref[...] = jnp.dot(a_ref[...], w_ref[...], preferred_element_type=F32).astype(o_ref.dtype)


def _proj(a, w, n_rows, out_dtype, col_off=0, n_cols=None, tm_pref=512, tn_pref=512):
    k = a.shape[1]
    n_cols = w.shape[1] - col_off if n_cols is None else n_cols
    tm = _tile(n_rows, tm_pref, 8)
    tn = _tile(int(np.gcd(n_cols, col_off)) if col_off else n_cols, tn_pref, LANES)
    off = col_off // tn
    return pl.pallas_call(
        _proj_kernel,
        grid=(n_cols // tn, n_rows // tm),
        in_specs=[pl.BlockSpec((tm, k), lambda j, i: (i, 0)),
                  pl.BlockSpec((k, tn), lambda j, i: (0, j + off))],
        out_specs=pl.BlockSpec((tm, tn), lambda j, i: (i, j)),
        out_shape=jax.ShapeDtypeStruct((n_rows, n_cols), out_dtype),
        compiler_params=_params("parallel", "parallel"), name="proj",
    )(a, w)


def _proj_res_kernel(a_ref, w_ref, x_ref, gt_ref, o_ref):
    acc = jnp.dot(a_ref[...], w_ref[...], preferred_element_type=F32)
    o_ref[...] = x_ref[...] + gt_ref[...] * acc


def _proj_residual(a, w, x, mod3, gate_chunk, n_rows, dims, tn_pref=512):
    k = a.shape[1]
    d = w.shape[1]
    tm = _tile(dims["ctx_rows"], 512, 8)
    tn = _tile(d, tn_pref, LANES)
    grp = dims["group_of"](tm)
    gblk = gate_chunk * (d // tn)
    return pl.pallas_call(
        _proj_res_kernel,
        grid=(d // tn, n_rows // tm),
        in_specs=[pl.BlockSpec((tm, k), lambda j, i: (i, 0)),
                  pl.BlockSpec((k, tn), lambda j, i: (0, j)),
                  pl.BlockSpec((tm, tn), lambda j, i: (i, j)),
                  pl.BlockSpec((None, 1, tn), lambda j, i: (grp(i), 0, gblk + j))],
        out_specs=pl.BlockSpec((tm, tn), lambda j, i: (i, j)),
        out_shape=jax.ShapeDtypeStruct((n_rows, d), F32),
        compiler_params=_params("parallel", "parallel"), name="proj_residual",
    )(a, w, x, mod3)


def _proj_qk_kernel(a_ref, w_ref, nw_ref, cos_ref, sin_ref, o_ref, *, head, scale):
    acc = jnp.dot(a_ref[...], w_ref[...], preferred_element_type=F32)
    nw = nw_ref[...]
    cs = cos_ref[...]
    sn = sin_ref[...]
    for s in range(acc.shape[1] // head):
        x = acc[:, s * head:(s + 1) * head]
        xn = x * lax.rsqrt(jnp.mean(x * x, axis=-1, keepdims=True) + RMS_EPS) * nw
        xr = xn * cs + pltpu.roll(xn, head // 2, 1) * sn
        o_ref[:, s * head:(s + 1) * head] = (xr * scale).astype(o_ref.dtype)


def _proj_qk(a, w, col_off, n_cols, norm_w, cos_t, sin_t, scale, n_rows):
    k = a.shape[1]
    head = norm_w.shape[0]
    tm = _tile(n_rows, 512, 8)
    tn = _tile(int(np.gcd(n_cols, col_off)) if col_off else n_cols, 512, head)
    off = col_off // tn
    return pl.pallas_call(
        functools.partial(_proj_qk_kernel, head=head, scale=scale),
        grid=(n_cols // tn, n_rows // tm),
        in_specs=[pl.BlockSpec((tm, k), lambda j, i: (i, 0)),
                  pl.BlockSpec((k, tn), lambda j, i: (0, j + off)),
                  pl.BlockSpec((1, head), lambda j, i: (0, 0)),
                  pl.BlockSpec((tm, head), lambda j, i: (i, 0)),
                  pl.BlockSpec((tm, head), lambda j, i: (i, 0))],
        out_specs=pl.BlockSpec((tm, tn), lambda j, i: (i, j)),
        out_shape=jax.ShapeDtypeStruct((n_rows, n_cols), BF16),
        compiler_params=_params("parallel", "parallel"), name="proj_qk",
    )(a, w, norm_w.reshape(1, head), cos_t, sin_t)


def _seg_sum(x, seg):
    n = x.shape[1]
    lane_blk = LANES if n % LANES == 0 else n
    r = lax.broadcasted_iota(jnp.int32, (lane_blk, lane_blk), 0) // seg
    c = lax.broadcasted_iota(jnp.int32, (lane_blk, lane_blk), 1) // seg
    ones_bd = (r == c).astype(F32)
    parts = [_dot_f32(x[:, s:s + lane_blk], ones_bd) for s in range(0, n, lane_blk)]
    return parts[0] if len(parts) == 1 else jnp.concatenate(parts, axis=1)


def _rwkv_prep_kernel(z_ref, zp_ref, zn_ref, t_ref, tp_ref, tn_ref, mu_ref, mut_ref, w0_ref, wup_ref,
                      a0_ref, aup_ref, gup_ref, kk_ref, ka_ref, rk_ref,
                      r_out, v_out, kkn_out, lw_out, km_out, bb_out, bon_out, gm_out,
                      *, rd, lg, lwp, seq, ctx, n_lat_rows):
    tm = z_ref.shape[0]
    row0 = pl.program_id(0) * tm
    in_lat = row0 < n_lat_rows
    seq_len = jnp.where(in_lat, seq, ctx)
    pos0 = jnp.where(in_lat, row0 % seq, (row0 - n_lat_rows) % ctx)
    has_prev = (pos0 != 0).astype(F32)
    has_next = (pos0 + tm != seq_len).astype(F32)

    def token_shift(cur_ref, prev_ref, next_ref, mix_ref):
        cur = cur_ref[...]
        rows = lax.broadcasted_iota(jnp.int32, cur.shape, 0)
        prev_row = prev_ref[7:8, :] * has_prev
        next_row = next_ref[0:1, :] * has_next
        before = jnp.where(rows == 0, prev_row, pltpu.roll(cur, 1, 0))
        after = jnp.where(rows == tm - 1, next_row, pltpu.roll(cur, tm - 1, 0))
        return cur + mix_ref[...] * (0.5 * (before + after) - cur)

    zs = token_shift(z_ref, zp_ref, zn_ref, mu_ref)
    ts = token_shift(t_ref, tp_ref, tn_ref, mut_ref)
    r = zs[:, 0:rd]
    k = zs[:, rd:2 * rd]
    v = zs[:, 2 * rd:3 * rd]
    g_down = ts[:, 0:lg]
    w_down = ts[:, lg:lg + lwp]
    a_down = ts[:, lg + lwp:lg + 2 * lwp]

    kk = k * kk_ref[...]
    kk = kk * lax.rsqrt(jnp.maximum(_seg_sum(kk * kk, RWKV_HEAD), 1e-24))
    r_out[...] = r
    v_out[...] = v
    kkn_out[...] = kk
    tw = jnp.tanh(w_down)
    bonus = jnp.zeros_like(r)
    for d in range(2):
        lw_out[d] = -np.float32(np.exp(-0.5)) * jax.nn.sigmoid(w0_ref[d] + _dot(tw, wup_ref[d]))
        a = jax.nn.sigmoid(a0_ref[d] + _dot(a_down, aup_ref[d]))
        k_mod = k * (1.0 + (a - 1.0) * ka_ref[...])
        km_out[d] = k_mod
        bb_out[d] = kk * a
        bonus = bonus + _seg_sum(r * k_mod * rk_ref[...], RWKV_HEAD) * v
    bon_out[...] = bonus
    gm_out[...] = _dot(jax.nn.sigmoid(g_down), gup_ref[...])


def _rwkv_prep(z, n_rows, lay, dims, mu_main, mu_tail, w0, w_up_p, a0, a_up_p, g_up, k_k, k_a, r_k):
    rd, lg, lwp = lay["rd"], lay["lg"], lay["lwp"]
    mw = 3 * rd
    tw = lg + 2 * lwp
    tblk = lay["tail"] // tw
    tm = _tile(dims["ctx"], 64, 8)
    nb8 = n_rows // 8
    full = lambda shape: pl.BlockSpec(shape, lambda i: (0,) * len(shape))
    row_spec = pl.BlockSpec((tm, rd), lambda i: (i, 0))
    dir_spec = pl.BlockSpec((2, tm, rd), lambda i: (0, i, 0))
    sds = jax.ShapeDtypeStruct
    kern = functools.partial(_rwkv_prep_kernel, rd=rd, lg=lg, lwp=lwp, seq=dims["seq"], ctx=dims["ctx"],
                             n_lat_rows=dims["lat_rows"])
    return pl.pallas_call(
        kern,
        grid=(n_rows // tm,),
        in_specs=[pl.BlockSpec((tm, mw), lambda i: (i, 0)),
                  pl.BlockSpec((8, mw), lambda i: (jnp.maximum(i * (tm // 8) - 1, 0), 0)),
                  pl.BlockSpec((8, mw), lambda i: (jnp.minimum((i + 1) * (tm // 8), nb8 - 1), 0)),
                  pl.BlockSpec((tm, tw), lambda i: (i, tblk)),
                  pl.BlockSpec((8, tw), lambda i: (jnp.maximum(i * (tm // 8) - 1, 0), tblk)),
                  pl.BlockSpec((8, tw), lambda i: (jnp.minimum((i + 1) * (tm // 8), nb8 - 1), tblk)),
                  full((1, mw)), full((1, tw)), full((2, 1, rd)), full((2, lwp, rd)), full((2, 1, rd)), full((2, lwp, rd)),
                  full((lg, rd)), full((1, rd)), full((1, rd)), full((1, rd))],
        out_specs=[row_spec, row_spec, row_spec, dir_spec, dir_spec, dir_spec, row_spec, row_spec],
        out_shape=[sds((n_rows, rd), F32)] * 3 + [sds((2, n_rows, rd), F32)] * 3 + [sds((n_rows, rd), F32)] * 2,
        compiler_params=_params("parallel"), name="rwkv_prep",
    )(z, z, z, z, z, z, mu_main, mu_tail, w0.reshape(2, 1, rd), w_up_p, a0.reshape(2, 1, rd), a_up_p, g_up,
      k_k.reshape(1, rd), k_a.reshape(1, rd), r_k.reshape(1, rd))


def _rwkv_scan_kernel(r_ref, v_ref, kk_ref, lw_ref, km_ref, bb_ref, y_ref, s_ref, *, heads):
    d = pl.program_id(0)
    j = pl.program_id(3)

    @pl.when(j == 0)
    def _():
        s_ref[...] = jnp.zeros_like(s_ref)

    n = RWKV_HEAD
    chunk = lw_ref.shape[0]
    t_idx = lax.broadcasted_iota(jnp.int32, (chunk, chunk), 0)
    s_idx = lax.broadcasted_iota(jnp.int32, (chunk, chunk), 1)
    lead = jnp.where(d == 0, t_idx - s_idx, s_idx - t_idx)
    incl = lead >= 0
    strict = lead > 0

    lw = lw_ref[...]
    cum = _dot_f32(incl.astype(F32), lw)
    tot = jnp.sum(lw, axis=0, keepdims=True)
    e_in = jnp.exp(cum)
    e_neg = jnp.exp(-cum)
    e_last = jnp.exp(tot - cum)
    kkn = kk_ref[...]
    km = km_ref[...]
    bb = bb_ref[...]
    r_t = r_ref[...] * e_in
    a_t = -kkn * jnp.exp(cum - lw)
    b_t = bb * e_neg
    k_t = km * e_neg
    b_l = bb * e_last
    k_l = km * e_last
    w_l = jnp.exp(tot)
    v_all = v_ref[...]

    hs = range(heads)
    sl = [slice(h * n, (h + 1) * n) for h in hs]
    eye = (t_idx == s_idx).astype(F32)
    s0 = [s_ref[h] for h in hs]
    m = [jnp.where(strict, _dot_nt(a_t[:, sl[h]], b_t[:, sl[h]]), 0.0) for h in hs]
    a_ak = [jnp.where(strict, _dot_nt(a_t[:, sl[h]], k_t[:, sl[h]]), 0.0) for h in hs]
    a_rb = [jnp.where(incl, _dot_nt(r_t[:, sl[h]], b_t[:, sl[h]]), 0.0) for h in hs]
    a_rk = [jnp.where(incl, _dot_nt(r_t[:, sl[h]], k_t[:, sl[h]]), 0.0) for h in hs]
    x = [_dot_nt(a_t[:, sl[h]], s0[h]) + _dot(a_ak[h], v_all[:, sl[h]]) for h in hs]
    y0 = [_dot_nt(r_t[:, sl[h]], s0[h]) + _dot(a_rk[h], v_all[:, sl[h]]) for h in hs]
    p = [eye + m[h] for h in hs]
    span = 2
    while span < chunk:
        m = [_dot(m[h], m[h]) for h in hs]
        p = [p[h] + _dot(m[h], p[h]) for h in hs]
        span *= 2
    u = [_dot(p[h], x[h]) for h in hs]
    y_ref[...] = jnp.concatenate([y0[h] + _dot(a_rb[h], u[h]) for h in hs], axis=1)
    for h in hs:
        s_ref[h] = (s0[h] * w_l[:, sl[h]] + _dot_tn(u[h], b_l[:, sl[h]])
                    + _dot_tn(v_all[:, sl[h]], k_l[:, sl[h]]))


def _chunk_index(d, b, j, dims):
    nc_ctx, nc_lat, nb = dims["ctx"] // CHUNK, dims["seq"] // CHUNK, dims["batch"]
    jc = jnp.where(d == 0, j, nc_ctx - 1 - j)
    jl = jnp.where(d == 0, j - nc_ctx, nc_lat - 1 - (j - nc_ctx))
    return jnp.where(j < nc_ctx, nb * nc_lat + b * nc_ctx + jc, b * nc_lat + jl)


def _rwkv_scan(r, v, kkn, lw, km, bb, dims, heads_per_step):
    n_rows, rd = r.shape
    gw = heads_per_step * RWKV_HEAD
    n_chunks = (dims["ctx"] + dims["seq"]) // CHUNK
    cidx = lambda d, b, g, j: _chunk_index(d, b, j, dims)
    row_spec = pl.BlockSpec((CHUNK, gw), lambda d, b, g, j: (cidx(d, b, g, j), g))
    dir_spec = pl.BlockSpec((None, CHUNK, gw), lambda d, b, g, j: (d, cidx(d, b, g, j), g))
    return pl.pallas_call(
        functools.partial(_rwkv_scan_kernel, heads=heads_per_step),
        grid=(2, dims["batch"], rd // gw, n_chunks),
        in_specs=[row_spec, row_spec, row_spec, dir_spec, dir_spec, dir_spec],
        out_specs=dir_spec,
        out_shape=jax.ShapeDtypeStruct((2, n_rows, rd), F32),
        scratch_shapes=[pltpu.VMEM((heads_per_step, RWKV_HEAD, RWKV_HEAD), F32)],
        compiler_params=_params("parallel", "parallel", "parallel", "arbitrary"), name="rwkv_scan",
    )(r, v, kkn, lw, km, bb)


def _mlstm_kernel(q_ref, k_ref, v_ref, gc_ref, gr_ref, bc_ref, br_ref, h_ref, c_ref, n_ref, m_ref,
                  *, heads, dqk, dv):
    d = pl.program_id(0)
    j = pl.program_id(2)

    @pl.when(j == 0)
    def _():
        c_ref[...] = jnp.zeros_like(c_ref)
        n_ref[...] = jnp.zeros_like(n_ref)
        m_ref[...] = jnp.zeros_like(m_ref)

    chunk = q_ref.shape[0]
    t_idx = lax.broadcasted_iota(jnp.int32, (chunk, chunk), 0)
    s_idx = lax.broadcasted_iota(jnp.int32, (chunk, chunk), 1)
    lead = jnp.where(d == 0, t_idx - s_idx, s_idx - t_idx)
    incl = lead >= 0
    tri = incl.astype(F32)
    tri_t = (lead <= 0).astype(F32)

    gcol = gc_ref[...] + bc_ref[...]
    grow = gr_ref[...] + br_ref[...]
    i_col = gcol[:, :heads]
    f_col = jax.nn.log_sigmoid(gcol[:, heads:])
    i_row = grow[:heads, :]
    f_row = jax.nn.log_sigmoid(grow[heads:, :])
    b_col = _dot_f32(tri, f_col)
    b_row = _dot_f32(f_row, tri_t)
    b_last = jnp.sum(f_col, axis=0, keepdims=True)
    scale = np.float32(dqk ** -0.5)

    hs = range(heads)
    q = [q_ref[:, h * dqk:(h + 1) * dqk] * scale for h in hs]
    k = [k_ref[:, h * dqk:(h + 1) * dqk] for h in hs]
    v = [v_ref[:, h * dv:(h + 1) * dv] for h in hs]
    c_st = [c_ref[h] for h in hs]
    n_st = [n_ref[h] for h in hs]
    m_st = [m_ref[h][:, 0:1] for h in hs]
    bc = [b_col[:, h:h + 1] for h in hs]
    qk = [_dot_nt(q[h], k[h]) for h in hs]
    qc = [_dot(q[h], c_st[h]) for h in hs]
    dmat = [jnp.where(incl, bc[h] + (i_row[h:h + 1, :] - b_row[h:h + 1, :]), -jnp.inf) for h in hs]
    inter = [bc[h] + m_st[h] for h in hs]
    m_t = [jnp.maximum(inter[h], jnp.max(dmat[h], axis=-1, keepdims=True)) for h in hs]
    w_inter = [jnp.exp(inter[h] - m_t[h]) for h in hs]
    s = [qk[h] * jnp.exp(dmat[h] - m_t[h]) for h in hs]
    num = [w_inter[h] * qc[h] + _dot(s[h], v[h]) for h in hs]
    den = [w_inter[h] * jnp.sum(q[h] * n_st[h], axis=-1, keepdims=True) + jnp.sum(s[h], axis=-1, keepdims=True)
           for h in hs]
    h_ref[...] = jnp.concatenate([num[h] / jnp.maximum(jnp.abs(den[h]), jnp.exp(-m_t[h])) for h in hs], axis=1)
    bl = [b_last[:, h:h + 1] for h in hs]
    g = [bl[h] - bc[h] + i_col[:, h:h + 1] for h in hs]
    m_new = [jnp.maximum(bl[h] + m_st[h], jnp.max(g[h], axis=0, keepdims=True)) for h in hs]
    decay = [jnp.exp(bl[h] + m_st[h] - m_new[h]) for h in hs]
    kw = [k[h] * jnp.exp(g[h] - m_new[h]) for h in hs]
    for h in hs:
        c_ref[h] = decay[h] * c_st[h] + _dot_tn(kw[h], v[h])
        n_ref[h] = decay[h] * n_st[h] + jnp.sum(kw[h], axis=0, keepdims=True)
        m_ref[h] = jnp.broadcast_to(m_new[h], m_ref.shape[1:])


def _mlstm_scan(z, lay, dims, gates_col, gates_row, bias_col, bias_row, heads, dqk, dv):
    n_rows = z.shape[0]
    n_chunks = (dims["ctx"] + dims["seq"]) // CHUNK
    md = heads * dv
    qw = heads * dqk
    cidx = lambda d, b, j: _chunk_index(d, b, j, dims)
    q_blk, k_blk, v_blk = lay["mq"] // qw, lay["mk"] // qw, lay["mv"] // md
    return pl.pallas_call(
        functools.partial(_mlstm_kernel, heads=heads, dqk=dqk, dv=dv),
        grid=(2, dims["batch"], n_chunks),
        in_specs=[pl.BlockSpec((CHUNK, qw), lambda d, b, j: (cidx(d, b, j), q_blk)),
                  pl.BlockSpec((CHUNK, qw), lambda d, b, j: (cidx(d, b, j), k_blk)),
                  pl.BlockSpec((CHUNK, md), lambda d, b, j: (cidx(d, b, j), v_blk)),
                  pl.BlockSpec((None, None, CHUNK, 2 * heads), lambda d, b, j: (d, cidx(d, b, j), 0, 0)),
                  pl.BlockSpec((None, None, 2 * heads, CHUNK), lambda d, b, j: (d, cidx(d, b, j), 0, 0)),
                  pl.BlockSpec((None, 1, 2 * heads), lambda d, b, j: (d, 0, 0)),
                  pl.BlockSpec((None, 2 * heads, 1), lambda d, b, j: (d, 0, 0))],
        out_specs=pl.BlockSpec((None, CHUNK, md), lambda d, b, j: (d, cidx(d, b, j), 0)),
        out_shape=jax.ShapeDtypeStruct((2, n_rows, md), F32),
        scratch_shapes=[pltpu.VMEM((heads, dqk, dv), F32), pltpu.VMEM((heads, 1, dqk), F32),
                        pltpu.VMEM((heads, 1, LANES), F32)],
        compiler_params=_params("parallel", "parallel", "arbitrary"), name="mlstm_scan",
    )(z, z, z, gates_col, gates_row, bias_col, bias_row)


def _even_finish_kernel(y_ref, bon_ref, gm_ref, gnw_ref, gnb_ref, h_ref, o_ref, nw_ref, out_ref,
                        *, rd, heads, dv):
    y = y_ref[0] + y_ref[1]
    inv_n = np.float32(1.0 / RWKV_HEAD)
    mean = _seg_sum(y, RWKV_HEAD) * inv_n
    yc = y - mean
    var = _seg_sum(yc * yc, RWKV_HEAD) * inv_n
    yn = yc * lax.rsqrt(var + GN_EPS) * gnw_ref[...] + gnb_ref[...] + bon_ref[...]
    out_ref[:, 0:rd] = (yn * gm_ref[...]).astype(out_ref.dtype)
    hm = h_ref[0] + h_ref[1]
    for h in range(heads):
        sl = slice(h * dv, (h + 1) * dv)
        x = hm[:, sl]
        xn = x * lax.rsqrt(jnp.mean(x * x, axis=-1, keepdims=True) + RMS_EPS) * nw_ref[:, sl]
        out_ref[:, rd + h * dv:rd + (h + 1) * dv] = (jax.nn.sigmoid(o_ref[:, sl]) * xn).astype(out_ref.dtype)


def _even_finish(y, bonus, gmul, gn_w, gn_b, hm, z, lay, ml_norm, n_rows, dims, heads, dv):
    rd = y.shape[2]
    md = hm.shape[2]
    tm = _tile(dims["ctx_rows"], 256, 8)
    o_blk = lay["mo"] // md
    return pl.pallas_call(
        functools.partial(_even_finish_kernel, rd=rd, heads=heads, dv=dv),
        grid=(n_rows // tm,),
        in_specs=[pl.BlockSpec((2, tm, rd), lambda i: (0, i, 0)),
                  pl.BlockSpec((tm, rd), lambda i: (i, 0)),
                  pl.BlockSpec((tm, rd), lambda i: (i, 0)),
                  pl.BlockSpec((1, rd), lambda i: (0, 0)),
                  pl.BlockSpec((1, rd), lambda i: (0, 0)),
                  pl.BlockSpec((2, tm, md), lambda i: (0, i, 0)),
                  pl.BlockSpec((tm, md), lambda i: (i, o_blk)),
                  pl.BlockSpec((1, md), lambda i: (0, 0))],
        out_specs=pl.BlockSpec((tm, rd + md), lambda i: (i, 0)),
        out_shape=jax.ShapeDtypeStruct((n_rows, rd + md), BF16),
        compiler_params=_params("parallel"), name="even_finish",
    )(y, bonus, gmul, gn_w.reshape(1, rd), gn_b.reshape(1, rd), hm, z, ml_norm.reshape(1, md))


def _attn_kernel(q_ref, kl_ref, kc_ref, vl_ref, vc_ref, o_ref, *, group, head):
    kl = kl_ref[...]
    kc = kc_ref[...]
    vl = vl_ref[...]
    vc = vc_ref[...]
    for g in range(group):
        sl = slice(g * head, (g + 1) * head)
        q = q_ref[:, sl]
        s_l = _dot_nt(q, kl)
        s_c = _dot_nt(q, kc)
        m = jnp.maximum(jnp.max(s_l, axis=-1, keepdims=True), jnp.max(s_c, axis=-1, keepdims=True))
        p_l = jnp.exp(s_l - m)
        p_c = jnp.exp(s_c - m)
        den = jnp.sum(p_l, axis=-1, keepdims=True) + jnp.sum(p_c, axis=-1, keepdims=True)
        o = _dot(p_l, vl) + _dot(p_c, vc)
        o_ref[:, sl] = (o / den).astype(o_ref.dtype)


def _attention(q, k, v, dims, head, group):
    seq, ctx, nb = dims["seq"], dims["ctx"], dims["batch"]
    kvh = k.shape[1] // head
    tq = _tile(seq, 256, 8)
    nq = seq // tq
    gw = group * head
    return pl.pallas_call(
        functools.partial(_attn_kernel, group=group, head=head),
        grid=(nb, kvh, nq),
        in_specs=[pl.BlockSpec((tq, gw), lambda b, h, i: (b * nq + i, h)),
                  pl.BlockSpec((seq, head), lambda b, h, i: (b, h)),
                  pl.BlockSpec((ctx, head), lambda b, h, i: (nb * (seq // ctx) + b, h)),
                  pl.BlockSpec((seq, head), lambda b, h, i: (b, h)),
                  pl.BlockSpec((ctx, head), lambda b, h, i: (nb * (seq // ctx) + b, h))],
        out_specs=pl.BlockSpec((tq, gw), lambda b, h, i: (b * nq + i, h)),
        out_shape=jax.ShapeDtypeStruct(q.shape, BF16),
        compiler_params=_params("parallel", "parallel", "parallel"), name="attention",
    )(q, k, k, v, v)


def _new_expert(i, be_ref):
    return jnp.logical_or(i == 0, be_ref[i] != be_ref[jnp.maximum(i - 1, 0)])


def _moe_up_kernel(be_ref, nu_ref, x_ref, w1_ref, w3_ref, mid_ref, w1_bf, w3_bf):
    i = pl.program_id(1)

    @pl.when(_new_expert(i, be_ref))
    def _():
        w1_bf[...] = w1_ref[...].astype(BF16)
        w3_bf[...] = w3_ref[...].astype(BF16)

    @pl.when(i < nu_ref[0])
    def _():
        x = x_ref[...]
        a = jnp.dot(x, w1_bf[...], preferred_element_type=F32)
        b = jnp.dot(x, w3_bf[...], preferred_element_type=F32)
        mid_ref[...] = (a * jax.nn.sigmoid(a) * b).astype(mid_ref.dtype)

    @pl.when(i >= nu_ref[0])
    def _():
        mid_ref[...] = jnp.zeros_like(mid_ref)


def _moe_down_kernel(be_ref, nu_ref, mid_ref, w2_ref, o_ref, w2_bf):
    i = pl.program_id(1)

    @pl.when(_new_expert(i, be_ref))
    def _():
        w2_bf[...] = w2_ref[...].astype(BF16)

    @pl.when(i < nu_ref[0])
    def _():
        o_ref[...] = jnp.dot(mid_ref[...], w2_bf[...], preferred_element_type=F32)

    @pl.when(i >= nu_ref[0])
    def _():
        o_ref[...] = jnp.zeros_like(o_ref)


def _moe_ffn(xs, block_expert, n_used, w1, w3, w2, layer, bm):
    n_rows, d = xs.shape
    de = w1.shape[3]
    tde = _tile(de, 512, LANES)
    tn = _tile(d, 2048, LANES)
    nblk = n_rows // bm
    mid = pl.pallas_call(
        _moe_up_kernel,
        grid_spec=pltpu.PrefetchScalarGridSpec(
            num_scalar_prefetch=2, grid=(de // tde, nblk),
            in_specs=[pl.BlockSpec((bm, d), lambda k, i, be, nu: (i, 0)),
                      pl.BlockSpec((None, None, d, tde), lambda k, i, be, nu: (layer, be[i], 0, k)),
                      pl.BlockSpec((None, None, d, tde), lambda k, i, be, nu: (layer, be[i], 0, k))],
            out_specs=pl.BlockSpec((bm, tde), lambda k, i, be, nu: (i, k)),
            scratch_shapes=[pltpu.VMEM((d, tde), BF16), pltpu.VMEM((d, tde), BF16)]),
        out_shape=jax.ShapeDtypeStruct((n_rows, de), BF16),
        compiler_params=_params("arbitrary", "arbitrary"), name="moe_up",
    )(block_expert, n_used, xs, w1, w3)
    return pl.pallas_call(
        _moe_down_kernel,
        grid_spec=pltpu.PrefetchScalarGridSpec(
            num_scalar_prefetch=2, grid=(d // tn, nblk),
            in_specs=[pl.BlockSpec((bm, de), lambda n, i, be, nu: (i, 0)),
                      pl.BlockSpec((None, None, de, tn), lambda n, i, be, nu: (layer, be[i], 0, n))],
            out_specs=pl.BlockSpec((bm, tn), lambda n, i, be, nu: (i, n)),
            scratch_shapes=[pltpu.VMEM((de, tn), BF16)]),
        out_shape=jax.ShapeDtypeStruct((n_rows, d), F32),
        compiler_params=_params("arbitrary", "arbitrary"), name="moe_down",
    )(block_expert, n_used, mid, w2)


def _dispatch_plan(idx, n_experts, bm):
    n_tok = idx.shape[1]
    n_assign = n_tok * TOP_K
    expert = idx.T.reshape(-1)
    onehot = (expert[:, None] == jnp.arange(n_experts)[None, :]).astype(jnp.int32)
    rank = jnp.take_along_axis(jnp.cumsum(onehot, axis=0) - onehot, expert[:, None], 1)[:, 0]
    counts = jnp.sum(onehot, axis=0)
    padded = (counts + bm - 1) // bm * bm
    pad_end = jnp.cumsum(padded)
    pos = (pad_end - padded)[expert] + rank
    n_blocks = -(-n_assign // bm) + n_experts
    token = jnp.repeat(jnp.arange(n_tok, dtype=jnp.int32), TOP_K)
    row_token = jnp.zeros((n_blocks * bm,), jnp.int32).at[pos].set(token)
    block_expert = jnp.minimum(jnp.searchsorted(pad_end, jnp.arange(n_blocks) * bm, side='right'),
                               n_experts - 1).astype(jnp.int32)
    n_used = (pad_end[-1] // bm).astype(jnp.int32).reshape(1)
    block_expert = jnp.where(jnp.arange(n_blocks) < n_used[0], block_expert,
                             block_expert[jnp.maximum(n_used[0] - 1, 0)])
    return pos.reshape(n_tok, TOP_K), row_token, block_expert, n_used


def _ffn_res_kernel(x_ref, f0_ref, f1_ref, g_ref, gt_ref, o_ref):
    g = g_ref[...]
    f = g[:, 0:1] * f0_ref[...] + g[:, 1:2] * f1_ref[...]
    o_ref[...] = x_ref[...] + gt_ref[...] * f


def _ffn_res_norm_kernel(x_ref, f0_ref, f1_ref, g_ref, gt_ref, nw_ref, o_ref):
    g = g_ref[...]
    f = g[:, 0:1] * f0_ref[...] + g[:, 1:2] * f1_ref[...]
    x = x_ref[...] + gt_ref[...] * f
    o_ref[...] = x * lax.rsqrt(jnp.mean(x * x, axis=-1, keepdims=True) + RMS_EPS) * nw_ref[...]


def _ffn_residual(x, f0, f1, gate, mod3, gate_chunk, n_rows, dims, final_norm=None):
    d = x.shape[1]
    tm = _tile(dims["ctx_rows"], 256, 8)
    grp = dims["group_of"](tm)
    row = pl.BlockSpec((tm, d), lambda i: (i, 0))
    in_specs = [row, row, row, pl.BlockSpec((tm, TOP_K), lambda i: (i, 0)),
                pl.BlockSpec((None, 1, d), lambda i: (grp(i), 0, gate_chunk))]
    args = [x, f0, f1, gate, mod3]
    kern = _ffn_res_kernel
    if final_norm is not None:
        in_specs.append(pl.BlockSpec((1, d), lambda i: (0, 0)))
        args.append(final_norm.reshape(1, d))
        kern = _ffn_res_norm_kernel
    return pl.pallas_call(
        kern, grid=(n_rows // tm,), in_specs=in_specs, out_specs=row,
        out_shape=jax.ShapeDtypeStruct((n_rows, d), F32),
        compiler_params=_params("parallel"), name="ffn_residual",
    )(*args)


def _even_layout(rd, lw, la, lg, mh, dqk, md):
    lwp = _rup(lw + 4 * mh, LANES)
    assert _rup(la, LANES) == lwp
    lay = {"rd": rd, "lg": lg, "lwp": lwp}
    off = 3 * rd
    for name, width in (("mq", mh * dqk), ("mk", mh * dqk), ("mv", md), ("mo", md), ("tail", lg + 2 * lwp)):
        assert off % width == 0
        lay[name] = off
        off += width
    lay["width"] = off
    return lay


def _pad_cols(seg, width):
    return jnp.pad(seg, ((0, 0), (0, width - seg.shape[1])))


def _pack_even_weights(w_in, mu, lay, lw, la, mh, dqk, md):
    rd, lg, lwp = lay["rd"], lay["lg"], lay["lwp"]
    rww = 3 * rd + lw + la + lg
    o = 3 * rd
    w_down, a_down, g_down = w_in[:, o:o + lw], w_in[:, o + lw:o + lw + la], w_in[:, o + lw + la:rww]
    m_end = rww + 2 * mh * dqk + 2 * md
    w_p = jnp.concatenate([w_in[:, :3 * rd], w_in[:, rww:m_end], g_down,
                           _pad_cols(jnp.concatenate([w_down, w_in[:, m_end:]], 1), lwp),
                           _pad_cols(a_down, lwp)], axis=1).astype(BF16)
    mu2 = mu.reshape(1, -1)
    mu_tail = jnp.concatenate([mu2[:, o + lw + la:rww], _pad_cols(mu2[:, o:o + lw], lwp),
                               _pad_cols(mu2[:, o + lw:o + lw + la], lwp)], axis=1)
    return w_p, mu2[:, :3 * rd], mu_tail


def _pad_rows(w, rows):
    return jnp.pad(w, ((0, 0), (0, rows - w.shape[1]), (0, 0)))


def _rope_tables(dims, head):
    seq, nb = dims["seq"], dims["batch"]
    pairs = head // 4
    rows = seq // GRID_W
    row = jnp.repeat(jnp.arange(rows), GRID_W).astype(F32)
    col = jnp.tile(jnp.arange(GRID_W), rows).astype(F32)
    inv = ROPE_THETA ** (-jnp.arange(pairs, dtype=F32) / pairs)
    ang = jnp.concatenate([row[:, None] * inv, col[:, None] * inv], -1)
    cos, sin = jnp.cos(ang), jnp.sin(ang)
    cos_t = jnp.tile(jnp.concatenate([cos, cos], -1), (nb, 1))
    sin_t = jnp.tile(jnp.concatenate([-sin, sin], -1), (nb, 1))
    n_ctx = dims["ctx_rows"]
    return (jnp.concatenate([cos_t, jnp.ones((n_ctx, head), F32)], 0),
            jnp.concatenate([sin_t, jnp.zeros((n_ctx, head), F32)], 0))


def kernel(x, c, ctx, c_ctx, mod_w, mod_b, norm_mix, norm_ffn, norm_final, router_w, router_bias,
           exp_w1, exp_w3, exp_w2, ev_w_in, ev_w_out, rw_mu, rw_w0, rw_w_up, rw_a0, rw_a_up, rw_g_up,
           rw_k_k, rw_k_a, rw_r_k, rw_gn_w, rw_gn_b, ml_gate_b, ml_norm, at_w_qkv, at_q_norm, at_k_norm,
           at_w_o):
    nb, seq, d = x.shape
    n_ctx = ctx.shape[1]
    depth = mod_w.shape[0]
    lat_rows, ctx_rows = nb * seq, nb * n_ctx
    all_rows = lat_rows + ctx_rows
    assert nb + 1 <= MOD_ROWS and seq % n_ctx == 0 and n_ctx % CHUNK == 0
    dims = {"batch": nb, "seq": seq, "ctx": n_ctx, "lat_rows": lat_rows, "ctx_rows": ctx_rows,
            "group_of": lambda tm: (lambda i: jnp.minimum(i * tm // seq, nb))}

    n_experts = router_w.shape[1]
    rd = rw_w0.shape[-1]
    lw, la, lg = rw_w_up.shape[2], rw_a_up.shape[2], rw_g_up.shape[1]
    mh = ml_gate_b.shape[-1]
    md = ml_norm.shape[-1]
    dv = md // mh
    dqk = dv // 2
    head = at_q_norm.shape[-1]
    kv_dim = (at_w_qkv.shape[-1] - d) // 2
    group = (d // head) // (kv_dim // head)
    moe_bm = 256

    xa = jnp.concatenate([x.reshape(lat_rows, d), ctx.reshape(ctx_rows, d)], axis=0)
    cond = jnp.zeros((MOD_ROWS, d), F32).at[:nb].set(c).at[nb].set(c_ctx)
    lay = _even_layout(rd, lw, la, lg, mh, dqk, md)
    out = None

    for layer in range(depth):
        ctx_out = layer < depth - 1
        j = layer // 2
        rows_out = all_rows if ctx_out else lat_rows
        mod3 = _mod_table(cond, mod_w, mod_b, layer).reshape(MOD_ROWS, 1, 6 * d)

        h = _norm_mod(xa, norm_mix[layer], mod3, 0, all_rows, dims)[0]
        if layer % 2 == 0:
            w_p, mu_main, mu_tail = _pack_even_weights(ev_w_in[j], rw_mu[j], lay, lw, la, mh, dqk, md)
            z = _proj(h, w_p, all_rows, F32)
            r, v, kkn, lwd, km, bb, bonus, gmul = _rwkv_prep(
                z, all_rows, lay, dims, mu_main, mu_tail, rw_w0[j], _pad_rows(rw_w_up[j], lay["lwp"]).astype(BF16),
                rw_a0[j], _pad_rows(rw_a_up[j], lay["lwp"]).astype(BF16), rw_g_up[j].astype(BF16),
                rw_k_k[j], rw_k_a[j], rw_r_k[j])
            y = _rwkv_scan(r, v, kkn, lwd, km, bb, dims, heads_per_step=min(16, rd // RWKV_HEAD))
            g_off = lay["tail"] + lay["lg"] + lw
            gates = z[:, g_off:g_off + 4 * mh].reshape(all_rows, 2, 2 * mh)
            gates_col = jnp.moveaxis(gates, 1, 0).reshape(2, all_rows // CHUNK, CHUNK, 2 * mh)
            gates_row = jnp.swapaxes(gates_col, 2, 3)
            bias = ml_gate_b[j].reshape(2, 2 * mh)
            hm = _mlstm_scan(z, lay, dims, gates_col, gates_row, bias.reshape(2, 1, 2 * mh),
                             bias.reshape(2, 2 * mh, 1), mh, dqk, dv)
            mix = _even_finish(y, bonus, gmul, rw_gn_w[j], rw_gn_b[j], hm, z, lay, ml_norm[j],
                               rows_out, dims, mh, dv)
            xa_new = _proj_residual(mix, ev_w_out[j].astype(BF16), xa, mod3, 2, rows_out, dims)
        else:
            w_qkv = at_w_qkv[j].astype(BF16)
            cos_t, sin_t = _rope_tables(dims, head)
            q = _proj_qk(h, w_qkv, 0, d, at_q_norm[j], cos_t, sin_t, np.float32(head ** -0.5), lat_rows)
            k = _proj_qk(h, w_qkv, d, kv_dim, at_k_norm[j], cos_t, sin_t, np.float32(1.0), all_rows)
            v = _proj(h, w_qkv, all_rows, BF16, col_off=d + kv_dim, n_cols=kv_dim)
            att = _attention(q, k, v, dims, head, group)
            xa_new = _proj_residual(att, at_w_o[j].astype(BF16), xa, mod3, 2, lat_rows, dims)
            if ctx_out:
                raise NotImplementedError("context output of an attention layer")
        xa = xa_new

        h2, idx, gate_t = _norm_mod(xa, norm_ffn[layer], mod3, 3, rows_out, dims,
                                    router=(router_w, router_bias))
        gate = gate_t.T
        pos, row_token, block_expert, n_used = _dispatch_plan(idx, n_experts, moe_bm)
        xs = h2.at[row_token].get(mode="promise_in_bounds")
        ys = _moe_ffn(xs, block_expert, n_used, exp_w1, exp_w3, exp_w2, layer, moe_bm)
        f0 = ys.at[pos[:, 0]].get(mode="promise_in_bounds")
        f1 = ys.at[pos[:, 1]].get(mode="promise_in_bounds")
        last = layer == depth - 1
        xa = _ffn_residual(xa, f0, f1, gate, mod3, 5, rows_out, dims,
                           final_norm=norm_final if last else None)
        if last:
            out = xa[:lat_rows].reshape(nb, seq, d)
    return out
```

```python
import functools

import jax
import jax.numpy as jnp
import numpy as np
from jax import lax
from jax.experimental import pallas as pl
from jax.experimental.pallas import tpu as pltpu

F32 = jnp.float32
BF16 = jnp.bfloat16
HIGHEST = lax.Precision.HIGHEST

GRID_W = 64
RMS_EPS = 1e-6
GN_EPS = 64e-5
RWKV_HEAD = 64
CHUNK = 64
N_GROUPS = 4
TOP_K = 2
ROPE_THETA = 10000.0
LANES = 128
MOD_ROWS = 8
VMEM_LIMIT = 56 * 1024 * 1024


def _rup(n, m):
    return (n + m - 1) // m * m


def _tile(n, pref, quantum):
    t = min(pref, n) // quantum * quantum
    while t >= quantum:
        if n % t == 0:
            return t
        t -= quantum
    return n


def _params(*sem):
    return pltpu.CompilerParams(dimension_semantics=sem, vmem_limit_bytes=VMEM_LIMIT)


def _dot(a, b):
    return jnp.dot(a.astype(BF16), b.astype(BF16), preferred_element_type=F32)


def _dot_nt(a, b):
    return lax.dot_general(a.astype(BF16), b.astype(BF16), (((1,), (1,)), ((), ())),
                           preferred_element_type=F32)


def _dot_tn(a, b):
    return lax.dot_general(a.astype(BF16), b.astype(BF16), (((0,), (0,)), ((), ())),
                           preferred_element_type=F32)


def _dot_f32(a, b):
    return jnp.dot(a, b, preferred_element_type=F32, precision=HIGHEST)


def _mod_kernel(c_ref, w_ref, b_ref, o_ref):
    c = c_ref[...]
    c = c * jax.nn.sigmoid(c)
    o_ref[...] = _dot(c, w_ref[...]) + b_ref[...]


def _mod_tables(cond, mod_w, mod_b):
    d = cond.shape[1]
    depth, _, n = mod_w.shape
    tn = _tile(n, 512, LANES)
    return pl.pallas_call(
        _mod_kernel,
        grid=(depth, n // tn),
        in_specs=[pl.BlockSpec((MOD_ROWS, d), lambda l, j: (0, 0)),
                  pl.BlockSpec((None, d, tn), lambda l, j: (l, 0, j)),
                  pl.BlockSpec((None, 1, tn), lambda l, j: (l, 0, j))],
        out_specs=pl.BlockSpec((None, MOD_ROWS, tn), lambda l, j: (l, 0, j)),
        out_shape=jax.ShapeDtypeStruct((depth, MOD_ROWS, n), F32),
        compiler_params=_params("parallel", "parallel"),
        name="mod_tables",
    )(cond, mod_w, mod_b.reshape(depth, 1, n))


def _norm_mod_kernel(x_ref, g_ref, sh_ref, sc_ref, h_ref):
    x = x_ref[...]
    y = x * lax.rsqrt(jnp.mean(x * x, axis=-1, keepdims=True) + RMS_EPS) * g_ref[...]
    h_ref[...] = (y * (1.0 + sc_ref[...]) + sh_ref[...]).astype(h_ref.dtype)


def _top2_sum(a, b, c, d):
    hi1, lo1 = jnp.maximum(a, b), jnp.minimum(a, b)
    hi2, lo2 = jnp.maximum(c, d), jnp.minimum(c, d)
    return jnp.maximum(hi1, hi2) + jnp.maximum(jnp.minimum(hi1, hi2), jnp.maximum(lo1, lo2))


def _first_argmax(vals):
    best_v = vals[0]
    best_i = jnp.zeros(vals[0].shape, jnp.int32)
    for i in range(1, len(vals)):
        better = vals[i] > best_v
        best_i = jnp.where(better, i, best_i)
        best_v = jnp.where(better, vals[i], best_v)
    return best_i, best_v


def _pick(rows, index):
    out = rows[0]
    for i in range(1, len(rows)):
        out = jnp.where(index == i, rows[i], out)
    return out


def _norm_mod_router_kernel(x_ref, g_ref, sh_ref, sc_ref, rwt_ref, rb_ref, h_ref, idx_ref, gate_ref,
                            *, n_experts):
    x = x_ref[...]
    y = x * lax.rsqrt(jnp.mean(x * x, axis=-1, keepdims=True) + RMS_EPS) * g_ref[...]
    h = y * (1.0 + sc_ref[...]) + sh_ref[...]
    h_ref[...] = h.astype(h_ref.dtype)
    logits = lax.dot_general(rwt_ref[...], h, (((1,), (1,)), ((), ())), preferred_element_type=F32,
                             precision=HIGHEST)
    aff_all = jax.nn.sigmoid(logits)
    sel_all = aff_all + rb_ref[...]
    per_group = n_experts // N_GROUPS
    aff = [aff_all[e:e + 1, :] for e in range(n_experts)]
    sel = [sel_all[e:e + 1, :] for e in range(n_experts)]
    assert per_group == 4 and TOP_K == 2
    best, _ = _first_argmax([_top2_sum(*sel[g * per_group:(g + 1) * per_group]) for g in range(N_GROUPS)])
    cand = [_pick([sel[g * per_group + i] for g in range(N_GROUPS)], best) for i in range(per_group)]
    cand_aff = [_pick([aff[g * per_group + i] for g in range(N_GROUPS)], best) for i in range(per_group)]
    i1, _ = _first_argmax(cand)
    i2, _ = _first_argmax([jnp.where(i1 == i, -jnp.inf, cand[i]) for i in range(per_group)])
    g1 = _pick(cand_aff, i1)
    g2 = _pick(cand_aff, i2)
    idx_ref[0:1, :] = best * per_group + i1
    idx_ref[1:2, :] = best * per_group + i2
    gate_ref[0:1, :] = g1 / (g1 + g2)
    gate_ref[1:2, :] = g2 / (g1 + g2)


def _norm_mod(x, g, mod3, shift_chunk, n_rows, dims, router=None):
    d = x.shape[1]
    tm = _tile(dims["ctx_rows"], 256, 8)
    grp = dims["group_of"](tm)
    in_specs = [pl.BlockSpec((tm, d), lambda i: (i, 0)),
                pl.BlockSpec((1, d), lambda i: (0, 0)),
                pl.BlockSpec((None, 1, d), lambda i: (grp(i), 0, shift_chunk)),
                pl.BlockSpec((None, 1, d), lambda i: (grp(i), 0, shift_chunk + 1))]
    args = [x, g.reshape(1, d), mod3, mod3]
    out_specs = [pl.BlockSpec((tm, d), lambda i: (i, 0))]
    out_shape = [jax.ShapeDtypeStruct((n_rows, d), BF16)]
    kern = _norm_mod_kernel
    if router is not None:
        router_w, router_bias = router
        n_experts = router_w.shape[1]
        in_specs += [pl.BlockSpec((n_experts, d), lambda i: (0, 0)),
                     pl.BlockSpec((n_experts, 1), lambda i: (0, 0))]
        args += [router_w.T, router_bias.astype(F32).reshape(n_experts, 1)]
        out_specs += [pl.BlockSpec((TOP_K, tm), lambda i: (0, i))] * 2
        out_shape += [jax.ShapeDtypeStruct((TOP_K, n_rows), jnp.int32),
                      jax.ShapeDtypeStruct((TOP_K, n_rows), F32)]
        kern = functools.partial(_norm_mod_router_kernel, n_experts=n_experts)
    return pl.pallas_call(
        kern, grid=(n_rows // tm,), in_specs=in_specs, out_specs=out_specs, out_shape=out_shape,
        compiler_params=_params("parallel"), name="norm_mod",
    )(*args)


def _proj_kernel(a_ref, w_ref, o_ref):
    o_ref[...] = jnp.dot(a_ref[...], w_ref[...], preferred_element_type=F32).astype(o_ref.dtype)


def _proj(a, w, n_rows, out_dtype, col_off=0, n_cols=None, tm_pref=512, tn_pref=512):
    k = a.shape[1]
    n_cols = w.shape[1] - col_off if n_cols is None else n_cols
    tm = _tile(n_rows, tm_pref, 8)
    tn = _tile(int(np.gcd(n_cols, col_off)) if col_off else n_cols, tn_pref, LANES)
    off = col_off // tn
    return pl.pallas_call(
        _proj_kernel,
        grid=(n_cols // tn, n_rows // tm),
        in_specs=[pl.BlockSpec((tm, k), lambda j, i: (i, 0)),
                  pl.BlockSpec((k, tn), lambda j, i: (0, j + off))],
        out_specs=pl.BlockSpec((tm, tn), lambda j, i: (i, j)),
        out_shape=jax.ShapeDtypeStruct((n_rows, n_cols), out_dtype),
        compiler_params=_params("parallel", "parallel"), name="proj",
    )(a, w)


def _proj_res_kernel(a_ref, w_ref, x_ref, gt_ref, o_ref):
    acc = jnp.dot(a_ref[...], w_ref[...], preferred_element_type=F32)
    o_ref[...] = x_ref[...] + gt_ref[...] * acc


def _proj_residual(a, w, x, mod3, gate_chunk, n_rows, dims, tn_pref=512):
    k = a.shape[1]
    d = w.shape[1]
    tm = _tile(dims["ctx_rows"], 512, 8)
    tn = _tile(d, tn_pref, LANES)
    grp = dims["group_of"](tm)
    gblk = gate_chunk * (d // tn)
    return pl.pallas_call(
        _proj_res_kernel,
        grid=(d // tn, n_rows // tm),
        in_specs=[pl.BlockSpec((tm, k), lambda j, i: (i, 0)),
                  pl.BlockSpec((k, tn), lambda j, i: (0, j)),
                  pl.BlockSpec((tm, tn), lambda j, i: (i, j)),
                  pl.BlockSpec((None, 1, tn), lambda j, i: (grp(i), 0, gblk + j))],
        out_specs=pl.BlockSpec((tm, tn), lambda j, i: (i, j)),
        out_shape=jax.ShapeDtypeStruct((n_rows, d), F32),
        compiler_params=_params("parallel", "parallel"), name="proj_residual",
    )(a, w, x, mod3)


def _proj_qk_kernel(a_ref, w_ref, nw_ref, cos_ref, sin_ref, o_ref, *, head, scale):
    acc = jnp.dot(a_ref[...], w_ref[...], preferred_element_type=F32)
    nw = nw_ref[...]
    cs = cos_ref[...]
    sn = sin_ref[...]
    for s in range(acc.shape[1] // head):
        x = acc[:, s * head:(s + 1) * head]
        xn = x * lax.rsqrt(jnp.mean(x * x, axis=-1, keepdims=True) + RMS_EPS) * nw
        xr = xn * cs + pltpu.roll(xn, head // 2, 1) * sn
        o_ref[:, s * head:(s + 1) * head] = (xr * scale).astype(o_ref.dtype)


def _proj_qk(a, w, col_off, n_cols, norm_w, cos_t, sin_t, scale, n_rows):
    k = a.shape[1]
    head = norm_w.shape[0]
    tm = _tile(n_rows, 512, 8)
    tn = _tile(int(np.gcd(n_cols, col_off)) if col_off else n_cols, 512, head)
    off = col_off // tn
    return pl.pallas_call(
        functools.partial(_proj_qk_kernel, head=head, scale=scale),
        grid=(n_cols // tn, n_rows // tm),
        in_specs=[pl.BlockSpec((tm, k), lambda j, i: (i, 0)),
                  pl.BlockSpec((k, tn), lambda j, i: (0, j + off)),
                  pl.BlockSpec((1, head), lambda j, i: (0, 0)),
                  pl.BlockSpec((tm, head), lambda j, i: (i, 0)),
                  pl.BlockSpec((tm, head), lambda j, i: (i, 0))],
        out_specs=pl.BlockSpec((tm, tn), lambda j, i: (i, j)),
        out_shape=jax.ShapeDtypeStruct((n_rows, n_cols), BF16),
        compiler_params=_params("parallel", "parallel"), name="proj_qk",
    )(a, w, norm_w.reshape(1, head), cos_t, sin_t)


def _seg_sum(x, seg):
    n = x.shape[1]
    lane_blk = LANES if n % LANES == 0 else n
    r = lax.broadcasted_iota(jnp.int32, (lane_blk, lane_blk), 0) // seg
    c = lax.broadcasted_iota(jnp.int32, (lane_blk, lane_blk), 1) // seg
    ones_bd = (r == c).astype(F32)
    parts = [_dot_f32(x[:, s:s + lane_blk], ones_bd) for s in range(0, n, lane_blk)]
    return parts[0] if len(parts) == 1 else jnp.concatenate(parts, axis=1)


def _rwkv_prep_kernel(z_ref, zp_ref, zn_ref, t_ref, tp_ref, tn_ref, mu_ref, mut_ref, w0_ref, wup_ref,
                      a0_ref, aup_ref, gup_ref, kk_ref, ka_ref, rk_ref,
                      r_out, v_out, kkn_out, lw_out, km_out, bb_out, bon_out, gm_out,
                      *, rd, lg, lwp, seq, ctx, n_lat_rows):
    tm = z_ref.shape[0]
    row0 = pl.program_id(0) * tm
    in_lat = row0 < n_lat_rows
    seq_len = jnp.where(in_lat, seq, ctx)
    pos0 = jnp.where(in_lat, row0 % seq, (row0 - n_lat_rows) % ctx)
    has_prev = (pos0 != 0).astype(F32)
    has_next = (pos0 + tm != seq_len).astype(F32)

    def token_shift(cur_ref, prev_ref, next_ref, mix_ref):
        cur = cur_ref[...]
        rows = lax.broadcasted_iota(jnp.int32, cur.shape, 0)
        prev_row = prev_ref[7:8, :] * has_prev
        next_row = next_ref[0:1, :] * has_next
        before = jnp.where(rows == 0, prev_row, pltpu.roll(cur, 1, 0))
        after = jnp.where(rows == tm - 1, next_row, pltpu.roll(cur, tm - 1, 0))
        return cur + mix_ref[...] * (0.5 * (before + after) - cur)

    zs = token_shift(z_ref, zp_ref, zn_ref, mu_ref)
    ts = token_shift(t_ref, tp_ref, tn_ref, mut_ref)
    r = zs[:, 0:rd]
    k = zs[:, rd:2 * rd]
    v = zs[:, 2 * rd:3 * rd]
    g_down = ts[:, 0:lg]
    w_down = ts[:, lg:lg + lwp]
    a_down = ts[:, lg + lwp:lg + 2 * lwp]

    kk = k * kk_ref[...]
    kk = kk * lax.rsqrt(jnp.maximum(_seg_sum(kk * kk, RWKV_HEAD), 1e-24))
    r_out[...] = r
    v_out[...] = v
    kkn_out[...] = kk
    tw = jnp.tanh(w_down)
    bonus = jnp.zeros_like(r)
    for d in range(2):
        lw_out[d] = -np.float32(np.exp(-0.5)) * jax.nn.sigmoid(w0_ref[d] + _dot(tw, wup_ref[d]))
        a = jax.nn.sigmoid(a0_ref[d] + _dot(a_down, aup_ref[d]))
        k_mod = k * (1.0 + (a - 1.0) * ka_ref[...])
        km_out[d] = k_mod
        bb_out[d] = kk * a
        bonus = bonus + _seg_sum(r * k_mod * rk_ref[...], RWKV_HEAD) * v
    bon_out[...] = bonus
    gm_out[...] = _dot(jax.nn.sigmoid(g_down), gup_ref[...])


def _rwkv_prep(z, n_rows, lay, dims, mu_main, mu_tail, w0, w_up_p, a0, a_up_p, g_up, k_k, k_a, r_k):
    rd, lg, lwp = lay["rd"], lay["lg"], lay["lwp"]
    mw = 3 * rd
    tw = lg + 2 * lwp
    tblk = lay["tail"] // tw
    tm = _tile(dims["ctx"], 64, 8)
    nb8 = n_rows // 8
    full = lambda shape: pl.BlockSpec(shape, lambda i: (0,) * len(shape))
    row_spec = pl.BlockSpec((tm, rd), lambda i: (i, 0))
    dir_spec = pl.BlockSpec((2, tm, rd), lambda i: (0, i, 0))
    sds = jax.ShapeDtypeStruct
    kern = functools.partial(_rwkv_prep_kernel, rd=rd, lg=lg, lwp=lwp, seq=dims["seq"], ctx=dims["ctx"],
                             n_lat_rows=dims["lat_rows"])
    return pl.pallas_call(
        kern,
        grid=(n_rows // tm,),
        in_specs=[pl.BlockSpec((tm, mw), lambda i: (i, 0)),
                  pl.BlockSpec((8, mw), lambda i: (jnp.maximum(i * (tm // 8) - 1, 0), 0)),
                  pl.BlockSpec((8, mw), lambda i: (jnp.minimum((i + 1) * (tm // 8), nb8 - 1), 0)),
                  pl.BlockSpec((tm, tw), lambda i: (i, tblk)),
                  pl.BlockSpec((8, tw), lambda i: (jnp.maximum(i * (tm // 8) - 1, 0), tblk)),
                  pl.BlockSpec((8, tw), lambda i: (jnp.minimum((i + 1) * (tm // 8), nb8 - 1), tblk)),
                  full((1, mw)), full((1, tw)), full((2, 1, rd)), full((2, lwp, rd)), full((2, 1, rd)), full((2, lwp, rd)),
                  full((lg, rd)), full((1, rd)), full((1, rd)), full((1, rd))],
        out_specs=[row_spec, row_spec, row_spec, dir_spec, dir_spec, dir_spec, row_spec, row_spec],
        out_shape=[sds((n_rows, rd), F32)] * 3 + [sds((2, n_rows, rd), F32)] * 3 + [sds((n_rows, rd), F32)] * 2,
        compiler_params=_params("parallel"), name="rwkv_prep",
    )(z, z, z, z, z, z, mu_main, mu_tail, w0.reshape(2, 1, rd), w_up_p, a0.reshape(2, 1, rd), a_up_p, g_up,
      k_k.reshape(1, rd), k_a.reshape(1, rd), r_k.reshape(1, rd))


def _rwkv_scan_kernel(r_ref, v_ref, kk_ref, lw_ref, km_ref, bb_ref, y_ref, s_ref, *, heads):
    d = pl.program_id(0)
    j = pl.program_id(3)

    @pl.when(j == 0)
    def _():
        s_ref[...] = jnp.zeros_like(s_ref)

    n = RWKV_HEAD
    chunk = lw_ref.shape[0]
    t_idx = lax.broadcasted_iota(jnp.int32, (chunk, chunk), 0)
    s_idx = lax.broadcasted_iota(jnp.int32, (chunk, chunk), 1)
    lead = jnp.where(d == 0, t_idx - s_idx, s_idx - t_idx)
    incl = lead >= 0
    assert chunk & (chunk - 1) == 0

    lw = lw_ref[...]
    cum = _dot_f32(incl.astype(F32), lw)
    tot = jnp.sum(lw, axis=0, keepdims=True)
    e_in = jnp.exp(cum)
    e_neg = jnp.exp(-cum)
    e_last = jnp.exp(tot - cum)
    kkn = kk_ref[...]
    km = km_ref[...]
    bb = bb_ref[...]
    r_t = r_ref[...] * e_in
    a_t = -kkn * jnp.exp(cum - lw)
    b_t = bb * e_neg
    k_t = km * e_neg
    b_l = bb * e_last
    k_l = km * e_last
    w_l = jnp.exp(tot)
    v_all = v_ref[...]

    hs = range(heads)
    sl = [slice(h * n, (h + 1) * n) for h in hs]
    t2 = lax.broadcasted_iota(jnp.int32, (2 * chunk, 2 * chunk), 0)
    s2 = lax.broadcasted_iota(jnp.int32, (2 * chunk, 2 * chunk), 1)
    tt = jnp.where(t2 >= chunk, t2 - chunk, t2)
    ss = jnp.where(s2 >= chunk, s2 - chunk, s2)
    keep = jnp.where(d == 0, tt - ss, ss - tt) >= jnp.where(t2 < chunk, 1, 0)
    right = lax.broadcasted_iota(jnp.int32, (chunk, 2 * chunk), 1) >= chunk
    eye_right = (lax.broadcasted_iota(jnp.int32, (chunk, 2 * chunk), 1)
                 == lax.broadcasted_iota(jnp.int32, (chunk, 2 * chunk), 0) + chunk).astype(F32)
    zeros_v = jnp.zeros((chunk, n), F32)
    stack = lambda top, bottom: jnp.concatenate([top, bottom], axis=0)

    s0 = [s_ref[h] for h in hs]
    v = [v_all[:, sl[h]] for h in hs]
    pair = [jnp.where(keep, _dot_nt(stack(a_t[:, sl[h]], r_t[:, sl[h]]), stack(b_t[:, sl[h]], k_t[:, sl[h]])), 0.0)
            for h in hs]
    a_side = [pair[h][:chunk] for h in hs]
    r_side = [pair[h][chunk:] for h in hs]
    x = [_dot_nt(a_t[:, sl[h]], s0[h]) + _dot(a_side[h], stack(zeros_v, v[h])) for h in hs]
    y0 = [_dot_nt(r_t[:, sl[h]], s0[h]) for h in hs]
    q = [jnp.where(right, eye_right, a_side[h]) for h in hs]
    for _ in range(chunk.bit_length() - 1):
        q = [_dot(q[h][:, :chunk], q[h]) + jnp.where(right, q[h], 0.0) for h in hs]
    uv = [stack(_dot(q[h][:, chunk:], x[h]), v[h]) for h in hs]
    y_ref[...] = jnp.concatenate([y0[h] + _dot(r_side[h], uv[h]) for h in hs], axis=1)
    for h in hs:
        s_ref[h] = s0[h] * w_l[:, sl[h]] + _dot_tn(uv[h], stack(b_l[:, sl[h]], k_l[:, sl[h]]))


def _chunk_index(d, b, j, dims):
    nc_ctx, nc_lat, nb = dims["ctx"] // CHUNK, dims["seq"] // CHUNK, dims["batch"]
    jc = jnp.where(d == 0, j, nc_ctx - 1 - j)
    jl = jnp.where(d == 0, j - nc_ctx, nc_lat - 1 - (j - nc_ctx))
    return jnp.where(j < nc_ctx, nb * nc_lat + b * nc_ctx + jc, b * nc_lat + jl)


def _rwkv_scan(r, v, kkn, lw, km, bb, dims, heads_per_step):
    n_rows, rd = r.shape
    gw = heads_per_step * RWKV_HEAD
    n_chunks = (dims["ctx"] + dims["seq"]) // CHUNK
    cidx = lambda d, b, g, j: _chunk_index(d, b, j, dims)
    row_spec = pl.BlockSpec((CHUNK, gw), lambda d, b, g, j: (cidx(d, b, g, j), g))
    dir_spec = pl.BlockSpec((None, CHUNK, gw), lambda d, b, g, j: (d, cidx(d, b, g, j), g))
    return pl.pallas_call(
        functools.partial(_rwkv_scan_kernel, heads=heads_per_step),
        grid=(2, dims["batch"], rd // gw, n_chunks),
        in_specs=[row_spec, row_spec, row_spec, dir_spec, dir_spec, dir_spec],
        out_specs=dir_spec,
        out_shape=jax.ShapeDtypeStruct((2, n_rows, rd), F32),
        scratch_shapes=[pltpu.VMEM((heads_per_step, RWKV_HEAD, RWKV_HEAD), F32)],
        compiler_params=_params("parallel", "parallel", "parallel", "arbitrary"), name="rwkv_scan",
    )(r, v, kkn, lw, km, bb)


def _mlstm_kernel(q_ref, k_ref, v_ref, gc_ref, gr_ref, bc_ref, br_ref, h_ref, c_ref, n_ref, m_ref,
                  *, heads, dqk, dv):
    d = pl.program_id(0)
    j = pl.program_id(2)

    @pl.when(j == 0)
    def _():
        c_ref[...] = jnp.zeros_like(c_ref)
        n_ref[...] = jnp.zeros_like(n_ref)
        m_ref[...] = jnp.zeros_like(m_ref)

    chunk = q_ref.shape[0]
    t_idx = lax.broadcasted_iota(jnp.int32, (chunk, chunk), 0)
    s_idx = lax.broadcasted_iota(jnp.int32, (chunk, chunk), 1)
    lead = jnp.where(d == 0, t_idx - s_idx, s_idx - t_idx)
    incl = lead >= 0
    tri = incl.astype(F32)
    tri_t = (lead <= 0).astype(F32)

    gcol = gc_ref[...] + bc_ref[...]
    grow = gr_ref[...] + br_ref[...]
    i_col = gcol[:, :heads]
    f_col = jax.nn.log_sigmoid(gcol[:, heads:])
    i_row = grow[:heads, :]
    f_row = jax.nn.log_sigmoid(grow[heads:, :])
    b_col = _dot_f32(tri, f_col)
    b_row = _dot_f32(f_row, tri_t)
    b_last = jnp.sum(f_col, axis=0, keepdims=True)
    scale = np.float32(dqk ** -0.5)

    hs = range(heads)
    q = [q_ref[:, h * dqk:(h + 1) * dqk] * scale for h in hs]
    k = [k_ref[:, h * dqk:(h + 1) * dqk] for h in hs]
    v = [v_ref[:, h * dv:(h + 1) * dv] for h in hs]
    c_st = [c_ref[h] for h in hs]
    n_st = [n_ref[h] for h in hs]
    m_st = [m_ref[h][:, 0:1] for h in hs]
    bc = [b_col[:, h:h + 1] for h in hs]
    qk = [_dot_nt(q[h], k[h]) for h in hs]
    qc = [_dot(q[h], c_st[h]) for h in hs]
    dmat = [jnp.where(incl, bc[h] + (i_row[h:h + 1, :] - b_row[h:h + 1, :]), -jnp.inf) for h in hs]
    inter = [bc[h] + m_st[h] for h in hs]
    m_t = [jnp.maximum(inter[h], jnp.max(dmat[h], axis=-1, keepdims=True)) for h in hs]
    w_inter = [jnp.exp(inter[h] - m_t[h]) for h in hs]
    s = [qk[h] * jnp.exp(dmat[h] - m_t[h]) for h in hs]
    num = [w_inter[h] * qc[h] + _dot(s[h], v[h]) for h in hs]
    den = [w_inter[h] * jnp.sum(q[h] * n_st[h], axis=-1, keepdims=True) + jnp.sum(s[h], axis=-1, keepdims=True)
           for h in hs]
    h_ref[...] = jnp.concatenate([num[h] / jnp.maximum(jnp.abs(den[h]), jnp.exp(-m_t[h])) for h in hs], axis=1)
    bl = [b_last[:, h:h + 1] for h in hs]
    g = [bl[h] - bc[h] + i_col[:, h:h + 1] for h in hs]
    m_new = [jnp.maximum(bl[h] + m_st[h], jnp.max(g[h], axis=0, keepdims=True)) for h in hs]
    decay = [jnp.exp(bl[h] + m_st[h] - m_new[h]) for h in hs]
    kw = [k[h] * jnp.exp(g[h] - m_new[h]) for h in hs]
    for h in hs:
        c_ref[h] = decay[h] * c_st[h] + _dot_tn(kw[h], v[h])
        n_ref[h] = decay[h] * n_st[h] + jnp.sum(kw[h], axis=0, keepdims=True)
        m_ref[h] = jnp.broadcast_to(m_new[h], m_ref.shape[1:])


def _mlstm_scan(z, lay, dims, gates_col, gates_row, bias_col, bias_row, heads, dqk, dv):
    n_rows = z.shape[0]
    n_chunks = (dims["ctx"] + dims["seq"]) // CHUNK
    md = heads * dv
    qw = heads * dqk
    cidx = lambda d, b, j: _chunk_index(d, b, j, dims)
    q_blk, k_blk, v_blk = lay["mq"] // qw, lay["mk"] // qw, lay["mv"] // md
    return pl.pallas_call(
        functools.partial(_mlstm_kernel, heads=heads, dqk=dqk, dv=dv),
        grid=(2, dims["batch"], n_chunks),
        in_specs=[pl.BlockSpec((CHUNK, qw), lambda d, b, j: (cidx(d, b, j), q_blk)),
                  pl.BlockSpec((CHUNK, qw), lambda d, b, j: (cidx(d, b, j), k_blk)),
                  pl.BlockSpec((CHUNK, md), lambda d, b, j: (cidx(d, b, j), v_blk)),
                  pl.BlockSpec((None, None, CHUNK, 2 * heads), lambda d, b, j: (d, cidx(d, b, j), 0, 0)),
                  pl.BlockSpec((None, None, 2 * heads, CHUNK), lambda d, b, j: (d, cidx(d, b, j), 0, 0)),
                  pl.BlockSpec((None, 1, 2 * heads), lambda d, b, j: (d, 0, 0)),
                  pl.BlockSpec((None, 2 * heads, 1), lambda d, b, j: (d, 0, 0))],
        out_specs=pl.BlockSpec((None, CHUNK, md), lambda d, b, j: (d, cidx(d, b, j), 0)),
        out_shape=jax.ShapeDtypeStruct((2, n_rows, md), F32),
        scratch_shapes=[pltpu.VMEM((heads, dqk, dv), F32), pltpu.VMEM((heads, 1, dqk), F32),
                        pltpu.VMEM((heads, 1, LANES), F32)],
        compiler_params=_params("parallel", "parallel", "arbitrary"), name="mlstm_scan",
    )(z, z, z, gates_col, gates_row, bias_col, bias_row)


def _even_finish_kernel(y_ref, bon_ref, gm_ref, gnw_ref, gnb_ref, h_ref, o_ref, nw_ref, out_ref,
                        *, rd, heads, dv):
    y = y_ref[0] + y_ref[1]
    inv_n = np.float32(1.0 / RWKV_HEAD)
    mean = _seg_sum(y, RWKV_HEAD) * inv_n
    yc = y - mean
    var = _seg_sum(yc * yc, RWKV_HEAD) * inv_n
    yn = yc * lax.rsqrt(var + GN_EPS) * gnw_ref[...] + gnb_ref[...] + bon_ref[...]
    out_ref[:, 0:rd] = (yn * gm_ref[...]).astype(out_ref.dtype)
    hm = h_ref[0] + h_ref[1]
    for h in range(heads):
        sl = slice(h * dv, (h + 1) * dv)
        x = hm[:, sl]
        xn = x * lax.rsqrt(jnp.mean(x * x, axis=-1, keepdims=True) + RMS_EPS) * nw_ref[:, sl]
        out_ref[:, rd + h * dv:rd + (h + 1) * dv] = (jax.nn.sigmoid(o_ref[:, sl]) * xn).astype(out_ref.dtype)


def _even_finish(y, bonus, gmul, gn_w, gn_b, hm, z, lay, ml_norm, n_rows, dims, heads, dv):
    rd = y.shape[2]
    md = hm.shape[2]
    tm = _tile(dims["ctx_rows"], 256, 8)
    o_blk = lay["mo"] // md
    return pl.pallas_call(
        functools.partial(_even_finish_kernel, rd=rd, heads=heads, dv=dv),
        grid=(n_rows // tm,),
        in_specs=[pl.BlockSpec((2, tm, rd), lambda i: (0, i, 0)),
                  pl.BlockSpec((tm, rd), lambda i: (i, 0)),
                  pl.BlockSpec((tm, rd), lambda i: (i, 0)),
                  pl.BlockSpec((1, rd), lambda i: (0, 0)),
                  pl.BlockSpec((1, rd), lambda i: (0, 0)),
                  pl.BlockSpec((2, tm, md), lambda i: (0, i, 0)),
                  pl.BlockSpec((tm, md), lambda i: (i, o_blk)),
                  pl.BlockSpec((1, md), lambda i: (0, 0))],
        out_specs=pl.BlockSpec((tm, rd + md), lambda i: (i, 0)),
        out_shape=jax.ShapeDtypeStruct((n_rows, rd + md), BF16),
        compiler_params=_params("parallel"), name="even_finish",
    )(y, bonus, gmul, gn_w.reshape(1, rd), gn_b.reshape(1, rd), hm, z, ml_norm.reshape(1, md))


def _attn_kernel(q_ref, kl_ref, kc_ref, vl_ref, vc_ref, o_ref, *, group, head):
    tq = q_ref.shape[0]
    q = jnp.concatenate([q_ref[:, g * head:(g + 1) * head] for g in range(group)], axis=0)
    ck = _tile(kl_ref.shape[0], 256, 8)
    keys = [kc_ref[...]] + [kl_ref[c * ck:(c + 1) * ck, :] for c in range(kl_ref.shape[0] // ck)]
    vals = [vc_ref[...]] + [vl_ref[c * ck:(c + 1) * ck, :] for c in range(kl_ref.shape[0] // ck)]
    m = acc = None
    for kc, vc in zip(keys, vals):
        s = _dot_nt(q, kc)
        row_max = jnp.max(s, axis=-1, keepdims=True)
        v_aug = jnp.concatenate([vc, jnp.ones_like(vc)], axis=1)
        if m is None:
            m = row_max
            acc = _dot(jnp.exp2(s - m), v_aug)
        else:
            m_new = jnp.maximum(m, row_max)
            acc = acc * jnp.exp2(m - m_new) + _dot(jnp.exp2(s - m_new), v_aug)
            m = m_new
    o = acc[:, :head] / acc[:, head:head + 1]
    for g in range(group):
        o_ref[:, g * head:(g + 1) * head] = o[g * tq:(g + 1) * tq].astype(o_ref.dtype)


def _attention(q, k, v, dims, head, group):
    seq, ctx, nb = dims["seq"], dims["ctx"], dims["batch"]
    kvh = k.shape[1] // head
    tq = _tile(seq, 256, 8)
    nq = seq // tq
    gw = group * head
    return pl.pallas_call(
        functools.partial(_attn_kernel, group=group, head=head),
        grid=(nb, kvh, nq),
        in_specs=[pl.BlockSpec((tq, gw), lambda b, h, i: (b * nq + i, h)),
                  pl.BlockSpec((seq, head), lambda b, h, i: (b, h)),
                  pl.BlockSpec((ctx, head), lambda b, h, i: (nb * (seq // ctx) + b, h)),
                  pl.BlockSpec((seq, head), lambda b, h, i: (b, h)),
                  pl.BlockSpec((ctx, head), lambda b, h, i: (nb * (seq // ctx) + b, h))],
        out_specs=pl.BlockSpec((tq, gw), lambda b, h, i: (b * nq + i, h)),
        out_shape=jax.ShapeDtypeStruct(q.shape, BF16),
        compiler_params=_params("parallel", "parallel", "parallel"), name="attention",
    )(q, k, k, v, v)


def _new_expert(i, be_ref):
    return jnp.logical_or(i == 0, be_ref[i] != be_ref[jnp.maximum(i - 1, 0)])


def _moe_up_kernel(be_ref, nu_ref, x_ref, w1_ref, w3_ref, mid_ref, w1_bf, w3_bf):
    i = pl.program_id(1)

    @pl.when(_new_expert(i, be_ref))
    def _():
        w1_bf[...] = w1_ref[...].astype(BF16)
        w3_bf[...] = w3_ref[...].astype(BF16)

    @pl.when(i < nu_ref[0])
    def _():
        x = x_ref[...]
        a = jnp.dot(x, w1_bf[...], preferred_element_type=F32)
        b = jnp.dot(x, w3_bf[...], preferred_element_type=F32)
        mid_ref[...] = (a * jax.nn.sigmoid(a) * b).astype(mid_ref.dtype)

    @pl.when(i >= nu_ref[0])
    def _():
        mid_ref[...] = jnp.zeros_like(mid_ref)


def _moe_down_kernel(be_ref, nu_ref, mid_ref, w2_ref, o_ref, w2_bf):
    i = pl.program_id(1)

    @pl.when(_new_expert(i, be_ref))
    def _():
        w2_bf[...] = w2_ref[...].astype(BF16)

    @pl.when(i < nu_ref[0])
    def _():
        o_ref[...] = jnp.dot(mid_ref[...], w2_bf[...], preferred_element_type=F32)

    @pl.when(i >= nu_ref[0])
    def _():
        o_ref[...] = jnp.zeros_like(o_ref)


def _moe_ffn(xs, block_expert, n_used, w1, w3, w2, layer, bm):
    n_rows, d = xs.shape
    de = w1.shape[3]
    tde = _tile(de, 512, LANES)
    tn = _tile(d, 2048, LANES)
    nblk = n_rows // bm
    mid = pl.pallas_call(
        _moe_up_kernel,
        grid_spec=pltpu.PrefetchScalarGridSpec(
            num_scalar_prefetch=2, grid=(de // tde, nblk),
            in_specs=[pl.BlockSpec((bm, d), lambda k, i, be, nu: (i, 0)),
                      pl.BlockSpec((None, None, d, tde), lambda k, i, be, nu: (layer, be[i], 0, k)),
                      pl.BlockSpec((None, None, d, tde), lambda k, i, be, nu: (layer, be[i], 0, k))],
            out_specs=pl.BlockSpec((bm, tde), lambda k, i, be, nu: (i, k)),
            scratch_shapes=[pltpu.VMEM((d, tde), BF16), pltpu.VMEM((d, tde), BF16)]),
        out_shape=jax.ShapeDtypeStruct((n_rows, de), BF16),
        compiler_params=_params("arbitrary", "arbitrary"), name="moe_up",
    )(block_expert, n_used, xs, w1, w3)
    return pl.pallas_call(
        _moe_down_kernel,
        grid_spec=pltpu.PrefetchScalarGridSpec(
            num_scalar_prefetch=2, grid=(d // tn, nblk),
            in_specs=[pl.BlockSpec((bm, de), lambda n, i, be, nu: (i, 0)),
                      pl.BlockSpec((None, None, de, tn), lambda n, i, be, nu: (layer, be[i], 0, n))],
            out_specs=pl.BlockSpec((bm, tn), lambda n, i, be, nu: (i, n)),
            scratch_shapes=[pltpu.VMEM((de, tn), BF16)]),
        out_shape=jax.ShapeDtypeStruct((n_rows, d), F32),
        compiler_params=_params("arbitrary", "arbitrary"), name="moe_down",
    )(block_expert, n_used, mid, w2)


def _dispatch_plan(idx, n_experts, bm):
    n_tok = idx.shape[1]
    n_assign = n_tok * TOP_K
    expert = idx.T.reshape(-1)
    onehot = (expert[:, None] == jnp.arange(n_experts)[None, :]).astype(jnp.int32)
    rank = jnp.take_along_axis(jnp.cumsum(onehot, axis=0) - onehot, expert[:, None], 1)[:, 0]
    counts = jnp.sum(onehot, axis=0)
    padded = (counts + bm - 1) // bm * bm
    pad_end = jnp.cumsum(padded)
    pos = (pad_end - padded)[expert] + rank
    n_blocks = -(-n_assign // bm) + n_experts
    token = jnp.repeat(jnp.arange(n_tok, dtype=jnp.int32), TOP_K)
    row_token = jnp.zeros((n_blocks * bm,), jnp.int32).at[pos].set(token)
    block_expert = jnp.minimum(jnp.searchsorted(pad_end, jnp.arange(n_blocks) * bm, side='right'),
                               n_experts - 1).astype(jnp.int32)
    n_used = (pad_end[-1] // bm).astype(jnp.int32).reshape(1)
    block_expert = jnp.where(jnp.arange(n_blocks) < n_used[0], block_expert,
                             block_expert[jnp.maximum(n_used[0] - 1, 0)])
    return pos.reshape(n_tok, TOP_K), row_token, block_expert, n_used


def _ffn_res_kernel(x_ref, f0_ref, f1_ref, g_ref, gt_ref, o_ref):
    g = g_ref[...]
    f = g[:, 0:1] * f0_ref[...] + g[:, 1:2] * f1_ref[...]
    o_ref[...] = x_ref[...] + gt_ref[...] * f


def _ffn_res_norm_kernel(x_ref, f0_ref, f1_ref, g_ref, gt_ref, nw_ref, o_ref):
    g = g_ref[...]
    f = g[:, 0:1] * f0_ref[...] + g[:, 1:2] * f1_ref[...]
    x = x_ref[...] + gt_ref[...] * f
    o_ref[...] = x * lax.rsqrt(jnp.mean(x * x, axis=-1, keepdims=True) + RMS_EPS) * nw_ref[...]


def _ffn_residual(x, f0, f1, gate, mod3, gate_chunk, n_rows, dims, final_norm=None):
    d = x.shape[1]
    tm = _tile(dims["ctx_rows"], 256, 8)
    grp = dims["group_of"](tm)
    row = pl.BlockSpec((tm, d), lambda i: (i, 0))
    in_specs = [row, row, row, pl.BlockSpec((tm, TOP_K), lambda i: (i, 0)),
                pl.BlockSpec((None, 1, d), lambda i: (grp(i), 0, gate_chunk))]
    args = [x, f0, f1, gate, mod3]
    kern = _ffn_res_kernel
    if final_norm is not None:
        in_specs.append(pl.BlockSpec((1, d), lambda i: (0, 0)))
        args.append(final_norm.reshape(1, d))
        kern = _ffn_res_norm_kernel
    return pl.pallas_call(
        kern, grid=(n_rows // tm,), in_specs=in_specs, out_specs=row,
        out_shape=jax.ShapeDtypeStruct((n_rows, d), F32),
        compiler_params=_params("parallel"), name="ffn_residual",
    )(*args)


def _even_layout(rd, lw, la, lg, mh, dqk, md):
    lwp = _rup(lw + 4 * mh, LANES)
    assert _rup(la, LANES) == lwp
    lay = {"rd": rd, "lg": lg, "lwp": lwp}
    off = 3 * rd
    for name, width in (("mq", mh * dqk), ("mk", mh * dqk), ("mv", md), ("mo", md), ("tail", lg + 2 * lwp)):
        assert off % width == 0
        lay[name] = off
        off += width
    lay["width"] = off
    return lay


def _pad_cols(seg, width):
    return jnp.pad(seg, ((0, 0), (0, width - seg.shape[1])))


def _pack_even_weights(w_in, mu, lay, lw, la, mh, dqk, md):
    rd, lg, lwp = lay["rd"], lay["lg"], lay["lwp"]
    rww = 3 * rd + lw + la + lg
    o = 3 * rd
    w_down, a_down, g_down = w_in[:, o:o + lw], w_in[:, o + lw:o + lw + la], w_in[:, o + lw + la:rww]
    m_end = rww + 2 * mh * dqk + 2 * md
    w_p = jnp.concatenate([w_in[:, :3 * rd], w_in[:, rww:m_end], g_down,
                           _pad_cols(jnp.concatenate([w_down, w_in[:, m_end:]], 1), lwp),
                           _pad_cols(a_down, lwp)], axis=1).astype(BF16)
    mu2 = mu.reshape(1, -1)
    mu_tail = jnp.concatenate([mu2[:, o + lw + la:rww], _pad_cols(mu2[:, o:o + lw], lwp),
                               _pad_cols(mu2[:, o + lw:o + lw + la], lwp)], axis=1)
    return w_p, mu2[:, :3 * rd], mu_tail


def _pad_rows(w, rows):
    return jnp.pad(w, ((0, 0), (0, rows - w.shape[1]), (0, 0)))


def _rope_tables(dims, head):
    seq, nb = dims["seq"], dims["batch"]
    pairs = head // 4
    rows = seq // GRID_W
    row = jnp.repeat(jnp.arange(rows), GRID_W).astype(F32)
    col = jnp.tile(jnp.arange(GRID_W), rows).astype(F32)
    inv = ROPE_THETA ** (-jnp.arange(pairs, dtype=F32) / pairs)
    ang = jnp.concatenate([row[:, None] * inv, col[:, None] * inv], -1)
    cos, sin = jnp.cos(ang), jnp.sin(ang)
    cos_t = jnp.tile(jnp.concatenate([cos, cos], -1), (nb, 1))
    sin_t = jnp.tile(jnp.concatenate([-sin, sin], -1), (nb, 1))
    n_ctx = dims["ctx_rows"]
    return (jnp.concatenate([cos_t, jnp.ones((n_ctx, head), F32)], 0),
            jnp.concatenate([sin_t, jnp.zeros((n_ctx, head), F32)], 0))


def kernel(x, c, ctx, c_ctx, mod_w, mod_b, norm_mix, norm_ffn, norm_final, router_w, router_bias,
           exp_w1, exp_w3, exp_w2, ev_w_in, ev_w_out, rw_mu, rw_w0, rw_w_up, rw_a0, rw_a_up, rw_g_up,
           rw_k_k, rw_k_a, rw_r_k, rw_gn_w, rw_gn_b, ml_gate_b, ml_norm, at_w_qkv, at_q_norm, at_k_norm,
           at_w_o):
    nb, seq, d = x.shape
    n_ctx = ctx.shape[1]
    depth = mod_w.shape[0]
    lat_rows, ctx_rows = nb * seq, nb * n_ctx
    all_rows = lat_rows + ctx_rows
    assert nb + 1 <= MOD_ROWS and seq % n_ctx == 0 and n_ctx % CHUNK == 0
    dims = {"batch": nb, "seq": seq, "ctx": n_ctx, "lat_rows": lat_rows, "ctx_rows": ctx_rows,
            "group_of": lambda tm: (lambda i: jnp.minimum(i * tm // seq, nb))}

    n_experts = router_w.shape[1]
    rd = rw_w0.shape[-1]
    lw, la, lg = rw_w_up.shape[2], rw_a_up.shape[2], rw_g_up.shape[1]
    mh = ml_gate_b.shape[-1]
    md = ml_norm.shape[-1]
    dv = md // mh
    dqk = dv // 2
    head = at_q_norm.shape[-1]
    kv_dim = (at_w_qkv.shape[-1] - d) // 2
    group = (d // head) // (kv_dim // head)
    moe_bm = 256

    xa = jnp.concatenate([x.reshape(lat_rows, d), ctx.reshape(ctx_rows, d)], axis=0)
    cond = jnp.zeros((MOD_ROWS, d), F32).at[:nb].set(c).at[nb].set(c_ctx)
    lay = _even_layout(rd, lw, la, lg, mh, dqk, md)
    mods = _mod_tables(cond, mod_w, mod_b)
    out = None

    for layer in range(depth):
        ctx_out = layer < depth - 1
        j = layer // 2
        rows_out = all_rows if ctx_out else lat_rows
        mod3 = mods[layer].reshape(MOD_ROWS, 1, 6 * d)

        h = _norm_mod(xa, norm_mix[layer], mod3, 0, all_rows, dims)[0]
        if layer % 2 == 0:
            w_p, mu_main, mu_tail = _pack_even_weights(ev_w_in[j], rw_mu[j], lay, lw, la, mh, dqk, md)
            z = _proj(h, w_p, all_rows, F32)
            r, v, kkn, lwd, km, bb, bonus, gmul = _rwkv_prep(
                z, all_rows, lay, dims, mu_main, mu_tail, rw_w0[j], _pad_rows(rw_w_up[j], lay["lwp"]).astype(BF16),
                rw_a0[j], _pad_rows(rw_a_up[j], lay["lwp"]).astype(BF16), rw_g_up[j].astype(BF16),
                rw_k_k[j], rw_k_a[j], rw_r_k[j])
            y = _rwkv_scan(r, v, kkn, lwd, km, bb, dims, heads_per_step=min(32, rd // RWKV_HEAD))
            g_off = lay["tail"] + lay["lg"] + lw
            gates = z[:, g_off:g_off + 4 * mh].reshape(all_rows, 2, 2 * mh)
            gates_col = jnp.moveaxis(gates, 1, 0).reshape(2, all_rows // CHUNK, CHUNK, 2 * mh)
            gates_row = jnp.swapaxes(gates_col, 2, 3)
            bias = ml_gate_b[j].reshape(2, 2 * mh)
            hm = _mlstm_scan(z, lay, dims, gates_col, gates_row, bias.reshape(2, 1, 2 * mh),
                             bias.reshape(2, 2 * mh, 1), mh, dqk, dv)
            mix = _even_finish(y, bonus, gmul, rw_gn_w[j], rw_gn_b[j], hm, z, lay, ml_norm[j],
                               rows_out, dims, mh, dv)
            xa_new = _proj_residual(mix, ev_w_out[j].astype(BF16), xa, mod3, 2, rows_out, dims)
        else:
            w_qkv = at_w_qkv[j].astype(BF16)
            cos_t, sin_t = _rope_tables(dims, head)
            q_scale = np.float32(head ** -0.5 * np.log2(np.e))
            q = _proj_qk(h, w_qkv, 0, d, at_q_norm[j], cos_t, sin_t, q_scale, lat_rows)
            k = _proj_qk(h, w_qkv, d, kv_dim, at_k_norm[j], cos_t, sin_t, np.float32(1.0), all_rows)
            v = _proj(h, w_qkv, all_rows, BF16, col_off=d + kv_dim, n_cols=kv_dim)
            att = _attention(q, k, v, dims, head, group)
            xa_new = _proj_residual(att, at_w_o[j].astype(BF16), xa, mod3, 2, lat_rows, dims)
            if ctx_out:
                raise NotImplementedError("context output of an attention layer")
        xa = xa_new

        h2, idx, gate_t = _norm_mod(xa, norm_ffn[layer], mod3, 3, rows_out, dims,
                                    router=(router_w, router_bias))
        gate = gate_t.T
        pos, row_token, block_expert, n_used = _dispatch_plan(idx, n_experts, moe_bm)
        xs = h2.at[row_token].get(mode="promise_in_bounds")
        ys = _moe_ffn(xs, block_expert, n_used, exp_w1, exp_w3, exp_w2, layer, moe_bm)
        f0 = ys.at[pos[:, 0]].get(mode="promise_in_bounds")
        f1 = ys.at[pos[:, 1]].get(mode="promise_in_bounds")
        last = layer == depth - 1
        xa = _ffn_residual(xa, f0, f1, gate, mod3, 5, rows_out, dims,
                           final_norm=norm_final if last else None)
        if last:
            out = xa[:lat_rows].reshape(nb, seq, d)
    return out
```

```python
import functools

import jax
import jax.numpy as jnp
import numpy as np
from jax import lax
from jax.experimental import pallas as pl
from jax.experimental.pallas import tpu as pltpu

F32 = jnp.float32
BF16 = jnp.bfloat16
HIGHEST = lax.Precision.HIGHEST

GRID_W = 64
RMS_EPS = 1e-6
GN_EPS = 64e-5
RWKV_HEAD = 64
CHUNK = 64
N_GROUPS = 4
TOP_K = 2
ROPE_THETA = 10000.0
LANES = 128
MOD_ROWS = 8
VMEM_LIMIT = 56 * 1024 * 1024


def _rup(n, m):
    return (n + m - 1) // m * m


def _tile(n, pref, quantum):
    t = min(pref, n) // quantum * quantum
    while t >= quantum:
        if n % t == 0:
            return t
        t -= quantum
    return n


def _params(*sem):
    return pltpu.CompilerParams(dimension_semantics=sem, vmem_limit_bytes=VMEM_LIMIT)


def _dot(a, b):
    return jnp.dot(a.astype(BF16), b.astype(BF16), preferred_element_type=F32)


def _dot_nt(a, b):
    return lax.dot_general(a.astype(BF16), b.astype(BF16), (((1,), (1,)), ((), ())),
                           preferred_element_type=F32)


def _dot_tn(a, b):
    return lax.dot_general(a.astype(BF16), b.astype(BF16), (((0,), (0,)), ((), ())),
                           preferred_element_type=F32)


def _dot_f32(a, b):
    return jnp.dot(a, b, preferred_element_type=F32, precision=HIGHEST)


def _pack_bf16_pairs(x):
    half = x.shape[1] // 2
    bits = lax.bitcast_convert_type(x.astype(BF16).astype(F32), jnp.uint32)
    return (bits[:, :half] >> 16) | bits[:, half:]


def _unpack_bf16_pairs(p):
    left = lax.bitcast_convert_type(p << 16, F32)
    right = lax.bitcast_convert_type(p & jnp.uint32(0xFFFF0000), F32)
    return jnp.concatenate([left, right], axis=1).astype(BF16)


def _mod_kernel(c_ref, w_ref, b_ref, o_ref):
    c = c_ref[...]
    c = c * jax.nn.sigmoid(c)
    o_ref[...] = _dot(c, w_ref[...]) + b_ref[...]


def _mod_tables(cond, mod_w, mod_b):
    d = cond.shape[1]
    depth, _, n = mod_w.shape
    tn = _tile(n, 512, LANES)
    return pl.pallas_call(
        _mod_kernel,
        grid=(depth, n // tn),
        in_specs=[pl.BlockSpec((MOD_ROWS, d), lambda l, j: (0, 0)),
                  pl.BlockSpec((None, d, tn), lambda l, j: (l, 0, j)),
                  pl.BlockSpec((None, 1, tn), lambda l, j: (l, 0, j))],
        out_specs=pl.BlockSpec((None, MOD_ROWS, tn), lambda l, j: (l, 0, j)),
        out_shape=jax.ShapeDtypeStruct((depth, MOD_ROWS, n), F32),
        compiler_params=_params("parallel", "parallel"),
        name="mod_tables",
    )(cond, mod_w, mod_b.reshape(depth, 1, n))


def _norm_mod_kernel(x_ref, g_ref, sh_ref, sc_ref, h_ref):
    x = x_ref[...]
    y = x * lax.rsqrt(jnp.mean(x * x, axis=-1, keepdims=True) + RMS_EPS) * g_ref[...]
    h_ref[...] = (y * (1.0 + sc_ref[...]) + sh_ref[...]).astype(h_ref.dtype)


def _top2_sum(a, b, c, d):
    hi1, lo1 = jnp.maximum(a, b), jnp.minimum(a, b)
    hi2, lo2 = jnp.maximum(c, d), jnp.minimum(c, d)
    return jnp.maximum(hi1, hi2) + jnp.maximum(jnp.minimum(hi1, hi2), jnp.maximum(lo1, lo2))


def _first_argmax(vals):
    best_v = vals[0]
    best_i = jnp.zeros(vals[0].shape, jnp.int32)
    for i in range(1, len(vals)):
        better = vals[i] > best_v
        best_i = jnp.where(better, i, best_i)
        best_v = jnp.where(better, vals[i], best_v)
    return best_i, best_v


def _pick(rows, index):
    out = rows[0]
    for i in range(1, len(rows)):
        out = jnp.where(index == i, rows[i], out)
    return out


def _norm_mod_router_kernel(x_ref, g_ref, sh_ref, sc_ref, rwt_ref, rb_ref, h_ref, idx_ref, gate_ref,
                            *, n_experts):
    x = x_ref[...]
    y = x * lax.rsqrt(jnp.mean(x * x, axis=-1, keepdims=True) + RMS_EPS) * g_ref[...]
    h = y * (1.0 + sc_ref[...]) + sh_ref[...]
    h_ref[...] = _pack_bf16_pairs(h)
    logits = lax.dot_general(rwt_ref[...], h, (((1,), (1,)), ((), ())), preferred_element_type=F32,
                             precision=HIGHEST)
    aff_all = jax.nn.sigmoid(logits)
    sel_all = aff_all + rb_ref[...]
    per_group = n_experts // N_GROUPS
    aff = [aff_all[e:e + 1, :] for e in range(n_experts)]
    sel = [sel_all[e:e + 1, :] for e in range(n_experts)]
    assert per_group == 4 and TOP_K == 2
    best, _ = _first_argmax([_top2_sum(*sel[g * per_group:(g + 1) * per_group]) for g in range(N_GROUPS)])
    cand = [_pick([sel[g * per_group + i] for g in range(N_GROUPS)], best) for i in range(per_group)]
    cand_aff = [_pick([aff[g * per_group + i] for g in range(N_GROUPS)], best) for i in range(per_group)]
    i1, _ = _first_argmax(cand)
    i2, _ = _first_argmax([jnp.where(i1 == i, -jnp.inf, cand[i]) for i in range(per_group)])
    g1 = _pick(cand_aff, i1)
    g2 = _pick(cand_aff, i2)
    idx_ref[0:1, :] = best * per_group + i1
    idx_ref[1:2, :] = best * per_group + i2
    gate_ref[0:1, :] = g1 / (g1 + g2)
    gate_ref[1:2, :] = g2 / (g1 + g2)


def _norm_mod(x, g, mod3, shift_chunk, n_rows, dims, router=None):
    d = x.shape[1]
    tm = _tile(dims["ctx_rows"], 256, 8)
    grp = dims["group_of"](tm)
    in_specs = [pl.BlockSpec((tm, d), lambda i: (i, 0)),
                pl.BlockSpec((1, d), lambda i: (0, 0)),
                pl.BlockSpec((None, 1, d), lambda i: (grp(i), 0, shift_chunk)),
                pl.BlockSpec((None, 1, d), lambda i: (grp(i), 0, shift_chunk + 1))]
    args = [x, g.reshape(1, d), mod3, mod3]
    out_specs = [pl.BlockSpec((tm, d), lambda i: (i, 0))]
    out_shape = [jax.ShapeDtypeStruct((n_rows, d), BF16)]
    kern = _norm_mod_kernel
    if router is not None:
        router_w, router_bias = router
        n_experts = router_w.shape[1]
        in_specs += [pl.BlockSpec((n_experts, d), lambda i: (0, 0)),
                     pl.BlockSpec((n_experts, 1), lambda i: (0, 0))]
        args += [router_w.T, router_bias.astype(F32).reshape(n_experts, 1)]
        out_specs = [pl.BlockSpec((tm, d // 2), lambda i: (i, 0))] + [pl.BlockSpec((TOP_K, tm), lambda i: (0, i))] * 2
        out_shape = [jax.ShapeDtypeStruct((n_rows, d // 2), jnp.uint32),
                     jax.ShapeDtypeStruct((TOP_K, n_rows), jnp.int32),
                     jax.ShapeDtypeStruct((TOP_K, n_rows), F32)]
        kern = functools.partial(_norm_mod_router_kernel, n_experts=n_experts)
    return pl.pallas_call(
        kern, grid=(n_rows // tm,), in_specs=in_specs, out_specs=out_specs, out_shape=out_shape,
        compiler_params=_params("parallel"), name="norm_mod",
    )(*args)


def _resident_bf16(w_ref, wb_ref):
    @pl.when(pl.program_id(1) == 0)
    def _():
        wb_ref[...] = w_ref[...].astype(BF16)
    return wb_ref[...]


def _proj_kernel(a_ref, w_ref, o_ref, wb_ref):
    w = _resident_bf16(w_ref, wb_ref)
    o_ref[...] = jnp.dot(a_ref[...], w, preferred_element_type=F32).astype(o_ref.dtype)


def _proj(a, w, n_rows, out_dtype, col_off=0, n_cols=None, tm_pref=512, tn_pref=512):
    k = a.shape[1]
    n_cols = w.shape[1] - col_off if n_cols is None else n_cols
    tm = _tile(n_rows, tm_pref, 8)
    tn = _tile(int(np.gcd(n_cols, col_off)) if col_off else n_cols, tn_pref, LANES)
    off = col_off // tn
    return pl.pallas_call(
        _proj_kernel,
        grid=(n_cols // tn, n_rows // tm),
        in_specs=[pl.BlockSpec((tm, k), lambda j, i: (i, 0)),
                  pl.BlockSpec((k, tn), lambda j, i: (0, j + off))],
        out_specs=pl.BlockSpec((tm, tn), lambda j, i: (i, j)),
        out_shape=jax.ShapeDtypeStruct((n_rows, n_cols), out_dtype),
        scratch_shapes=[pltpu.VMEM((k, tn), BF16)],
        compiler_params=_params("parallel", "arbitrary"), name="proj",
    )(a, w)


def _proj_res_kernel(a_ref, w_ref, x_ref, gt_ref, o_ref, wb_ref):
    w = _resident_bf16(w_ref, wb_ref)
    acc = jnp.dot(a_ref[...], w, preferred_element_type=F32)
    o_ref[...] = x_ref[...] + gt_ref[...] * acc


def _proj_residual(a, w, x, mod3, gate_chunk, n_rows, dims, tn_pref=512):
    k = a.shape[1]
    d = w.shape[1]
    tm = _tile(dims["ctx_rows"], 512, 8)
    tn = _tile(d, tn_pref, LANES)
    grp = dims["group_of"](tm)
    gblk = gate_chunk * (d // tn)
    return pl.pallas_call(
        _proj_res_kernel,
        grid=(d // tn, n_rows // tm),
        in_specs=[pl.BlockSpec((tm, k), lambda j, i: (i, 0)),
                  pl.BlockSpec((k, tn), lambda j, i: (0, j)),
                  pl.BlockSpec((tm, tn), lambda j, i: (i, j)),
                  pl.BlockSpec((None, 1, tn), lambda j, i: (grp(i), 0, gblk + j))],
        out_specs=pl.BlockSpec((tm, tn), lambda j, i: (i, j)),
        out_shape=jax.ShapeDtypeStruct((n_rows, d), F32),
        scratch_shapes=[pltpu.VMEM((k, tn), BF16)],
        compiler_params=_params("parallel", "arbitrary"), name="proj_residual",
    )(a, w, x, mod3)


def _proj_qk_kernel(a_ref, w_ref, nw_ref, cos_ref, sin_ref, o_ref, wb_ref, *, head, scale):
    w = _resident_bf16(w_ref, wb_ref)
    nw = nw_ref[...] * scale
    tm = a_ref.shape[0]
    rows = _tile(tm, 128, 8)
    for r in range(0, tm, rows):
        acc = jnp.dot(a_ref[r:r + rows, :], w, preferred_element_type=F32)
        cs = cos_ref[r:r + rows, :]
        sn = sin_ref[r:r + rows, :]
        for s in range(acc.shape[1] // head):
            x = acc[:, s * head:(s + 1) * head]
            xn = x * lax.rsqrt(jnp.mean(x * x, axis=-1, keepdims=True) + RMS_EPS) * nw
            xr = xn * cs + pltpu.roll(xn, head // 2, 1) * sn
            o_ref[r:r + rows, s * head:(s + 1) * head] = xr.astype(o_ref.dtype)


def _proj_qk(a, w, col_off, n_cols, norm_w, cos_t, sin_t, scale, n_rows):
    k = a.shape[1]
    head = norm_w.shape[0]
    tm = _tile(n_rows, 512, 8)
    tn = _tile(int(np.gcd(n_cols, col_off)) if col_off else n_cols, 512, head)
    off = col_off // tn
    return pl.pallas_call(
        functools.partial(_proj_qk_kernel, head=head, scale=scale),
        grid=(n_cols // tn, n_rows // tm),
        in_specs=[pl.BlockSpec((tm, k), lambda j, i: (i, 0)),
                  pl.BlockSpec((k, tn), lambda j, i: (0, j + off)),
                  pl.BlockSpec((1, head), lambda j, i: (0, 0)),
                  pl.BlockSpec((tm, head), lambda j, i: (i, 0)),
                  pl.BlockSpec((tm, head), lambda j, i: (i, 0))],
        out_specs=pl.BlockSpec((tm, tn), lambda j, i: (i, j)),
        out_shape=jax.ShapeDtypeStruct((n_rows, n_cols), BF16),
        scratch_shapes=[pltpu.VMEM((k, tn), BF16)],
        compiler_params=_params("parallel", "arbitrary"), name="proj_qk",
    )(a, w, norm_w.reshape(1, head), cos_t, sin_t)


def _seg_sum(x, seg):
    n = x.shape[1]
    lane_blk = LANES if n % LANES == 0 else n
    r = lax.broadcasted_iota(jnp.int32, (lane_blk, lane_blk), 0) // seg
    c = lax.broadcasted_iota(jnp.int32, (lane_blk, lane_blk), 1) // seg
    ones_bd = (r == c).astype(F32)
    parts = [_dot_f32(x[:, s:s + lane_blk], ones_bd) for s in range(0, n, lane_blk)]
    return parts[0] if len(parts) == 1 else jnp.concatenate(parts, axis=1)


def _rwkv_prep_kernel(z_ref, zp_ref, zn_ref, t_ref, tp_ref, tn_ref, mu_ref, mut_ref, w0_ref, wup_ref,
                      a0_ref, aup_ref, gup_ref, kk_ref, ka_ref, rk_ref,
                      r_out, v_out, kkn_out, lw_out, km_out, bb_out, bon_out, gm_out,
                      *, rd, lg, lwp, seq, ctx, n_lat_rows):
    tm = z_ref.shape[0]
    row0 = pl.program_id(0) * tm
    in_lat = row0 < n_lat_rows
    seq_len = jnp.where(in_lat, seq, ctx)
    pos0 = jnp.where(in_lat, row0 % seq, (row0 - n_lat_rows) % ctx)
    has_prev = (pos0 != 0).astype(F32)
    has_next = (pos0 + tm != seq_len).astype(F32)

    def token_shift(cur_ref, prev_ref, next_ref, mix_ref):
        cur = cur_ref[...]
        rows = lax.broadcasted_iota(jnp.int32, cur.shape, 0)
        prev_row = prev_ref[7:8, :] * has_prev
        next_row = next_ref[0:1, :] * has_next
        before = jnp.where(rows == 0, prev_row, pltpu.roll(cur, 1, 0))
        after = jnp.where(rows == tm - 1, next_row, pltpu.roll(cur, tm - 1, 0))
        return cur + mix_ref[...] * (0.5 * (before + after) - cur)

    zs = token_shift(z_ref, zp_ref, zn_ref, mu_ref)
    ts = token_shift(t_ref, tp_ref, tn_ref, mut_ref)
    r = zs[:, 0:rd]
    k = zs[:, rd:2 * rd]
    v = zs[:, 2 * rd:3 * rd]
    g_down = ts[:, 0:lg]
    w_down = ts[:, lg:lg + lwp]
    a_down = ts[:, lg + lwp:lg + 2 * lwp]

    kk = k * kk_ref[...]
    kk = kk * lax.rsqrt(jnp.maximum(_seg_sum(kk * kk, RWKV_HEAD), 1e-24))
    r_out[...] = r
    v_out[...] = v
    kkn_out[...] = kk
    tw = jnp.tanh(w_down)
    bonus = jnp.zeros_like(r)
    for d in range(2):
        lw_out[d] = -np.float32(np.exp(-0.5)) * jax.nn.sigmoid(w0_ref[d] + _dot(tw, wup_ref[d]))
        a = jax.nn.sigmoid(a0_ref[d] + _dot(a_down, aup_ref[d]))
        k_mod = k * (1.0 + (a - 1.0) * ka_ref[...])
        km_out[d] = k_mod
        bb_out[d] = kk * a
        bonus = bonus + _seg_sum(r * k_mod * rk_ref[...], RWKV_HEAD) * v
    bon_out[...] = bonus
    gm_out[...] = _dot(jax.nn.sigmoid(g_down), gup_ref[...])


def _rwkv_prep(z, n_rows, lay, dims, mu_main, mu_tail, w0, w_up_p, a0, a_up_p, g_up, k_k, k_a, r_k):
    rd, lg, lwp = lay["rd"], lay["lg"], lay["lwp"]
    mw = 3 * rd
    tw = lg + 2 * lwp
    tblk = lay["tail"] // tw
    tm = _tile(dims["ctx"], 64, 8)
    nb8 = n_rows // 8
    full = lambda shape: pl.BlockSpec(shape, lambda i: (0,) * len(shape))
    row_spec = pl.BlockSpec((tm, rd), lambda i: (i, 0))
    dir_spec = pl.BlockSpec((2, tm, rd), lambda i: (0, i, 0))
    sds = jax.ShapeDtypeStruct
    kern = functools.partial(_rwkv_prep_kernel, rd=rd, lg=lg, lwp=lwp, seq=dims["seq"], ctx=dims["ctx"],
                             n_lat_rows=dims["lat_rows"])
    return pl.pallas_call(
        kern,
        grid=(n_rows // tm,),
        in_specs=[pl.BlockSpec((tm, mw), lambda i: (i, 0)),
                  pl.BlockSpec((8, mw), lambda i: (jnp.maximum(i * (tm // 8) - 1, 0), 0)),
                  pl.BlockSpec((8, mw), lambda i: (jnp.minimum((i + 1) * (tm // 8), nb8 - 1), 0)),
                  pl.BlockSpec((tm, tw), lambda i: (i, tblk)),
                  pl.BlockSpec((8, tw), lambda i: (jnp.maximum(i * (tm // 8) - 1, 0), tblk)),
                  pl.BlockSpec((8, tw), lambda i: (jnp.minimum((i + 1) * (tm // 8), nb8 - 1), tblk)),
                  full((1, mw)), full((1, tw)), full((2, 1, rd)), full((2, lwp, rd)), full((2, 1, rd)), full((2, lwp, rd)),
                  full((lg, rd)), full((1, rd)), full((1, rd)), full((1, rd))],
        out_specs=[row_spec, row_spec, row_spec, dir_spec, dir_spec, dir_spec, row_spec, row_spec],
        out_shape=[sds((n_rows, rd), F32)] * 3 + [sds((2, n_rows, rd), F32)] * 3 + [sds((n_rows, rd), F32)] * 2,
        compiler_params=_params("parallel"), name="rwkv_prep",
    )(z, z, z, z, z, z, mu_main, mu_tail, w0.reshape(2, 1, rd), w_up_p, a0.reshape(2, 1, rd), a_up_p, g_up,
      k_k.reshape(1, rd), k_a.reshape(1, rd), r_k.reshape(1, rd))


def _rwkv_scan_kernel(r_ref, v_ref, kk_ref, lw_ref, km_ref, bb_ref, y_ref, s_ref, *, heads):
    d = pl.program_id(0)
    j = pl.program_id(3)

    @pl.when(j == 0)
    def _():
        s_ref[...] = jnp.zeros_like(s_ref)

    n = RWKV_HEAD
    chunk = lw_ref.shape[0]
    t_idx = lax.broadcasted_iota(jnp.int32, (chunk, chunk), 0)
    s_idx = lax.broadcasted_iota(jnp.int32, (chunk, chunk), 1)
    lead = jnp.where(d == 0, t_idx - s_idx, s_idx - t_idx)
    incl = lead >= 0
    assert chunk & (chunk - 1) == 0

    lw = lw_ref[...]
    cum = _dot_f32(incl.astype(F32), lw)
    tot = jnp.sum(lw, axis=0, keepdims=True)
    e_in = jnp.exp(cum)
    e_neg = jnp.exp(-cum)
    e_last = jnp.exp(tot - cum)
    kkn = kk_ref[...]
    km = km_ref[...]
    bb = bb_ref[...]
    r_t = r_ref[...] * e_in
    a_t = -kkn * jnp.exp(cum - lw)
    b_t = bb * e_neg
    k_t = km * e_neg
    b_l = bb * e_last
    k_l = km * e_last
    w_l = jnp.exp(tot)
    v_all = v_ref[...]

    hs = range(heads)
    sl = [slice(h * n, (h + 1) * n) for h in hs]
    t2 = lax.broadcasted_iota(jnp.int32, (2 * chunk, 2 * chunk), 0)
    s2 = lax.broadcasted_iota(jnp.int32, (2 * chunk, 2 * chunk), 1)
    tt = jnp.where(t2 >= chunk, t2 - chunk, t2)
    ss = jnp.where(s2 >= chunk, s2 - chunk, s2)
    keep = jnp.where(d == 0, tt - ss, ss - tt) >= jnp.where(t2 < chunk, 1, 0)
    right = lax.broadcasted_iota(jnp.int32, (chunk, 2 * chunk), 1) >= chunk
    eye_right = (lax.broadcasted_iota(jnp.int32, (chunk, 2 * chunk), 1)
                 == lax.broadcasted_iota(jnp.int32, (chunk, 2 * chunk), 0) + chunk).astype(F32)
    zeros_v = jnp.zeros((chunk, n), F32)
    stack = lambda top, bottom: jnp.concatenate([top, bottom], axis=0)

    s0 = [s_ref[h] for h in hs]
    v = [v_all[:, sl[h]] for h in hs]
    pair = [jnp.where(keep, _dot_nt(stack(a_t[:, sl[h]], r_t[:, sl[h]]), stack(b_t[:, sl[h]], k_t[:, sl[h]])), 0.0)
            for h in hs]
    a_side = [pair[h][:chunk] for h in hs]
    r_side = [pair[h][chunk:] for h in hs]
    x = [_dot_nt(a_t[:, sl[h]], s0[h]) + _dot(a_side[h], stack(zeros_v, v[h])) for h in hs]
    y0 = [_dot_nt(r_t[:, sl[h]], s0[h]) for h in hs]
    q = [jnp.where(right, eye_right, a_side[h]) for h in hs]
    for _ in range(chunk.bit_length() - 1):
        q = [_dot(q[h][:, :chunk], q[h]) + jnp.where(right, q[h], 0.0) for h in hs]
    uv = [stack(_dot(q[h][:, chunk:], x[h]), v[h]) for h in hs]
    y_ref[...] = jnp.concatenate([y0[h] + _dot(r_side[h], uv[h]) for h in hs], axis=1)
    for h in hs:
        s_ref[h] = s0[h] * w_l[:, sl[h]] + _dot_tn(uv[h], stack(b_l[:, sl[h]], k_l[:, sl[h]]))


def _chunk_index(d, b, j, dims):
    nc_ctx, nc_lat, nb = dims["ctx"] // CHUNK, dims["seq"] // CHUNK, dims["batch"]
    jc = jnp.where(d == 0, j, nc_ctx - 1 - j)
    jl = jnp.where(d == 0, j - nc_ctx, nc_lat - 1 - (j - nc_ctx))
    return jnp.where(j < nc_ctx, nb * nc_lat + b * nc_ctx + jc, b * nc_lat + jl)


def _rwkv_scan(r, v, kkn, lw, km, bb, dims, heads_per_step):
    n_rows, rd = r.shape
    gw = heads_per_step * RWKV_HEAD
    n_chunks = (dims["ctx"] + dims["seq"]) // CHUNK
    cidx = lambda d, b, g, j: _chunk_index(d, b, j, dims)
    row_spec = pl.BlockSpec((CHUNK, gw), lambda d, b, g, j: (cidx(d, b, g, j), g))
    dir_spec = pl.BlockSpec((None, CHUNK, gw), lambda d, b, g, j: (d, cidx(d, b, g, j), g))
    return pl.pallas_call(
        functools.partial(_rwkv_scan_kernel, heads=heads_per_step),
        grid=(2, dims["batch"], rd // gw, n_chunks),
        in_specs=[row_spec, row_spec, row_spec, dir_spec, dir_spec, dir_spec],
        out_specs=dir_spec,
        out_shape=jax.ShapeDtypeStruct((2, n_rows, rd), F32),
        scratch_shapes=[pltpu.VMEM((heads_per_step, RWKV_HEAD, RWKV_HEAD), F32)],
        compiler_params=_params("parallel", "parallel", "parallel", "arbitrary"), name="rwkv_scan",
    )(r, v, kkn, lw, km, bb)


def _mlstm_kernel(q_ref, k_ref, v_ref, gc_ref, gr_ref, bc_ref, br_ref, h_ref, c_ref, n_ref, m_ref,
                  *, heads, dqk, dv):
    d = pl.program_id(0)
    j = pl.program_id(2)

    @pl.when(j == 0)
    def _():
        c_ref[...] = jnp.zeros_like(c_ref)
        n_ref[...] = jnp.zeros_like(n_ref)
        m_ref[...] = jnp.zeros_like(m_ref)

    chunk = q_ref.shape[0]
    t_idx = lax.broadcasted_iota(jnp.int32, (chunk, chunk), 0)
    s_idx = lax.broadcasted_iota(jnp.int32, (chunk, chunk), 1)
    lead = jnp.where(d == 0, t_idx - s_idx, s_idx - t_idx)
    incl = lead >= 0
    tri = incl.astype(F32)
    tri_t = (lead <= 0).astype(F32)

    gcol = gc_ref[...] + bc_ref[...]
    grow = gr_ref[...] + br_ref[...]
    i_col = gcol[:, :heads]
    f_col = jax.nn.log_sigmoid(gcol[:, heads:])
    i_row = grow[:heads, :]
    f_row = jax.nn.log_sigmoid(grow[heads:, :])
    b_col = _dot_f32(tri, f_col)
    b_row = _dot_f32(f_row, tri_t)
    b_last = jnp.sum(f_col, axis=0, keepdims=True)
    scale = np.float32(dqk ** -0.5)

    hs = range(heads)
    q = [q_ref[:, h * dqk:(h + 1) * dqk] * scale for h in hs]
    k = [k_ref[:, h * dqk:(h + 1) * dqk] for h in hs]
    v = [v_ref[:, h * dv:(h + 1) * dv] for h in hs]
    c_st = [c_ref[h] for h in hs]
    n_st = [n_ref[h] for h in hs]
    m_st = [m_ref[h][:, 0:1] for h in hs]
    bc = [b_col[:, h:h + 1] for h in hs]
    qk = [_dot_nt(q[h], k[h]) for h in hs]
    qc = [_dot(q[h], c_st[h]) for h in hs]
    dmat = [jnp.where(incl, bc[h] + (i_row[h:h + 1, :] - b_row[h:h + 1, :]), -jnp.inf) for h in hs]
    inter = [bc[h] + m_st[h] for h in hs]
    m_t = [jnp.maximum(inter[h], jnp.max(dmat[h], axis=-1, keepdims=True)) for h in hs]
    w_inter = [jnp.exp(inter[h] - m_t[h]) for h in hs]
    s = [qk[h] * jnp.exp(dmat[h] - m_t[h]) for h in hs]
    num = [w_inter[h] * qc[h] + _dot(s[h], v[h]) for h in hs]
    den = [w_inter[h] * jnp.sum(q[h] * n_st[h], axis=-1, keepdims=True) + jnp.sum(s[h], axis=-1, keepdims=True)
           for h in hs]
    h_ref[...] = jnp.concatenate([num[h] / jnp.maximum(jnp.abs(den[h]), jnp.exp(-m_t[h])) for h in hs], axis=1)
    bl = [b_last[:, h:h + 1] for h in hs]
    g = [bl[h] - bc[h] + i_col[:, h:h + 1] for h in hs]
    m_new = [jnp.maximum(bl[h] + m_st[h], jnp.max(g[h], axis=0, keepdims=True)) for h in hs]
    decay = [jnp.exp(bl[h] + m_st[h] - m_new[h]) for h in hs]
    kw = [k[h] * jnp.exp(g[h] - m_new[h]) for h in hs]
    for h in hs:
        c_ref[h] = decay[h] * c_st[h] + _dot_tn(kw[h], v[h])
        n_ref[h] = decay[h] * n_st[h] + jnp.sum(kw[h], axis=0, keepdims=True)
        m_ref[h] = jnp.broadcast_to(m_new[h], m_ref.shape[1:])


def _mlstm_scan(z, lay, dims, gates_col, gates_row, bias_col, bias_row, heads, dqk, dv):
    n_rows = z.shape[0]
    n_chunks = (dims["ctx"] + dims["seq"]) // CHUNK
    md = heads * dv
    qw = heads * dqk
    cidx = lambda d, b, j: _chunk_index(d, b, j, dims)
    q_blk, k_blk, v_blk = lay["mq"] // qw, lay["mk"] // qw, lay["mv"] // md
    return pl.pallas_call(
        functools.partial(_mlstm_kernel, heads=heads, dqk=dqk, dv=dv),
        grid=(2, dims["batch"], n_chunks),
        in_specs=[pl.BlockSpec((CHUNK, qw), lambda d, b, j: (cidx(d, b, j), q_blk)),
                  pl.BlockSpec((CHUNK, qw), lambda d, b, j: (cidx(d, b, j), k_blk)),
                  pl.BlockSpec((CHUNK, md), lambda d, b, j: (cidx(d, b, j), v_blk)),
                  pl.BlockSpec((None, None, CHUNK, 2 * heads), lambda d, b, j: (d, cidx(d, b, j), 0, 0)),
                  pl.BlockSpec((None, None, 2 * heads, CHUNK), lambda d, b, j: (d, cidx(d, b, j), 0, 0)),
                  pl.BlockSpec((None, 1, 2 * heads), lambda d, b, j: (d, 0, 0)),
                  pl.BlockSpec((None, 2 * heads, 1), lambda d, b, j: (d, 0, 0))],
        out_specs=pl.BlockSpec((None, CHUNK, md), lambda d, b, j: (d, cidx(d, b, j), 0)),
        out_shape=jax.ShapeDtypeStruct((2, n_rows, md), F32),
        scratch_shapes=[pltpu.VMEM((heads, dqk, dv), F32), pltpu.VMEM((heads, 1, dqk), F32),
                        pltpu.VMEM((heads, 1, LANES), F32)],
        compiler_params=_params("parallel", "parallel", "arbitrary"), name="mlstm_scan",
    )(z, z, z, gates_col, gates_row, bias_col, bias_row)


def _even_finish_kernel(y_ref, bon_ref, gm_ref, gnw_ref, gnb_ref, h_ref, o_ref, nw_ref, out_ref,
                        *, rd, heads, dv):
    y = y_ref[0] + y_ref[1]
    inv_n = np.float32(1.0 / RWKV_HEAD)
    mean = _seg_sum(y, RWKV_HEAD) * inv_n
    yc = y - mean
    var = _seg_sum(yc * yc, RWKV_HEAD) * inv_n
    yn = yc * lax.rsqrt(var + GN_EPS) * gnw_ref[...] + gnb_ref[...] + bon_ref[...]
    out_ref[:, 0:rd] = (yn * gm_ref[...]).astype(out_ref.dtype)
    hm = h_ref[0] + h_ref[1]
    for h in range(heads):
        sl = slice(h * dv, (h + 1) * dv)
        x = hm[:, sl]
        xn = x * lax.rsqrt(jnp.mean(x * x, axis=-1, keepdims=True) + RMS_EPS) * nw_ref[:, sl]
        out_ref[:, rd + h * dv:rd + (h + 1) * dv] = (jax.nn.sigmoid(o_ref[:, sl]) * xn).astype(out_ref.dtype)


def _even_finish(y, bonus, gmul, gn_w, gn_b, hm, z, lay, ml_norm, n_rows, dims, heads, dv):
    rd = y.shape[2]
    md = hm.shape[2]
    tm = _tile(dims["ctx_rows"], 256, 8)
    o_blk = lay["mo"] // md
    return pl.pallas_call(
        functools.partial(_even_finish_kernel, rd=rd, heads=heads, dv=dv),
        grid=(n_rows // tm,),
        in_specs=[pl.BlockSpec((2, tm, rd), lambda i: (0, i, 0)),
                  pl.BlockSpec((tm, rd), lambda i: (i, 0)),
                  pl.BlockSpec((tm, rd), lambda i: (i, 0)),
                  pl.BlockSpec((1, rd), lambda i: (0, 0)),
                  pl.BlockSpec((1, rd), lambda i: (0, 0)),
                  pl.BlockSpec((2, tm, md), lambda i: (0, i, 0)),
                  pl.BlockSpec((tm, md), lambda i: (i, o_blk)),
                  pl.BlockSpec((1, md), lambda i: (0, 0))],
        out_specs=pl.BlockSpec((tm, rd + md), lambda i: (i, 0)),
        out_shape=jax.ShapeDtypeStruct((n_rows, rd + md), BF16),
        compiler_params=_params("parallel"), name="even_finish",
    )(y, bonus, gmul, gn_w.reshape(1, rd), gn_b.reshape(1, rd), hm, z, ml_norm.reshape(1, md))


def _attn_kernel(q_ref, kl_ref, kc_ref, vl_ref, vc_ref, o_ref, *, group, head):
    tq = q_ref.shape[0]
    q = jnp.concatenate([q_ref[:, g * head:(g + 1) * head] for g in range(group)], axis=0)
    ck = _tile(kl_ref.shape[0], 256, 8)
    keys = [kc_ref[...]] + [kl_ref[c * ck:(c + 1) * ck, :] for c in range(kl_ref.shape[0] // ck)]
    vals = [vc_ref[...]] + [vl_ref[c * ck:(c + 1) * ck, :] for c in range(kl_ref.shape[0] // ck)]
    m = acc = None
    for kc, vc in zip(keys, vals):
        s = _dot_nt(q, kc)
        row_max = jnp.max(s, axis=-1, keepdims=True)
        v_aug = jnp.concatenate([vc, jnp.ones_like(vc)], axis=1)
        if m is None:
            m = row_max
            acc = _dot(jnp.exp2(s - m), v_aug)
        else:
            m_new = jnp.maximum(m, row_max)
            acc = acc * jnp.exp2(m - m_new) + _dot(jnp.exp2(s - m_new), v_aug)
            m = m_new
    o = acc[:, :head] / acc[:, head:head + 1]
    for g in range(group):
        o_ref[:, g * head:(g + 1) * head] = o[g * tq:(g + 1) * tq].astype(o_ref.dtype)


def _attention(q, k, v, dims, head, group):
    seq, ctx, nb = dims["seq"], dims["ctx"], dims["batch"]
    kvh = k.shape[1] // head
    tq = _tile(seq, 256, 8)
    nq = seq // tq
    gw = group * head
    return pl.pallas_call(
        functools.partial(_attn_kernel, group=group, head=head),
        grid=(nb, kvh, nq),
        in_specs=[pl.BlockSpec((tq, gw), lambda b, h, i: (b * nq + i, h)),
                  pl.BlockSpec((seq, head), lambda b, h, i: (b, h)),
                  pl.BlockSpec((ctx, head), lambda b, h, i: (nb * (seq // ctx) + b, h)),
                  pl.BlockSpec((seq, head), lambda b, h, i: (b, h)),
                  pl.BlockSpec((ctx, head), lambda b, h, i: (nb * (seq // ctx) + b, h))],
        out_specs=pl.BlockSpec((tq, gw), lambda b, h, i: (b * nq + i, h)),
        out_shape=jax.ShapeDtypeStruct(q.shape, BF16),
        compiler_params=_params("parallel", "parallel", "parallel"), name="attention",
    )(q, k, k, v, v)


def _new_expert(i, be_ref):
    return jnp.logical_or(i == 0, be_ref[i] != be_ref[jnp.maximum(i - 1, 0)])


def _moe_up_kernel(be_ref, nu_ref, x_ref, w1_ref, w3_ref, mid_ref, w1_bf, w3_bf):
    i = pl.program_id(1)

    @pl.when(_new_expert(i, be_ref))
    def _():
        w1_bf[...] = w1_ref[...].astype(BF16)
        w3_bf[...] = w3_ref[...].astype(BF16)

    @pl.when(i < nu_ref[0])
    def _():
        x = _unpack_bf16_pairs(x_ref[...])
        a = jnp.dot(x, w1_bf[...], preferred_element_type=F32)
        b = jnp.dot(x, w3_bf[...], preferred_element_type=F32)
        mid_ref[...] = (a * jax.nn.sigmoid(a) * b).astype(mid_ref.dtype)

    @pl.when(i >= nu_ref[0])
    def _():
        mid_ref[...] = jnp.zeros_like(mid_ref)


def _moe_down_kernel(be_ref, nu_ref, mid_ref, w2_ref, o_ref, w2_bf):
    i = pl.program_id(1)

    @pl.when(_new_expert(i, be_ref))
    def _():
        w2_bf[...] = w2_ref[...].astype(BF16)

    @pl.when(i < nu_ref[0])
    def _():
        o_ref[...] = jnp.dot(mid_ref[...], w2_bf[...], preferred_element_type=F32)

    @pl.when(i >= nu_ref[0])
    def _():
        o_ref[...] = jnp.zeros_like(o_ref)


def _moe_ffn(xs, block_expert, n_used, w1, w3, w2, layer, bm):
    n_rows = xs.shape[0]
    d, de = w1.shape[2], w1.shape[3]
    tde = _tile(de, 512, LANES)
    tn = _tile(d, 2048, LANES)
    nblk = n_rows // bm
    mid = pl.pallas_call(
        _moe_up_kernel,
        grid_spec=pltpu.PrefetchScalarGridSpec(
            num_scalar_prefetch=2, grid=(de // tde, nblk),
            in_specs=[pl.BlockSpec((bm, d // 2), lambda k, i, be, nu: (i, 0)),
                      pl.BlockSpec((None, None, d, tde), lambda k, i, be, nu: (layer, be[i], 0, k)),
                      pl.BlockSpec((None, None, d, tde), lambda k, i, be, nu: (layer, be[i], 0, k))],
            out_specs=pl.BlockSpec((bm, tde), lambda k, i, be, nu: (i, k)),
            scratch_shapes=[pltpu.VMEM((d, tde), BF16), pltpu.VMEM((d, tde), BF16)]),
        out_shape=jax.ShapeDtypeStruct((n_rows, de), BF16),
        compiler_params=_params("arbitrary", "arbitrary"), name="moe_up",
    )(block_expert, n_used, xs, w1, w3)
    return pl.pallas_call(
        _moe_down_kernel,
        grid_spec=pltpu.PrefetchScalarGridSpec(
            num_scalar_prefetch=2, grid=(d // tn, nblk),
            in_specs=[pl.BlockSpec((bm, de), lambda n, i, be, nu: (i, 0)),
                      pl.BlockSpec((None, None, de, tn), lambda n, i, be, nu: (layer, be[i], 0, n))],
            out_specs=pl.BlockSpec((bm, tn), lambda n, i, be, nu: (i, n)),
            scratch_shapes=[pltpu.VMEM((de, tn), BF16)]),
        out_shape=jax.ShapeDtypeStruct((n_rows, d), F32),
        compiler_params=_params("arbitrary", "arbitrary"), name="moe_down",
    )(block_expert, n_used, mid, w2)


def _dispatch_plan(idx, n_experts, bm):
    n_tok = idx.shape[1]
    n_assign = n_tok * TOP_K
    expert = idx.T.reshape(-1)
    onehot = (expert[:, None] == jnp.arange(n_experts)[None, :]).astype(jnp.int32)
    rank = jnp.take_along_axis(jnp.cumsum(onehot, axis=0) - onehot, expert[:, None], 1)[:, 0]
    counts = jnp.sum(onehot, axis=0)
    padded = (counts + bm - 1) // bm * bm
    pad_end = jnp.cumsum(padded)
    pos = (pad_end - padded)[expert] + rank
    n_blocks = -(-n_assign // bm) + n_experts
    token = jnp.repeat(jnp.arange(n_tok, dtype=jnp.int32), TOP_K)
    row_token = jnp.zeros((n_blocks * bm,), jnp.int32).at[pos].set(token)
    block_start = jnp.arange(n_blocks, dtype=jnp.int32) * bm
    block_expert = jnp.minimum(jnp.sum((pad_end[None, :] <= block_start[:, None]).astype(jnp.int32), axis=1),
                               n_experts - 1)
    n_used = (pad_end[-1] // bm).astype(jnp.int32).reshape(1)
    block_expert = jnp.where(jnp.arange(n_blocks) < n_used[0], block_expert,
                             block_expert[jnp.maximum(n_used[0] - 1, 0)])
    return pos.reshape(n_tok, TOP_K), row_token, block_expert, n_used


def _ffn_res_kernel(x_ref, f0_ref, f1_ref, g_ref, gt_ref, o_ref):
    g = g_ref[...]
    f = g[:, 0:1] * f0_ref[...] + g[:, 1:2] * f1_ref[...]
    o_ref[...] = x_ref[...] + gt_ref[...] * f


def _ffn_res_norm_kernel(x_ref, f0_ref, f1_ref, g_ref, gt_ref, nw_ref, o_ref):
    g = g_ref[...]
    f = g[:, 0:1] * f0_ref[...] + g[:, 1:2] * f1_ref[...]
    x = x_ref[...] + gt_ref[...] * f
    o_ref[...] = x * lax.rsqrt(jnp.mean(x * x, axis=-1, keepdims=True) + RMS_EPS) * nw_ref[...]


def _ffn_residual(x, f0, f1, gate, mod3, gate_chunk, n_rows, dims, final_norm=None):
    d = x.shape[1]
    tm = _tile(dims["ctx_rows"], 256, 8)
    grp = dims["group_of"](tm)
    row = pl.BlockSpec((tm, d), lambda i: (i, 0))
    in_specs = [row, row, row, pl.BlockSpec((tm, TOP_K), lambda i: (i, 0)),
                pl.BlockSpec((None, 1, d), lambda i: (grp(i), 0, gate_chunk))]
    args = [x, f0, f1, gate, mod3]
    kern = _ffn_res_kernel
    if final_norm is not None:
        in_specs.append(pl.BlockSpec((1, d), lambda i: (0, 0)))
        args.append(final_norm.reshape(1, d))
        kern = _ffn_res_norm_kernel
    return pl.pallas_call(
        kern, grid=(n_rows // tm,), in_specs=in_specs, out_specs=row,
        out_shape=jax.ShapeDtypeStruct((n_rows, d), F32),
        compiler_params=_params("parallel"), name="ffn_residual",
    )(*args)


def _even_layout(rd, lw, la, lg, mh, dqk, md):
    lwp = _rup(lw + 4 * mh, LANES)
    assert _rup(la, LANES) == lwp
    lay = {"rd": rd, "lg": lg, "lwp": lwp}
    off = 3 * rd
    for name, width in (("mq", mh * dqk), ("mk", mh * dqk), ("mv", md), ("mo", md), ("tail", lg + 2 * lwp)):
        assert off % width == 0
        lay[name] = off
        off += width
    lay["width"] = off
    return lay


def _pad_cols(seg, width):
    return jnp.pad(seg, ((0, 0), (0, width - seg.shape[1])))


def _pack_even_weights(w_in, mu, lay, lw, la, mh, dqk, md):
    rd, lg, lwp = lay["rd"], lay["lg"], lay["lwp"]
    rww = 3 * rd + lw + la + lg
    o = 3 * rd
    w_down, a_down, g_down = w_in[:, o:o + lw], w_in[:, o + lw:o + lw + la], w_in[:, o + lw + la:rww]
    m_end = rww + 2 * mh * dqk + 2 * md
    w_p = jnp.concatenate([w_in[:, :3 * rd], w_in[:, rww:m_end], g_down,
                           _pad_cols(jnp.concatenate([w_down, w_in[:, m_end:]], 1), lwp),
                           _pad_cols(a_down, lwp)], axis=1).astype(BF16)
    mu2 = mu.reshape(1, -1)
    mu_tail = jnp.concatenate([mu2[:, o + lw + la:rww], _pad_cols(mu2[:, o:o + lw], lwp),
                               _pad_cols(mu2[:, o + lw:o + lw + la], lwp)], axis=1)
    return w_p, mu2[:, :3 * rd], mu_tail


def _pad_rows(w, rows):
    return jnp.pad(w, ((0, 0), (0, rows - w.shape[1]), (0, 0)))


def _rope_tables(dims, head):
    seq, nb = dims["seq"], dims["batch"]
    pairs = head // 4
    rows = seq // GRID_W
    row = np.repeat(np.arange(rows), GRID_W).astype(np.float32)
    col = np.tile(np.arange(GRID_W), rows).astype(np.float32)
    inv = (np.float32(ROPE_THETA) ** (-np.arange(pairs, dtype=np.float32) / np.float32(pairs))).astype(np.float32)
    ang = np.concatenate([row[:, None] * inv, col[:, None] * inv], -1)
    cos, sin = np.cos(ang).astype(np.float32), np.sin(ang).astype(np.float32)
    cos_t = np.tile(np.concatenate([cos, cos], -1), (nb, 1))
    sin_t = np.tile(np.concatenate([-sin, sin], -1), (nb, 1))
    n_ctx = dims["ctx_rows"]
    return (jnp.asarray(np.concatenate([cos_t, np.ones((n_ctx, head), np.float32)], 0)),
            jnp.asarray(np.concatenate([sin_t, np.zeros((n_ctx, head), np.float32)], 0)))


def kernel(x, c, ctx, c_ctx, mod_w, mod_b, norm_mix, norm_ffn, norm_final, router_w, router_bias,
           exp_w1, exp_w3, exp_w2, ev_w_in, ev_w_out, rw_mu, rw_w0, rw_w_up, rw_a0, rw_a_up, rw_g_up,
           rw_k_k, rw_k_a, rw_r_k, rw_gn_w, rw_gn_b, ml_gate_b, ml_norm, at_w_qkv, at_q_norm, at_k_norm,
           at_w_o):
    nb, seq, d = x.shape
    n_ctx = ctx.shape[1]
    depth = mod_w.shape[0]
    lat_rows, ctx_rows = nb * seq, nb * n_ctx
    all_rows = lat_rows + ctx_rows
    assert nb + 1 <= MOD_ROWS and seq % n_ctx == 0 and n_ctx % CHUNK == 0
    dims = {"batch": nb, "seq": seq, "ctx": n_ctx, "lat_rows": lat_rows, "ctx_rows": ctx_rows,
            "group_of": lambda tm: (lambda i: jnp.minimum(i * tm // seq, nb))}

    n_experts = router_w.shape[1]
    rd = rw_w0.shape[-1]
    lw, la, lg = rw_w_up.shape[2], rw_a_up.shape[2], rw_g_up.shape[1]
    mh = ml_gate_b.shape[-1]
    md = ml_norm.shape[-1]
    dv = md // mh
    dqk = dv // 2
    head = at_q_norm.shape[-1]
    kv_dim = (at_w_qkv.shape[-1] - d) // 2
    group = (d // head) // (kv_dim // head)
    moe_bm = 256

    xa = jnp.concatenate([x.reshape(lat_rows, d), ctx.reshape(ctx_rows, d)], axis=0)
    cond = jnp.zeros((MOD_ROWS, d), F32).at[:nb].set(c).at[nb].set(c_ctx)
    lay = _even_layout(rd, lw, la, lg, mh, dqk, md)
    mods = _mod_tables(cond, mod_w, mod_b)
    out = None

    for layer in range(depth):
        ctx_out = layer < depth - 1
        j = layer // 2
        rows_out = all_rows if ctx_out else lat_rows
        mod3 = mods[layer].reshape(MOD_ROWS, 1, 6 * d)

        h = _norm_mod(xa, norm_mix[layer], mod3, 0, all_rows, dims)[0]
        if layer % 2 == 0:
            w_p, mu_main, mu_tail = _pack_even_weights(ev_w_in[j], rw_mu[j], lay, lw, la, mh, dqk, md)
            z = _proj(h, w_p, all_rows, F32)
            r, v, kkn, lwd, km, bb, bonus, gmul = _rwkv_prep(
                z, all_rows, lay, dims, mu_main, mu_tail, rw_w0[j], _pad_rows(rw_w_up[j], lay["lwp"]).astype(BF16),
                rw_a0[j], _pad_rows(rw_a_up[j], lay["lwp"]).astype(BF16), rw_g_up[j].astype(BF16),
                rw_k_k[j], rw_k_a[j], rw_r_k[j])
            y = _rwkv_scan(r, v, kkn, lwd, km, bb, dims, heads_per_step=min(32, rd // RWKV_HEAD))
            g_off = lay["tail"] + lay["lg"] + lw
            gates = z[:, g_off:g_off + 4 * mh].reshape(all_rows, 2, 2 * mh)
            gates_col = jnp.moveaxis(gates, 1, 0).reshape(2, all_rows // CHUNK, CHUNK, 2 * mh)
            gates_row = jnp.swapaxes(gates_col, 2, 3)
            bias = ml_gate_b[j].reshape(2, 2 * mh)
            hm = _mlstm_scan(z, lay, dims, gates_col, gates_row, bias.reshape(2, 1, 2 * mh),
                             bias.reshape(2, 2 * mh, 1), mh, dqk, dv)
            mix = _even_finish(y, bonus, gmul, rw_gn_w[j], rw_gn_b[j], hm, z, lay, ml_norm[j],
                               rows_out, dims, mh, dv)
            xa_new = _proj_residual(mix, ev_w_out[j], xa, mod3, 2, rows_out, dims)
        else:
            w_qkv = at_w_qkv[j]
            cos_t, sin_t = _rope_tables(dims, head)
            q_scale = np.float32(head ** -0.5 * np.log2(np.e))
            q = _proj_qk(h, w_qkv, 0, d, at_q_norm[j], cos_t, sin_t, q_scale, lat_rows)
            k = _proj_qk(h, w_qkv, d, kv_dim, at_k_norm[j], cos_t, sin_t, np.float32(1.0), all_rows)
            v = _proj(h, w_qkv, all_rows, BF16, col_off=d + kv_dim, n_cols=kv_dim)
            att = _attention(q, k, v, dims, head, group)
            xa_new = _proj_residual(att, at_w_o[j], xa, mod3, 2, lat_rows, dims)
            if ctx_out:
                raise NotImplementedError("context output of an attention layer")
        xa = xa_new

        h2, idx, gate_t = _norm_mod(xa, norm_ffn[layer], mod3, 3, rows_out, dims,
                                    router=(router_w, router_bias))
        gate = gate_t.T
        pos, row_token, block_expert, n_used = _dispatch_plan(idx, n_experts, moe_bm)
        xs = h2.at[row_token].get(mode="promise_in_bounds")
        ys = _moe_ffn(xs, block_expert, n_used, exp_w1, exp_w3, exp_w2, layer, moe_bm)
        f0 = ys.at[pos[:, 0]].get(mode="promise_in_bounds")
        f1 = ys.at[pos[:, 1]].get(mode="promise_in_bounds")
        last = layer == depth - 1
        xa = _ffn_residual(xa, f0, f1, gate, mod3, 5, rows_out, dims,
                           final_norm=norm_final if last else None)
        if last:
            out = xa[:lat_rows].reshape(nb, seq, d)
    return out
```

```python
import functools

import jax
import jax.numpy as jnp
import numpy as np
from jax import lax
from jax.experimental import pallas as pl
from jax.experimental.pallas import tpu as pltpu

F32 = jnp.float32
BF16 = jnp.bfloat16
HIGHEST = lax.Precision.HIGHEST

GRID_W = 64
RMS_EPS = 1e-6
GN_EPS = 64e-5
RWKV_HEAD = 64
CHUNK = 64
N_GROUPS = 4
TOP_K = 2
ROPE_THETA = 10000.0
LANES = 128
MOD_ROWS = 8
VMEM_LIMIT = 56 * 1024 * 1024


def _rup(n, m):
    return (n + m - 1) // m * m


def _tile(n, pref, quantum):
    t = min(pref, n) // quantum * quantum
    while t >= quantum:
        if n % t == 0:
            return t
        t -= quantum
    return n


def _params(*sem):
    return pltpu.CompilerParams(dimension_semantics=sem, vmem_limit_bytes=VMEM_LIMIT)


def _dot(a, b):
    return jnp.dot(a.astype(BF16), b.astype(BF16), preferred_element_type=F32)


def _dot_nt(a, b):
    return lax.dot_general(a.astype(BF16), b.astype(BF16), (((1,), (1,)), ((), ())),
                           preferred_element_type=F32)


def _dot_tn(a, b):
    return lax.dot_general(a.astype(BF16), b.astype(BF16), (((0,), (0,)), ((), ())),
                           preferred_element_type=F32)


def _dot_f32(a, b):
    return jnp.dot(a, b, preferred_element_type=F32, precision=HIGHEST)


def _pack_bf16_pairs(x):
    half = x.shape[1] // 2
    bits = lax.bitcast_convert_type(x.astype(BF16).astype(F32), jnp.uint32)
    return (bits[:, :half] >> 16) | bits[:, half:]


def _unpack_bf16_pairs(p):
    left = lax.bitcast_convert_type(p << 16, F32)
    right = lax.bitcast_convert_type(p & jnp.uint32(0xFFFF0000), F32)
    return jnp.concatenate([left, right], axis=1).astype(BF16)


def _mod_kernel(c_ref, w_ref, b_ref, o_ref):
    c = c_ref[...]
    c = c * jax.nn.sigmoid(c)
    o_ref[...] = _dot(c, w_ref[...]) + b_ref[...]


def _mod_tables(cond, mod_w, mod_b):
    d = cond.shape[1]
    depth, _, n = mod_w.shape
    tn = _tile(n, 512, LANES)
    return pl.pallas_call(
        _mod_kernel,
        grid=(depth, n // tn),
        in_specs=[pl.BlockSpec((MOD_ROWS, d), lambda l, j: (0, 0)),
                  pl.BlockSpec((None, d, tn), lambda l, j: (l, 0, j)),
                  pl.BlockSpec((None, 1, tn), lambda l, j: (l, 0, j))],
        out_specs=pl.BlockSpec((None, MOD_ROWS, tn), lambda l, j: (l, 0, j)),
        out_shape=jax.ShapeDtypeStruct((depth, MOD_ROWS, n), F32),
        compiler_params=_params("parallel", "parallel"),
        name="mod_tables",
    )(cond, mod_w, mod_b.reshape(depth, 1, n))


def _norm_mod_kernel(x_ref, g_ref, sh_ref, sc_ref, h_ref):
    x = x_ref[...]
    y = x * lax.rsqrt(jnp.mean(x * x, axis=-1, keepdims=True) + RMS_EPS) * g_ref[...]
    h_ref[...] = (y * (1.0 + sc_ref[...]) + sh_ref[...]).astype(h_ref.dtype)


def _top2_sum(a, b, c, d):
    hi1, lo1 = jnp.maximum(a, b), jnp.minimum(a, b)
    hi2, lo2 = jnp.maximum(c, d), jnp.minimum(c, d)
    return jnp.maximum(hi1, hi2) + jnp.maximum(jnp.minimum(hi1, hi2), jnp.maximum(lo1, lo2))


def _first_argmax(vals):
    best_v = vals[0]
    best_i = jnp.zeros(vals[0].shape, jnp.int32)
    for i in range(1, len(vals)):
        better = vals[i] > best_v
        best_i = jnp.where(better, i, best_i)
        best_v = jnp.where(better, vals[i], best_v)
    return best_i, best_v


def _pick(rows, index):
    out = rows[0]
    for i in range(1, len(rows)):
        out = jnp.where(index == i, rows[i], out)
    return out


def _norm_mod_router_kernel(x_ref, g_ref, sh_ref, sc_ref, rwt_ref, rb_ref, h_ref, idx_ref, gate_ref,
                            *, n_experts):
    x = x_ref[...]
    y = x * lax.rsqrt(jnp.mean(x * x, axis=-1, keepdims=True) + RMS_EPS) * g_ref[...]
    h = y * (1.0 + sc_ref[...]) + sh_ref[...]
    h_ref[...] = _pack_bf16_pairs(h)
    logits = lax.dot_general(rwt_ref[...], h, (((1,), (1,)), ((), ())), preferred_element_type=F32,
                             precision=HIGHEST)
    aff_all = jax.nn.sigmoid(logits)
    sel_all = aff_all + rb_ref[...]
    per_group = n_experts // N_GROUPS
    aff = [aff_all[e:e + 1, :] for e in range(n_experts)]
    sel = [sel_all[e:e + 1, :] for e in range(n_experts)]
    assert per_group == 4 and TOP_K == 2
    best, _ = _first_argmax([_top2_sum(*sel[g * per_group:(g + 1) * per_group]) for g in range(N_GROUPS)])
    cand = [_pick([sel[g * per_group + i] for g in range(N_GROUPS)], best) for i in range(per_group)]
    cand_aff = [_pick([aff[g * per_group + i] for g in range(N_GROUPS)], best) for i in range(per_group)]
    i1, _ = _first_argmax(cand)
    i2, _ = _first_argmax([jnp.where(i1 == i, -jnp.inf, cand[i]) for i in range(per_group)])
    g1 = _pick(cand_aff, i1)
    g2 = _pick(cand_aff, i2)
    idx_ref[0:1, :] = best * per_group + i1
    idx_ref[1:2, :] = best * per_group + i2
    gate_ref[0:1, :] = g1 / (g1 + g2)
    gate_ref[1:2, :] = g2 / (g1 + g2)


def _norm_mod(x, g, mod3, shift_chunk, n_rows, dims, router=None):
    d = x.shape[1]
    tm = _tile(dims["ctx_rows"], 256, 8)
    grp = dims["group_of"](tm)
    in_specs = [pl.BlockSpec((tm, d), lambda i: (i, 0)),
                pl.BlockSpec((1, d), lambda i: (0, 0)),
                pl.BlockSpec((None, 1, d), lambda i: (grp(i), 0, shift_chunk)),
                pl.BlockSpec((None, 1, d), lambda i: (grp(i), 0, shift_chunk + 1))]
    args = [x, g.reshape(1, d), mod3, mod3]
    out_specs = [pl.BlockSpec((tm, d), lambda i: (i, 0))]
    out_shape = [jax.ShapeDtypeStruct((n_rows, d), BF16)]
    kern = _norm_mod_kernel
    if router is not None:
        router_w, router_bias = router
        n_experts = router_w.shape[1]
        in_specs += [pl.BlockSpec((n_experts, d), lambda i: (0, 0)),
                     pl.BlockSpec((n_experts, 1), lambda i: (0, 0))]
        args += [router_w.T, router_bias.astype(F32).reshape(n_experts, 1)]
        out_specs = [pl.BlockSpec((tm, d // 2), lambda i: (i, 0))] + [pl.BlockSpec((TOP_K, tm), lambda i: (0, i))] * 2
        out_shape = [jax.ShapeDtypeStruct((n_rows, d // 2), jnp.uint32),
                     jax.ShapeDtypeStruct((TOP_K, n_rows), jnp.int32),
                     jax.ShapeDtypeStruct((TOP_K, n_rows), F32)]
        kern = functools.partial(_norm_mod_router_kernel, n_experts=n_experts)
    return pl.pallas_call(
        kern, grid=(n_rows // tm,), in_specs=in_specs, out_specs=out_specs, out_shape=out_shape,
        compiler_params=_params("parallel"), name="norm_mod",
    )(*args)


def _resident_bf16(w_ref, wb_ref):
    @pl.when(pl.program_id(1) == 0)
    def _():
        wb_ref[...] = w_ref[...].astype(BF16)
    return wb_ref[...]


def _proj_kernel(a_ref, w_ref, o_ref, wb_ref):
    w = _resident_bf16(w_ref, wb_ref)
    o_ref[...] = jnp.dot(a_ref[...], w, preferred_element_type=F32).astype(o_ref.dtype)


def _proj(a, w, n_rows, out_dtype, col_off=0, n_cols=None, tm_pref=512, tn_pref=512):
    k = a.shape[1]
    n_cols = w.shape[1] - col_off if n_cols is None else n_cols
    tm = _tile(n_rows, tm_pref, 8)
    tn = _tile(int(np.gcd(n_cols, col_off)) if col_off else n_cols, tn_pref, LANES)
    off = col_off // tn
    return pl.pallas_call(
        _proj_kernel,
        grid=(n_cols // tn, n_rows // tm),
        in_specs=[pl.BlockSpec((tm, k), lambda j, i: (i, 0)),
                  pl.BlockSpec((k, tn), lambda j, i: (0, j + off))],
        out_specs=pl.BlockSpec((tm, tn), lambda j, i: (i, j)),
        out_shape=jax.ShapeDtypeStruct((n_rows, n_cols), out_dtype),
        scratch_shapes=[pltpu.VMEM((k, tn), BF16)],
        compiler_params=_params("parallel", "arbitrary"), name="proj",
    )(a, w)


def _proj_res_kernel(a_ref, w_ref, x_ref, gt_ref, o_ref, wb_ref):
    w = _resident_bf16(w_ref, wb_ref)
    acc = jnp.dot(a_ref[...], w, preferred_element_type=F32)
    o_ref[...] = x_ref[...] + gt_ref[...] * acc


def _proj_residual(a, w, x, mod3, gate_chunk, n_rows, dims, tn_pref=512):
    k = a.shape[1]
    d = w.shape[1]
    tm = _tile(dims["ctx_rows"], 512, 8)
    tn = _tile(d, tn_pref, LANES)
    grp = dims["group_of"](tm)
    gblk = gate_chunk * (d // tn)
    return pl.pallas_call(
        _proj_res_kernel,
        grid=(d // tn, n_rows // tm),
        in_specs=[pl.BlockSpec((tm, k), lambda j, i: (i, 0)),
                  pl.BlockSpec((k, tn), lambda j, i: (0, j)),
                  pl.BlockSpec((tm, tn), lambda j, i: (i, j)),
                  pl.BlockSpec((None, 1, tn), lambda j, i: (grp(i), 0, gblk + j))],
        out_specs=pl.BlockSpec((tm, tn), lambda j, i: (i, j)),
        out_shape=jax.ShapeDtypeStruct((n_rows, d), F32),
        scratch_shapes=[pltpu.VMEM((k, tn), BF16)],
        compiler_params=_params("parallel", "arbitrary"), name="proj_residual",
    )(a, w, x, mod3)


def _proj_qk_kernel(a_ref, w_ref, nw_ref, cos_ref, sin_ref, o_ref, wb_ref, *, head, scale):
    w = _resident_bf16(w_ref, wb_ref)
    nw = nw_ref[...] * scale
    tm = a_ref.shape[0]
    rows = _tile(tm, 128, 8)
    for r in range(0, tm, rows):
        acc = jnp.dot(a_ref[r:r + rows, :], w, preferred_element_type=F32)
        cs = cos_ref[r:r + rows, :]
        sn = sin_ref[r:r + rows, :]
        for s in range(acc.shape[1] // head):
            x = acc[:, s * head:(s + 1) * head]
            xn = x * lax.rsqrt(jnp.mean(x * x, axis=-1, keepdims=True) + RMS_EPS) * nw
            xr = xn * cs + pltpu.roll(xn, head // 2, 1) * sn
            o_ref[r:r + rows, s * head:(s + 1) * head] = xr.astype(o_ref.dtype)


def _proj_qk(a, w, col_off, n_cols, norm_w, cos_t, sin_t, scale, n_rows):
    k = a.shape[1]
    head = norm_w.shape[0]
    tm = _tile(n_rows, 512, 8)
    tn = _tile(int(np.gcd(n_cols, col_off)) if col_off else n_cols, 512, head)
    off = col_off // tn
    return pl.pallas_call(
        functools.partial(_proj_qk_kernel, head=head, scale=scale),
        grid=(n_cols // tn, n_rows // tm),
        in_specs=[pl.BlockSpec((tm, k), lambda j, i: (i, 0)),
                  pl.BlockSpec((k, tn), lambda j, i: (0, j + off)),
                  pl.BlockSpec((1, head), lambda j, i: (0, 0)),
                  pl.BlockSpec((tm, head), lambda j, i: (i, 0)),
                  pl.BlockSpec((tm, head), lambda j, i: (i, 0))],
        out_specs=pl.BlockSpec((tm, tn), lambda j, i: (i, j)),
        out_shape=jax.ShapeDtypeStruct((n_rows, n_cols), BF16),
        scratch_shapes=[pltpu.VMEM((k, tn), BF16)],
        compiler_params=_params("parallel", "arbitrary"), name="proj_qk",
    )(a, w, norm_w.reshape(1, head), cos_t, sin_t)


def _seg_sum(x, seg):
    n = x.shape[1]
    lane_blk = LANES if n % LANES == 0 else n
    r = lax.broadcasted_iota(jnp.int32, (lane_blk, lane_blk), 0) // seg
    c = lax.broadcasted_iota(jnp.int32, (lane_blk, lane_blk), 1) // seg
    ones_bd = (r == c).astype(BF16)
    hi = x.astype(BF16)
    rest = x - hi.astype(F32)
    mid = rest.astype(BF16)
    lo = (rest - mid.astype(F32)).astype(BF16)
    parts = []
    for s in range(0, n, lane_blk):
        cols = slice(s, s + lane_blk)
        parts.append(jnp.dot(hi[:, cols], ones_bd, preferred_element_type=F32)
                     + jnp.dot(mid[:, cols], ones_bd, preferred_element_type=F32)
                     + jnp.dot(lo[:, cols], ones_bd, preferred_element_type=F32))
    return parts[0] if len(parts) == 1 else jnp.concatenate(parts, axis=1)


def _rwkv_prep_kernel(z_ref, zp_ref, zn_ref, t_ref, tp_ref, tn_ref, mu_ref, mut_ref, w0_ref, wup_ref,
                      a0_ref, aup_ref, gup_ref, kk_ref, ka_ref, rk_ref,
                      r_out, v_out, kkn_out, lw_out, km_out, bb_out, bon_out, gm_out,
                      *, rd, lg, lwp, seq, ctx, n_lat_rows):
    tm = z_ref.shape[0]
    row0 = pl.program_id(0) * tm
    in_lat = row0 < n_lat_rows
    seq_len = jnp.where(in_lat, seq, ctx)
    pos0 = jnp.where(in_lat, row0 % seq, (row0 - n_lat_rows) % ctx)
    has_prev = (pos0 != 0).astype(F32)
    has_next = (pos0 + tm != seq_len).astype(F32)

    def token_shift(cur_ref, prev_ref, next_ref, mix_ref):
        cur = cur_ref[...]
        rows = lax.broadcasted_iota(jnp.int32, cur.shape, 0)
        prev_row = prev_ref[7:8, :] * has_prev
        next_row = next_ref[0:1, :] * has_next
        before = jnp.where(rows == 0, prev_row, pltpu.roll(cur, 1, 0))
        after = jnp.where(rows == tm - 1, next_row, pltpu.roll(cur, tm - 1, 0))
        return cur + mix_ref[...] * (0.5 * (before + after) - cur)

    zs = token_shift(z_ref, zp_ref, zn_ref, mu_ref)
    ts = token_shift(t_ref, tp_ref, tn_ref, mut_ref)
    r = zs[:, 0:rd]
    k = zs[:, rd:2 * rd]
    v = zs[:, 2 * rd:3 * rd]
    g_down = ts[:, 0:lg]
    w_down = ts[:, lg:lg + lwp]
    a_down = ts[:, lg + lwp:lg + 2 * lwp]

    kk = k * kk_ref[...]
    kk = kk * lax.rsqrt(jnp.maximum(_seg_sum(kk * kk, RWKV_HEAD), 1e-24))
    r_out[...] = r
    v_out[...] = v
    kkn_out[...] = kk
    tw = jnp.tanh(w_down)
    bonus = jnp.zeros_like(r)
    for d in range(2):
        lw_out[d] = -np.float32(np.exp(-0.5)) * jax.nn.sigmoid(w0_ref[d] + _dot(tw, wup_ref[d]))
        a = jax.nn.sigmoid(a0_ref[d] + _dot(a_down, aup_ref[d]))
        k_mod = k * (1.0 + (a - 1.0) * ka_ref[...])
        km_out[d] = k_mod
        bb_out[d] = kk * a
        bonus = bonus + _seg_sum(r * k_mod * rk_ref[...], RWKV_HEAD) * v
    bon_out[...] = bonus
    gm_out[...] = _dot(jax.nn.sigmoid(g_down), gup_ref[...])


def _rwkv_prep(z_rkv, z, n_rows, lay, dims, mu_main, mu_tail, w0, w_up_p, a0, a_up_p, g_up, k_k, k_a, r_k):
    rd, lg, lwp = lay["rd"], lay["lg"], lay["lwp"]
    mw = 3 * rd
    tw = lg + 2 * lwp
    tblk = lay["tail"] // tw
    tm = _tile(dims["ctx"], 64, 8)
    nb8 = n_rows // 8
    full = lambda shape: pl.BlockSpec(shape, lambda i: (0,) * len(shape))
    row_spec = pl.BlockSpec((tm, rd), lambda i: (i, 0))
    dir_spec = pl.BlockSpec((2, tm, rd), lambda i: (0, i, 0))
    sds = jax.ShapeDtypeStruct
    kern = functools.partial(_rwkv_prep_kernel, rd=rd, lg=lg, lwp=lwp, seq=dims["seq"], ctx=dims["ctx"],
                             n_lat_rows=dims["lat_rows"])
    return pl.pallas_call(
        kern,
        grid=(n_rows // tm,),
        in_specs=[pl.BlockSpec((tm, mw), lambda i: (i, 0)),
                  pl.BlockSpec((8, mw), lambda i: (jnp.maximum(i * (tm // 8) - 1, 0), 0)),
                  pl.BlockSpec((8, mw), lambda i: (jnp.minimum((i + 1) * (tm // 8), nb8 - 1), 0)),
                  pl.BlockSpec((tm, tw), lambda i: (i, tblk)),
                  pl.BlockSpec((8, tw), lambda i: (jnp.maximum(i * (tm // 8) - 1, 0), tblk)),
                  pl.BlockSpec((8, tw), lambda i: (jnp.minimum((i + 1) * (tm // 8), nb8 - 1), tblk)),
                  full((1, mw)), full((1, tw)), full((2, 1, rd)), full((2, lwp, rd)), full((2, 1, rd)), full((2, lwp, rd)),
                  full((lg, rd)), full((1, rd)), full((1, rd)), full((1, rd))],
        out_specs=[row_spec, row_spec, row_spec, dir_spec, dir_spec, dir_spec, row_spec, row_spec],
        out_shape=[sds((n_rows, rd), F32)] * 3 + [sds((2, n_rows, rd), F32)] * 3 + [sds((n_rows, rd), F32)] * 2,
        compiler_params=_params("parallel"), name="rwkv_prep",
    )(z_rkv, z_rkv, z_rkv, z, z, z, mu_main, mu_tail, w0.reshape(2, 1, rd), w_up_p, a0.reshape(2, 1, rd), a_up_p, g_up,
      k_k.reshape(1, rd), k_a.reshape(1, rd), r_k.reshape(1, rd))


def _rwkv_scan_kernel(r_ref, v_ref, kk_ref, lw_ref, km_ref, bb_ref, y_ref, s_ref, *, heads):
    d = pl.program_id(0)
    j = pl.program_id(3)

    @pl.when(j == 0)
    def _():
        s_ref[...] = jnp.zeros_like(s_ref)

    n = RWKV_HEAD
    chunk = lw_ref.shape[0]
    t_idx = lax.broadcasted_iota(jnp.int32, (chunk, chunk), 0)
    s_idx = lax.broadcasted_iota(jnp.int32, (chunk, chunk), 1)
    lead = jnp.where(d == 0, t_idx - s_idx, s_idx - t_idx)
    incl = lead >= 0
    assert chunk & (chunk - 1) == 0

    lw = lw_ref[...]
    cum = _dot_f32(incl.astype(F32), lw)
    tot = jnp.sum(lw, axis=0, keepdims=True)
    e_in = jnp.exp(cum)
    e_neg = jnp.exp(-cum)
    e_last = jnp.exp(tot - cum)
    kkn = kk_ref[...]
    km = km_ref[...]
    bb = bb_ref[...]
    r_t = r_ref[...] * e_in
    a_t = -kkn * jnp.exp(cum - lw)
    b_t = bb * e_neg
    k_t = km * e_neg
    b_l = bb * e_last
    k_l = km * e_last
    w_l = jnp.exp(tot)
    v_all = v_ref[...]

    hs = range(heads)
    sl = [slice(h * n, (h + 1) * n) for h in hs]
    t2 = lax.broadcasted_iota(jnp.int32, (2 * chunk, 2 * chunk), 0)
    s2 = lax.broadcasted_iota(jnp.int32, (2 * chunk, 2 * chunk), 1)
    tt = jnp.where(t2 >= chunk, t2 - chunk, t2)
    ss = jnp.where(s2 >= chunk, s2 - chunk, s2)
    keep = jnp.where(d == 0, tt - ss, ss - tt) >= jnp.where(t2 < chunk, 1, 0)
    right = lax.broadcasted_iota(jnp.int32, (chunk, 2 * chunk), 1) >= chunk
    eye_right = (lax.broadcasted_iota(jnp.int32, (chunk, 2 * chunk), 1)
                 == lax.broadcasted_iota(jnp.int32, (chunk, 2 * chunk), 0) + chunk).astype(F32)
    zeros_v = jnp.zeros((chunk, n), F32)
    stack = lambda top, bottom: jnp.concatenate([top, bottom], axis=0)

    s0 = [s_ref[h] for h in hs]
    v = [v_all[:, sl[h]] for h in hs]
    pair = [jnp.where(keep, _dot_nt(stack(a_t[:, sl[h]], r_t[:, sl[h]]), stack(b_t[:, sl[h]], k_t[:, sl[h]])), 0.0)
            for h in hs]
    a_side = [pair[h][:chunk] for h in hs]
    r_side = [pair[h][chunk:] for h in hs]
    x = [_dot_nt(a_t[:, sl[h]], s0[h]) + _dot(a_side[h], stack(zeros_v, v[h])) for h in hs]
    y0 = [_dot_nt(r_t[:, sl[h]], s0[h]) for h in hs]
    q = [jnp.where(right, eye_right, a_side[h]) for h in hs]
    for _ in range(chunk.bit_length() - 1):
        q = [_dot(q[h][:, :chunk], q[h]) + jnp.where(right, q[h], 0.0) for h in hs]
    uv = [stack(_dot(q[h][:, chunk:], x[h]), v[h]) for h in hs]
    y_ref[...] = jnp.concatenate([y0[h] + _dot(r_side[h], uv[h]) for h in hs], axis=1)
    for h in hs:
        s_ref[h] = s0[h] * w_l[:, sl[h]] + _dot_tn(uv[h], stack(b_l[:, sl[h]], k_l[:, sl[h]]))


def _chunk_index(d, b, j, dims):
    nc_ctx, nc_lat, nb = dims["ctx"] // CHUNK, dims["seq"] // CHUNK, dims["batch"]
    jc = jnp.where(d == 0, j, nc_ctx - 1 - j)
    jl = jnp.where(d == 0, j - nc_ctx, nc_lat - 1 - (j - nc_ctx))
    return jnp.where(j < nc_ctx, nb * nc_lat + b * nc_ctx + jc, b * nc_lat + jl)


def _rwkv_scan(r, v, kkn, lw, km, bb, dims, heads_per_step):
    n_rows, rd = r.shape
    gw = heads_per_step * RWKV_HEAD
    n_chunks = (dims["ctx"] + dims["seq"]) // CHUNK
    cidx = lambda d, b, g, j: _chunk_index(d, b, j, dims)
    row_spec = pl.BlockSpec((CHUNK, gw), lambda d, b, g, j: (cidx(d, b, g, j), g))
    dir_spec = pl.BlockSpec((None, CHUNK, gw), lambda d, b, g, j: (d, cidx(d, b, g, j), g))
    return pl.pallas_call(
        functools.partial(_rwkv_scan_kernel, heads=heads_per_step),
        grid=(2, dims["batch"], rd // gw, n_chunks),
        in_specs=[row_spec, row_spec, row_spec, dir_spec, dir_spec, dir_spec],
        out_specs=dir_spec,
        out_shape=jax.ShapeDtypeStruct((2, n_rows, rd), F32),
        scratch_shapes=[pltpu.VMEM((heads_per_step, RWKV_HEAD, RWKV_HEAD), F32)],
        compiler_params=_params("parallel", "parallel", "parallel", "arbitrary"), name="rwkv_scan",
    )(r, v, kkn, lw, km, bb)


def _mlstm_kernel(q_ref, k_ref, v_ref, gc_ref, gr_ref, bc_ref, br_ref, h_ref, c_ref, n_ref, m_ref,
                  *, heads, dqk, dv):
    d = pl.program_id(0)
    j = pl.program_id(2)

    @pl.when(j == 0)
    def _():
        c_ref[...] = jnp.zeros_like(c_ref)
        n_ref[...] = jnp.zeros_like(n_ref)
        m_ref[...] = jnp.zeros_like(m_ref)

    chunk = q_ref.shape[0]
    t_idx = lax.broadcasted_iota(jnp.int32, (chunk, chunk), 0)
    s_idx = lax.broadcasted_iota(jnp.int32, (chunk, chunk), 1)
    lead = jnp.where(d == 0, t_idx - s_idx, s_idx - t_idx)
    incl = lead >= 0
    tri = incl.astype(F32)
    tri_t = (lead <= 0).astype(F32)

    gcol = gc_ref[...] + bc_ref[...]
    grow = gr_ref[...] + br_ref[...]
    i_col = gcol[:, :heads]
    f_col = jax.nn.log_sigmoid(gcol[:, heads:])
    i_row = grow[:heads, :]
    f_row = jax.nn.log_sigmoid(grow[heads:, :])
    b_col = _dot_f32(tri, f_col)
    b_row = _dot_f32(f_row, tri_t)
    b_last = jnp.sum(f_col, axis=0, keepdims=True)
    scale = np.float32(dqk ** -0.5)

    hs = range(heads)
    q = [q_ref[:, h * dqk:(h + 1) * dqk] * scale for h in hs]
    k = [k_ref[:, h * dqk:(h + 1) * dqk] for h in hs]
    v = [v_ref[:, h * dv:(h + 1) * dv] for h in hs]
    c_st = [c_ref[h] for h in hs]
    n_st = [n_ref[h] for h in hs]
    m_st = [m_ref[h][:, 0:1] for h in hs]
    bc = [b_col[:, h:h + 1] for h in hs]
    qk = [_dot_nt(q[h], k[h]) for h in hs]
    qc = [_dot(q[h], c_st[h]) for h in hs]
    dmat = [jnp.where(incl, bc[h] + (i_row[h:h + 1, :] - b_row[h:h + 1, :]), -jnp.inf) for h in hs]
    inter = [bc[h] + m_st[h] for h in hs]
    m_t = [jnp.maximum(inter[h], jnp.max(dmat[h], axis=-1, keepdims=True)) for h in hs]
    w_inter = [jnp.exp(inter[h] - m_t[h]) for h in hs]
    s = [qk[h] * jnp.exp(dmat[h] - m_t[h]) for h in hs]
    num = [w_inter[h] * qc[h] + _dot(s[h], v[h]) for h in hs]
    den = [w_inter[h] * jnp.sum(q[h] * n_st[h], axis=-1, keepdims=True) + jnp.sum(s[h], axis=-1, keepdims=True)
           for h in hs]
    h_ref[...] = jnp.concatenate([num[h] / jnp.maximum(jnp.abs(den[h]), jnp.exp(-m_t[h])) for h in hs], axis=1)
    bl = [b_last[:, h:h + 1] for h in hs]
    g = [bl[h] - bc[h] + i_col[:, h:h + 1] for h in hs]
    m_new = [jnp.maximum(bl[h] + m_st[h], jnp.max(g[h], axis=0, keepdims=True)) for h in hs]
    decay = [jnp.exp(bl[h] + m_st[h] - m_new[h]) for h in hs]
    kw = [k[h] * jnp.exp(g[h] - m_new[h]) for h in hs]
    for h in hs:
        c_ref[h] = decay[h] * c_st[h] + _dot_tn(kw[h], v[h])
        n_ref[h] = decay[h] * n_st[h] + jnp.sum(kw[h], axis=0, keepdims=True)
        m_ref[h] = jnp.broadcast_to(m_new[h], m_ref.shape[1:])


def _mlstm_scan(z, lay, dims, gates_col, gates_row, bias_col, bias_row, heads, dqk, dv):
    n_rows = z.shape[0]
    n_chunks = (dims["ctx"] + dims["seq"]) // CHUNK
    md = heads * dv
    qw = heads * dqk
    cidx = lambda d, b, j: _chunk_index(d, b, j, dims)
    q_blk, k_blk, v_blk = lay["mq"] // qw, lay["mk"] // qw, lay["mv"] // md
    return pl.pallas_call(
        functools.partial(_mlstm_kernel, heads=heads, dqk=dqk, dv=dv),
        grid=(2, dims["batch"], n_chunks),
        in_specs=[pl.BlockSpec((CHUNK, qw), lambda d, b, j: (cidx(d, b, j), q_blk)),
                  pl.BlockSpec((CHUNK, qw), lambda d, b, j: (cidx(d, b, j), k_blk)),
                  pl.BlockSpec((CHUNK, md), lambda d, b, j: (cidx(d, b, j), v_blk)),
                  pl.BlockSpec((None, None, CHUNK, 2 * heads), lambda d, b, j: (d, cidx(d, b, j), 0, 0)),
                  pl.BlockSpec((None, None, 2 * heads, CHUNK), lambda d, b, j: (d, cidx(d, b, j), 0, 0)),
                  pl.BlockSpec((None, 1, 2 * heads), lambda d, b, j: (d, 0, 0)),
                  pl.BlockSpec((None, 2 * heads, 1), lambda d, b, j: (d, 0, 0))],
        out_specs=pl.BlockSpec((None, CHUNK, md), lambda d, b, j: (d, cidx(d, b, j), 0)),
        out_shape=jax.ShapeDtypeStruct((2, n_rows, md), F32),
        scratch_shapes=[pltpu.VMEM((heads, dqk, dv), F32), pltpu.VMEM((heads, 1, dqk), F32),
                        pltpu.VMEM((heads, 1, LANES), F32)],
        compiler_params=_params("parallel", "parallel", "arbitrary"), name="mlstm_scan",
    )(z, z, z, gates_col, gates_row, bias_col, bias_row)


def _even_finish_kernel(y_ref, bon_ref, gm_ref, gnw_ref, gnb_ref, h_ref, o_ref, nw_ref, out_ref,
                        *, rd, heads, dv):
    y = y_ref[0] + y_ref[1]
    inv_n = np.float32(1.0 / RWKV_HEAD)
    mean = _seg_sum(y, RWKV_HEAD) * inv_n
    yc = y - mean
    var = _seg_sum(yc * yc, RWKV_HEAD) * inv_n
    yn = yc * lax.rsqrt(var + GN_EPS) * gnw_ref[...] + gnb_ref[...] + bon_ref[...]
    out_ref[:, 0:rd] = (yn * gm_ref[...]).astype(out_ref.dtype)
    hm = h_ref[0] + h_ref[1]
    for h in range(heads):
        sl = slice(h * dv, (h + 1) * dv)
        x = hm[:, sl]
        xn = x * lax.rsqrt(jnp.mean(x * x, axis=-1, keepdims=True) + RMS_EPS) * nw_ref[:, sl]
        out_ref[:, rd + h * dv:rd + (h + 1) * dv] = (jax.nn.sigmoid(o_ref[:, sl]) * xn).astype(out_ref.dtype)


def _even_finish(y, bonus, gmul, gn_w, gn_b, hm, z, lay, ml_norm, n_rows, dims, heads, dv):
    rd = y.shape[2]
    md = hm.shape[2]
    tm = _tile(dims["ctx_rows"], 256, 8)
    o_blk = lay["mo"] // md
    return pl.pallas_call(
        functools.partial(_even_finish_kernel, rd=rd, heads=heads, dv=dv),
        grid=(n_rows // tm,),
        in_specs=[pl.BlockSpec((2, tm, rd), lambda i: (0, i, 0)),
                  pl.BlockSpec((tm, rd), lambda i: (i, 0)),
                  pl.BlockSpec((tm, rd), lambda i: (i, 0)),
                  pl.BlockSpec((1, rd), lambda i: (0, 0)),
                  pl.BlockSpec((1, rd), lambda i: (0, 0)),
                  pl.BlockSpec((2, tm, md), lambda i: (0, i, 0)),
                  pl.BlockSpec((tm, md), lambda i: (i, o_blk)),
                  pl.BlockSpec((1, md), lambda i: (0, 0))],
        out_specs=pl.BlockSpec((tm, rd + md), lambda i: (i, 0)),
        out_shape=jax.ShapeDtypeStruct((n_rows, rd + md), BF16),
        compiler_params=_params("parallel"), name="even_finish",
    )(y, bonus, gmul, gn_w.reshape(1, rd), gn_b.reshape(1, rd), hm, z, ml_norm.reshape(1, md))


def _attn_kernel(q_ref, kl_ref, kc_ref, vl_ref, vc_ref, o_ref, *, group, head):
    tq = q_ref.shape[0]
    q = jnp.concatenate([q_ref[:, g * head:(g + 1) * head] for g in range(group)], axis=0)
    ck = _tile(kl_ref.shape[0], 256, 8)
    keys = [kc_ref[...]] + [kl_ref[c * ck:(c + 1) * ck, :] for c in range(kl_ref.shape[0] // ck)]
    vals = [vc_ref[...]] + [vl_ref[c * ck:(c + 1) * ck, :] for c in range(kl_ref.shape[0] // ck)]
    m = acc = None
    for kc, vc in zip(keys, vals):
        s = _dot_nt(q, kc)
        row_max = jnp.max(s, axis=-1, keepdims=True)
        v_aug = jnp.concatenate([vc, jnp.ones_like(vc)], axis=1)
        if m is None:
            m = row_max
            acc = _dot(jnp.exp2(s - m), v_aug)
        else:
            m_new = jnp.maximum(m, row_max)
            acc = acc * jnp.exp2(m - m_new) + _dot(jnp.exp2(s - m_new), v_aug)
            m = m_new
    o = acc[:, :head] / acc[:, head:head + 1]
    for g in range(group):
        o_ref[:, g * head:(g + 1) * head] = o[g * tq:(g + 1) * tq].astype(o_ref.dtype)


def _attention(q, k, v, dims, head, group):
    seq, ctx, nb = dims["seq"], dims["ctx"], dims["batch"]
    kvh = k.shape[1] // head
    tq = _tile(seq, 256, 8)
    nq = seq // tq
    gw = group * head
    return pl.pallas_call(
        functools.partial(_attn_kernel, group=group, head=head),
        grid=(nb, kvh, nq),
        in_specs=[pl.BlockSpec((tq, gw), lambda b, h, i: (b * nq + i, h)),
                  pl.BlockSpec((seq, head), lambda b, h, i: (b, h)),
                  pl.BlockSpec((ctx, head), lambda b, h, i: (nb * (seq // ctx) + b, h)),
                  pl.BlockSpec((seq, head), lambda b, h, i: (b, h)),
                  pl.BlockSpec((ctx, head), lambda b, h, i: (nb * (seq // ctx) + b, h))],
        out_specs=pl.BlockSpec((tq, gw), lambda b, h, i: (b * nq + i, h)),
        out_shape=jax.ShapeDtypeStruct(q.shape, BF16),
        compiler_params=_params("parallel", "parallel", "parallel"), name="attention",
    )(q, k, k, v, v)


def _new_expert(i, be_ref):
    return jnp.logical_or(i == 0, be_ref[i] != be_ref[jnp.maximum(i - 1, 0)])


def _moe_up_kernel(be_ref, nu_ref, x_ref, w1_ref, w3_ref, mid_ref, w1_bf, w3_bf):
    i = pl.program_id(1)

    @pl.when(_new_expert(i, be_ref))
    def _():
        w1_bf[...] = w1_ref[...].astype(BF16)
        w3_bf[...] = w3_ref[...].astype(BF16)

    @pl.when(i < nu_ref[0])
    def _():
        x = _unpack_bf16_pairs(x_ref[...])
        a = jnp.dot(x, w1_bf[...], preferred_element_type=F32)
        b = jnp.dot(x, w3_bf[...], preferred_element_type=F32)
        mid_ref[...] = (a * jax.nn.sigmoid(a) * b).astype(mid_ref.dtype)

    @pl.when(i >= nu_ref[0])
    def _():
        mid_ref[...] = jnp.zeros_like(mid_ref)


def _moe_down_kernel(be_ref, nu_ref, mid_ref, w2_ref, o_ref, w2_bf):
    i = pl.program_id(1)

    @pl.when(_new_expert(i, be_ref))
    def _():
        w2_bf[...] = w2_ref[...].astype(BF16)

    @pl.when(i < nu_ref[0])
    def _():
        o_ref[...] = jnp.dot(mid_ref[...], w2_bf[...], preferred_element_type=F32)

    @pl.when(i >= nu_ref[0])
    def _():
        o_ref[...] = jnp.zeros_like(o_ref)


def _moe_ffn(xs, block_expert, n_used, w1, w3, w2, layer, bm):
    n_rows = xs.shape[0]
    d, de = w1.shape[2], w1.shape[3]
    tde = _tile(de, 512, LANES)
    tn = _tile(d, 4096, LANES)
    nblk = n_rows // bm
    mid = pl.pallas_call(
        _moe_up_kernel,
        grid_spec=pltpu.PrefetchScalarGridSpec(
            num_scalar_prefetch=2, grid=(de // tde, nblk),
            in_specs=[pl.BlockSpec((bm, d // 2), lambda k, i, be, nu: (i, 0)),
                      pl.BlockSpec((None, None, d, tde), lambda k, i, be, nu: (layer, be[i], 0, k)),
                      pl.BlockSpec((None, None, d, tde), lambda k, i, be, nu: (layer, be[i], 0, k))],
            out_specs=pl.BlockSpec((bm, tde), lambda k, i, be, nu: (i, k)),
            scratch_shapes=[pltpu.VMEM((d, tde), BF16), pltpu.VMEM((d, tde), BF16)]),
        out_shape=jax.ShapeDtypeStruct((n_rows, de), BF16),
        compiler_params=_params("arbitrary", "arbitrary"), name="moe_up",
    )(block_expert, n_used, xs, w1, w3)
    return pl.pallas_call(
        _moe_down_kernel,
        grid_spec=pltpu.PrefetchScalarGridSpec(
            num_scalar_prefetch=2, grid=(d // tn, nblk),
            in_specs=[pl.BlockSpec((bm, de), lambda n, i, be, nu: (i, 0)),
                      pl.BlockSpec((None, None, de, tn), lambda n, i, be, nu: (layer, be[i], 0, n))],
            out_specs=pl.BlockSpec((bm, tn), lambda n, i, be, nu: (i, n)),
            scratch_shapes=[pltpu.VMEM((de, tn), BF16)]),
        out_shape=jax.ShapeDtypeStruct((n_rows, d), F32),
        compiler_params=_params("arbitrary", "arbitrary"), name="moe_down",
    )(block_expert, n_used, mid, w2)


def _dispatch_plan(idx, n_experts, bm):
    n_tok = idx.shape[1]
    n_assign = n_tok * TOP_K
    expert = idx.T.reshape(-1)
    onehot = (expert[:, None] == jnp.arange(n_experts)[None, :]).astype(jnp.int32)
    rank = jnp.take_along_axis(jnp.cumsum(onehot, axis=0) - onehot, expert[:, None], 1)[:, 0]
    counts = jnp.sum(onehot, axis=0)
    padded = (counts + bm - 1) // bm * bm
    pad_end = jnp.cumsum(padded)
    pos = (pad_end - padded)[expert] + rank
    n_blocks = -(-n_assign // bm) + n_experts
    token = jnp.repeat(jnp.arange(n_tok, dtype=jnp.int32), TOP_K)
    row_token = jnp.zeros((n_blocks * bm,), jnp.int32).at[pos].set(token)
    block_start = jnp.arange(n_blocks, dtype=jnp.int32) * bm
    block_expert = jnp.minimum(jnp.sum((pad_end[None, :] <= block_start[:, None]).astype(jnp.int32), axis=1),
                               n_experts - 1)
    n_used = (pad_end[-1] // bm).astype(jnp.int32).reshape(1)
    block_expert = jnp.where(jnp.arange(n_blocks) < n_used[0], block_expert,
                             block_expert[jnp.maximum(n_used[0] - 1, 0)])
    return pos.reshape(n_tok, TOP_K), row_token, block_expert, n_used


def _ffn_res_kernel(x_ref, f0_ref, f1_ref, g_ref, gt_ref, o_ref):
    g = g_ref[...]
    f = g[:, 0:1] * f0_ref[...] + g[:, 1:2] * f1_ref[...]
    o_ref[...] = x_ref[...] + gt_ref[...] * f


def _ffn_res_norm_kernel(x_ref, f0_ref, f1_ref, g_ref, gt_ref, nw_ref, o_ref):
    g = g_ref[...]
    f = g[:, 0:1] * f0_ref[...] + g[:, 1:2] * f1_ref[...]
    x = x_ref[...] + gt_ref[...] * f
    o_ref[...] = x * lax.rsqrt(jnp.mean(x * x, axis=-1, keepdims=True) + RMS_EPS) * nw_ref[...]


def _ffn_residual(x, f0, f1, gate, mod3, gate_chunk, n_rows, dims, final_norm=None):
    d = x.shape[1]
    tm = _tile(dims["ctx_rows"], 256, 8)
    grp = dims["group_of"](tm)
    row = pl.BlockSpec((tm, d), lambda i: (i, 0))
    in_specs = [row, row, row, pl.BlockSpec((tm, TOP_K), lambda i: (i, 0)),
                pl.BlockSpec((None, 1, d), lambda i: (grp(i), 0, gate_chunk))]
    args = [x, f0, f1, gate, mod3]
    kern = _ffn_res_kernel
    if final_norm is not None:
        in_specs.append(pl.BlockSpec((1, d), lambda i: (0, 0)))
        args.append(final_norm.reshape(1, d))
        kern = _ffn_res_norm_kernel
    return pl.pallas_call(
        kern, grid=(n_rows // tm,), in_specs=in_specs, out_specs=row,
        out_shape=jax.ShapeDtypeStruct((n_rows, d), F32),
        compiler_params=_params("parallel"), name="ffn_residual",
    )(*args)


def _even_layout(rd, lw, la, lg, mh, dqk, md):
    lwp = _rup(lw + 4 * mh, LANES)
    assert _rup(la, LANES) == lwp
    lay = {"rd": rd, "lg": lg, "lwp": lwp}
    off = 0
    for name, width in (("mq", mh * dqk), ("mk", mh * dqk), ("mv", md), ("mo", md), ("tail", lg + 2 * lwp)):
        assert off % width == 0
        lay[name] = off
        off += width
    lay["width"] = off
    return lay


def _pad_cols(seg, width):
    return jnp.pad(seg, ((0, 0), (0, width - seg.shape[1])))


def _pack_even_weights(w_in, mu, lay, lw, la, mh, dqk, md):
    rd, lg, lwp = lay["rd"], lay["lg"], lay["lwp"]
    rww = 3 * rd + lw + la + lg
    o = 3 * rd
    w_down, a_down, g_down = w_in[:, o:o + lw], w_in[:, o + lw:o + lw + la], w_in[:, o + lw + la:rww]
    m_end = rww + 2 * mh * dqk + 2 * md
    w_p = jnp.concatenate([w_in[:, rww:m_end], g_down,
                           _pad_cols(jnp.concatenate([w_down, w_in[:, m_end:]], 1), lwp),
                           _pad_cols(a_down, lwp)], axis=1).astype(BF16)
    mu2 = mu.reshape(1, -1)
    mu_tail = jnp.concatenate([mu2[:, o + lw + la:rww], _pad_cols(mu2[:, o:o + lw], lwp),
                               _pad_cols(mu2[:, o + lw:o + lw + la], lwp)], axis=1)
    return w_p, mu2[:, :3 * rd], mu_tail


def _pad_rows(w, rows):
    return jnp.pad(w, ((0, 0), (0, rows - w.shape[1]), (0, 0)))


def _rope_tables(dims, head):
    seq, nb = dims["seq"], dims["batch"]
    pairs = head // 4
    rows = seq // GRID_W
    row = np.repeat(np.arange(rows), GRID_W).astype(np.float32)
    col = np.tile(np.arange(GRID_W), rows).astype(np.float32)
    inv = (np.float32(ROPE_THETA) ** (-np.arange(pairs, dtype=np.float32) / np.float32(pairs))).astype(np.float32)
    ang = np.concatenate([row[:, None] * inv, col[:, None] * inv], -1)
    cos, sin = np.cos(ang).astype(np.float32), np.sin(ang).astype(np.float32)
    cos_t = np.tile(np.concatenate([cos, cos], -1), (nb, 1))
    sin_t = np.tile(np.concatenate([-sin, sin], -1), (nb, 1))
    n_ctx = dims["ctx_rows"]
    return (jnp.asarray(np.concatenate([cos_t, np.ones((n_ctx, head), np.float32)], 0)),
            jnp.asarray(np.concatenate([sin_t, np.zeros((n_ctx, head), np.float32)], 0)))


def kernel(x, c, ctx, c_ctx, mod_w, mod_b, norm_mix, norm_ffn, norm_final, router_w, router_bias,
           exp_w1, exp_w3, exp_w2, ev_w_in, ev_w_out, rw_mu, rw_w0, rw_w_up, rw_a0, rw_a_up, rw_g_up,
           rw_k_k, rw_k_a, rw_r_k, rw_gn_w, rw_gn_b, ml_gate_b, ml_norm, at_w_qkv, at_q_norm, at_k_norm,
           at_w_o):
    nb, seq, d = x.shape
    n_ctx = ctx.shape[1]
    depth = mod_w.shape[0]
    lat_rows, ctx_rows = nb * seq, nb * n_ctx
    all_rows = lat_rows + ctx_rows
    assert nb + 1 <= MOD_ROWS and seq % n_ctx == 0 and n_ctx % CHUNK == 0
    dims = {"batch": nb, "seq": seq, "ctx": n_ctx, "lat_rows": lat_rows, "ctx_rows": ctx_rows,
            "group_of": lambda tm: (lambda i: jnp.minimum(i * tm // seq, nb))}

    n_experts = router_w.shape[1]
    rd = rw_w0.shape[-1]
    lw, la, lg = rw_w_up.shape[2], rw_a_up.shape[2], rw_g_up.shape[1]
    mh = ml_gate_b.shape[-1]
    md = ml_norm.shape[-1]
    dv = md // mh
    dqk = dv // 2
    head = at_q_norm.shape[-1]
    kv_dim = (at_w_qkv.shape[-1] - d) // 2
    group = (d // head) // (kv_dim // head)
    moe_bm = 256

    xa = jnp.concatenate([x.reshape(lat_rows, d), ctx.reshape(ctx_rows, d)], axis=0)
    cond = jnp.zeros((MOD_ROWS, d), F32).at[:nb].set(c).at[nb].set(c_ctx)
    lay = _even_layout(rd, lw, la, lg, mh, dqk, md)
    mods = _mod_tables(cond, mod_w, mod_b)
    out = None

    for layer in range(depth):
        ctx_out = layer < depth - 1
        j = layer // 2
        rows_out = all_rows if ctx_out else lat_rows
        mod3 = mods[layer].reshape(MOD_ROWS, 1, 6 * d)

        h = _norm_mod(xa, norm_mix[layer], mod3, 0, all_rows, dims)[0]
        if layer % 2 == 0:
            w_rest, mu_main, mu_tail = _pack_even_weights(ev_w_in[j], rw_mu[j], lay, lw, la, mh, dqk, md)
            z_rkv = _proj(h, ev_w_in[j], all_rows, F32, n_cols=3 * rd)
            z = _proj(h, w_rest, all_rows, F32)
            r, v, kkn, lwd, km, bb, bonus, gmul = _rwkv_prep(
                z_rkv, z, all_rows, lay, dims, mu_main, mu_tail, rw_w0[j], _pad_rows(rw_w_up[j], lay["lwp"]).astype(BF16),
                rw_a0[j], _pad_rows(rw_a_up[j], lay["lwp"]).astype(BF16), rw_g_up[j].astype(BF16),
                rw_k_k[j], rw_k_a[j], rw_r_k[j])
            y = _rwkv_scan(r, v, kkn, lwd, km, bb, dims, heads_per_step=min(32, rd // RWKV_HEAD))
            g_off = lay["tail"] + lay["lg"] + lw
            gates = z[:, g_off:g_off + 4 * mh].reshape(all_rows, 2, 2 * mh)
            gates_col = jnp.moveaxis(gates, 1, 0).reshape(2, all_rows // CHUNK, CHUNK, 2 * mh)
            gates_row = jnp.swapaxes(gates_col, 2, 3)
            bias = ml_gate_b[j].reshape(2, 2 * mh)
            hm = _mlstm_scan(z, lay, dims, gates_col, gates_row, bias.reshape(2, 1, 2 * mh),
                             bias.reshape(2, 2 * mh, 1), mh, dqk, dv)
            mix = _even_finish(y, bonus, gmul, rw_gn_w[j], rw_gn_b[j], hm, z, lay, ml_norm[j],
                               rows_out, dims, mh, dv)
            xa_new = _proj_residual(mix, ev_w_out[j], xa, mod3, 2, rows_out, dims)
        else:
            w_qkv = at_w_qkv[j]
            cos_t, sin_t = _rope_tables(dims, head)
            q_scale = np.float32(head ** -0.5 * np.log2(np.e))
            q = _proj_qk(h, w_qkv, 0, d, at_q_norm[j], cos_t, sin_t, q_scale, lat_rows)
            k = _proj_qk(h, w_qkv, d, kv_dim, at_k_norm[j], cos_t, sin_t, np.float32(1.0), all_rows)
            v = _proj(h, w_qkv, all_rows, BF16, col_off=d + kv_dim, n_cols=kv_dim)
            att = _attention(q, k, v, dims, head, group)
            xa_new = _proj_residual(att, at_w_o[j], xa, mod3, 2, lat_rows, dims)
            if ctx_out:
                raise NotImplementedError("context output of an attention layer")
        xa = xa_new

        h2, idx, gate_t = _norm_mod(xa, norm_ffn[layer], mod3, 3, rows_out, dims,
                                    router=(router_w, router_bias))
        gate = gate_t.T
        pos, row_token, block_expert, n_used = _dispatch_plan(idx, n_experts, moe_bm)
        xs = h2.at[row_token].get(mode="promise_in_bounds")
        ys = _moe_ffn(xs, block_expert, n_used, exp_w1, exp_w3, exp_w2, layer, moe_bm)
        f0 = ys.at[pos[:, 0]].get(mode="promise_in_bounds")
        f1 = ys.at[pos[:, 1]].get(mode="promise_in_bounds")
        last = layer == depth - 1
        xa = _ffn_residual(xa, f0, f1, gate, mod3, 5, rows_out, dims,
                           final_norm=norm_final if last else None)
        if last:
            out = xa[:lat_rows].reshape(nb, seq, d)
    return out
```

```python
import functools

import jax
import jax.numpy as jnp
import numpy as np
from jax import lax
from jax.experimental import pallas as pl
from jax.experimental.pallas import tpu as pltpu

F32 = jnp.float32
BF16 = jnp.bfloat16
HIGHEST = lax.Precision.HIGHEST

GRID_W = 64
RMS_EPS = 1e-6
GN_EPS = 64e-5
RWKV_HEAD = 64
CHUNK = 64
N_GROUPS = 4
TOP_K = 2
ROPE_THETA = 10000.0
LANES = 128
MOD_ROWS = 8
VMEM_LIMIT = 56 * 1024 * 1024


def _rup(n, m):
    return (n + m - 1) // m * m


def _tile(n, pref, quantum):
    t = min(pref, n) // quantum * quantum
    while t >= quantum:
        if n % t == 0:
            return t
        t -= quantum
    return n


def _params(*sem):
    return pltpu.CompilerParams(dimension_semantics=sem, vmem_limit_bytes=VMEM_LIMIT)


def _dot(a, b):
    return jnp.dot(a.astype(BF16), b.astype(BF16), preferred_element_type=F32)


def _dot_nt(a, b):
    return lax.dot_general(a.astype(BF16), b.astype(BF16), (((1,), (1,)), ((), ())),
                           preferred_element_type=F32)


def _dot_tn(a, b):
    return lax.dot_general(a.astype(BF16), b.astype(BF16), (((0,), (0,)), ((), ())),
                           preferred_element_type=F32)


def _split3(x):
    hi = x.astype(BF16)
    rest = x - hi.astype(F32)
    mid = rest.astype(BF16)
    return hi, mid, (rest - mid.astype(F32)).astype(BF16)


def _dot_f32(a, b):
    return jnp.dot(a, b, preferred_element_type=F32, precision=HIGHEST)


def _dot_mask(x, mask):
    m = mask.astype(BF16)
    hi, mid, lo = _split3(x)
    return (jnp.dot(hi, m, preferred_element_type=F32) + jnp.dot(mid, m, preferred_element_type=F32)
            + jnp.dot(lo, m, preferred_element_type=F32))


def _pack_bf16_pairs(x):
    half = x.shape[1] // 2
    bits = lax.bitcast_convert_type(x.astype(BF16).astype(F32), jnp.uint32)
    return (bits[:, :half] >> 16) | bits[:, half:]


def _unpack_bf16_pairs(p):
    left = lax.bitcast_convert_type(p << 16, F32)
    right = lax.bitcast_convert_type(p & jnp.uint32(0xFFFF0000), F32)
    return jnp.concatenate([left, right], axis=1).astype(BF16)


def _mod_kernel(c_ref, w_ref, b_ref, o_ref):
    c = c_ref[...]
    c = c * jax.nn.sigmoid(c)
    o_ref[...] = _dot(c, w_ref[...]) + b_ref[...]


def _mod_tables(cond, mod_w, mod_b):
    d = cond.shape[1]
    depth, _, n = mod_w.shape
    tn = _tile(n, 512, LANES)
    return pl.pallas_call(
        _mod_kernel,
        grid=(depth, n // tn),
        in_specs=[pl.BlockSpec((MOD_ROWS, d), lambda l, j: (0, 0)),
                  pl.BlockSpec((None, d, tn), lambda l, j: (l, 0, j)),
                  pl.BlockSpec((None, 1, tn), lambda l, j: (l, 0, j))],
        out_specs=pl.BlockSpec((None, MOD_ROWS, tn), lambda l, j: (l, 0, j)),
        out_shape=jax.ShapeDtypeStruct((depth, MOD_ROWS, n), F32),
        compiler_params=_params("parallel", "parallel"),
        name="mod_tables",
    )(cond, mod_w, mod_b.reshape(depth, 1, n))


def _norm_mod_kernel(x_ref, g_ref, sh_ref, sc_ref, h_ref):
    x = x_ref[...]
    y = x * lax.rsqrt(jnp.mean(x * x, axis=-1, keepdims=True) + RMS_EPS) * g_ref[...]
    h_ref[...] = (y * (1.0 + sc_ref[...]) + sh_ref[...]).astype(h_ref.dtype)


def _split_rows_specs(block, col_of, n_lat_tiles, row_axis):
    def lat(*idx):
        return (jnp.minimum(idx[row_axis], n_lat_tiles - 1), col_of(*idx))

    def ctx(*idx):
        return (jnp.maximum(idx[row_axis] - n_lat_tiles, 0), col_of(*idx))
    return [pl.BlockSpec(block, lat), pl.BlockSpec(block, ctx)]


def _norm_mod_split_kernel(xl_ref, xc_ref, g_ref, sh_ref, sc_ref, h_ref, *, n_lat_tiles):
    x = jnp.where(pl.program_id(0) < n_lat_tiles, xl_ref[...], xc_ref[...])
    y = x * lax.rsqrt(jnp.mean(x * x, axis=-1, keepdims=True) + RMS_EPS) * g_ref[...]
    h_ref[...] = (y * (1.0 + sc_ref[...]) + sh_ref[...]).astype(h_ref.dtype)


def _top2_sum(a, b, c, d):
    hi1, lo1 = jnp.maximum(a, b), jnp.minimum(a, b)
    hi2, lo2 = jnp.maximum(c, d), jnp.minimum(c, d)
    return jnp.maximum(hi1, hi2) + jnp.maximum(jnp.minimum(hi1, hi2), jnp.maximum(lo1, lo2))


def _first_argmax(vals):
    best_v = vals[0]
    best_i = jnp.zeros(vals[0].shape, jnp.int32)
    for i in range(1, len(vals)):
        better = vals[i] > best_v
        best_i = jnp.where(better, i, best_i)
        best_v = jnp.where(better, vals[i], best_v)
    return best_i, best_v


def _pick(rows, index):
    out = rows[0]
    for i in range(1, len(rows)):
        out = jnp.where(index == i, rows[i], out)
    return out


def _norm_mod_router_kernel(x_ref, g_ref, sh_ref, sc_ref, rwt_ref, rb_ref, h_ref, idx_ref, gate_ref,
                            *, n_experts):
    x = x_ref[...]
    y = x * lax.rsqrt(jnp.mean(x * x, axis=-1, keepdims=True) + RMS_EPS) * g_ref[...]
    h = y * (1.0 + sc_ref[...]) + sh_ref[...]
    h_ref[...] = _pack_bf16_pairs(h)
    logits = lax.dot_general(rwt_ref[...], h, (((1,), (1,)), ((), ())), preferred_element_type=F32,
                             precision=HIGHEST)
    aff_all = jax.nn.sigmoid(logits)
    sel_all = aff_all + rb_ref[...]
    per_group = n_experts // N_GROUPS
    aff = [aff_all[e:e + 1, :] for e in range(n_experts)]
    sel = [sel_all[e:e + 1, :] for e in range(n_experts)]
    assert per_group == 4 and TOP_K == 2
    best, _ = _first_argmax([_top2_sum(*sel[g * per_group:(g + 1) * per_group]) for g in range(N_GROUPS)])
    cand = [_pick([sel[g * per_group + i] for g in range(N_GROUPS)], best) for i in range(per_group)]
    cand_aff = [_pick([aff[g * per_group + i] for g in range(N_GROUPS)], best) for i in range(per_group)]
    i1, _ = _first_argmax(cand)
    i2, _ = _first_argmax([jnp.where(i1 == i, -jnp.inf, cand[i]) for i in range(per_group)])
    g1 = _pick(cand_aff, i1)
    g2 = _pick(cand_aff, i2)
    idx_ref[0:1, :] = best * per_group + i1
    idx_ref[1:2, :] = best * per_group + i2
    gate_ref[0:1, :] = g1 / (g1 + g2)
    gate_ref[1:2, :] = g2 / (g1 + g2)


def _norm_mod(x, g, mod3, shift_chunk, n_rows, dims, router=None):
    split = isinstance(x, tuple)
    d = x[0].shape[1] if split else x.shape[1]
    tm = _tile(dims["ctx_rows"], 256, 8)
    grp = dims["group_of"](tm)
    if split:
        x_specs = _split_rows_specs((tm, d), lambda i: 0, dims["lat_rows"] // tm, 0)
        kern = functools.partial(_norm_mod_split_kernel, n_lat_tiles=dims["lat_rows"] // tm)
    else:
        x_specs = [pl.BlockSpec((tm, d), lambda i: (i, 0))]
        kern = _norm_mod_kernel
    in_specs = x_specs + [pl.BlockSpec((1, d), lambda i: (0, 0)),
                          pl.BlockSpec((None, 1, d), lambda i: (grp(i), 0, shift_chunk)),
                          pl.BlockSpec((None, 1, d), lambda i: (grp(i), 0, shift_chunk + 1))]
    args = (list(x) if split else [x]) + [g.reshape(1, d), mod3, mod3]
    out_specs = [pl.BlockSpec((tm, d), lambda i: (i, 0))]
    out_shape = [jax.ShapeDtypeStruct((n_rows, d), BF16)]
    if router is not None:
        router_w, router_bias = router
        n_experts = router_w.shape[1]
        in_specs += [pl.BlockSpec((n_experts, d), lambda i: (0, 0)),
                     pl.BlockSpec((n_experts, 1), lambda i: (0, 0))]
        args += [router_w.T, router_bias.astype(F32).reshape(n_experts, 1)]
        out_specs = [pl.BlockSpec((tm, d // 2), lambda i: (i, 0))] + [pl.BlockSpec((TOP_K, tm), lambda i: (0, i))] * 2
        out_shape = [jax.ShapeDtypeStruct((n_rows, d // 2), jnp.uint32),
                     jax.ShapeDtypeStruct((TOP_K, n_rows), jnp.int32),
                     jax.ShapeDtypeStruct((TOP_K, n_rows), F32)]
        kern = functools.partial(_norm_mod_router_kernel, n_experts=n_experts)
    return pl.pallas_call(
        kern, grid=(n_rows // tm,), in_specs=in_specs, out_specs=out_specs, out_shape=out_shape,
        compiler_params=_params("parallel"), name="norm_mod",
    )(*args)


def _resident_bf16(w_ref, wb_ref):
    @pl.when(pl.program_id(1) == 0)
    def _():
        wb_ref[...] = w_ref[...].astype(BF16)
    return wb_ref[...]


def _proj_kernel(a_ref, w_ref, o_ref, wb_ref):
    w = _resident_bf16(w_ref, wb_ref)
    o_ref[...] = jnp.dot(a_ref[...], w, preferred_element_type=F32).astype(o_ref.dtype)


def _weight_spec(w, k, tn, col_block):
    layer = w[1]
    return pl.BlockSpec((None, k, tn), lambda j, i: (layer, 0, col_block(j)))


def _proj(a, w, n_rows, out_dtype, col_off=0, n_cols=None, tm_pref=512, tn_pref=512):
    k = a.shape[1]
    n_cols = w[0].shape[2] - col_off if n_cols is None else n_cols
    tm = _tile(n_rows, tm_pref, 8)
    tn = _tile(int(np.gcd(n_cols, col_off)) if col_off else n_cols, tn_pref, LANES)
    off = col_off // tn
    return pl.pallas_call(
        _proj_kernel,
        grid=(n_cols // tn, n_rows // tm),
        in_specs=[pl.BlockSpec((tm, k), lambda j, i: (i, 0)),
                  _weight_spec(w, k, tn, lambda j: j + off)],
        out_specs=pl.BlockSpec((tm, tn), lambda j, i: (i, j)),
        out_shape=jax.ShapeDtypeStruct((n_rows, n_cols), out_dtype),
        scratch_shapes=[pltpu.VMEM((k, tn), BF16)],
        compiler_params=_params("parallel", "arbitrary"), name="proj",
    )(a, w[0])


def _proj_cols_kernel(a_ref, *refs, plan):
    n_w = len(refs) - 2
    w_refs, o_ref, wb_ref = refs[:n_w], refs[n_w], refs[n_w + 1]

    @pl.when(pl.program_id(1) == 0)
    def _():
        blocks = [r[...] for r in w_refs]
        pieces = [jnp.zeros((wb_ref.shape[0], hi - lo), F32) if slot is None else blocks[slot][:, lo:hi]
                  for slot, lo, hi in plan]
        wb_ref[...] = jnp.concatenate(pieces, axis=1).astype(BF16)

    o_ref[...] = jnp.dot(a_ref[...], wb_ref[...], preferred_element_type=F32)


def _proj_cols(a, w, n_rows, n_tiles, block_of, plan):
    k = a.shape[1]
    tn = sum(hi - lo for _, lo, hi in plan)
    tm = _tile(n_rows, 512, 8)
    return pl.pallas_call(
        functools.partial(_proj_cols_kernel, plan=tuple(plan)),
        grid=(n_tiles, n_rows // tm),
        in_specs=[pl.BlockSpec((tm, k), lambda j, i: (i, 0))]
                 + [_weight_spec(w, k, LANES, f) for f in block_of],
        out_specs=pl.BlockSpec((tm, tn), lambda j, i: (i, j)),
        out_shape=jax.ShapeDtypeStruct((n_rows, n_tiles * tn), F32),
        scratch_shapes=[pltpu.VMEM((k, tn), BF16)],
        compiler_params=_params("parallel", "arbitrary"), name="proj_cols",
    )(a, *([w[0]] * len(block_of)))


def _column_pieces(ranges, pad_to):
    block_ids = sorted({b for s, e in ranges for b in range(s // LANES, (e - 1) // LANES + 1)})
    plan = []
    for (s, e), width in zip(ranges, pad_to):
        for b in range(s // LANES, (e - 1) // LANES + 1):
            plan.append((block_ids.index(b), max(s, b * LANES) - b * LANES, min(e, (b + 1) * LANES) - b * LANES))
        if width > e - s:
            plan.append((None, 0, width - (e - s)))
    return block_ids, plan


def _proj_res_kernel(a_ref, w_ref, x_ref, gt_ref, o_ref, wb_ref):
    w = _resident_bf16(w_ref, wb_ref)
    acc = jnp.dot(a_ref[...], w, preferred_element_type=F32)
    o_ref[...] = x_ref[...] + gt_ref[...] * acc


def _proj_res_split_kernel(a_ref, w_ref, xl_ref, xc_ref, gt_ref, o_ref, wb_ref, *, n_lat_tiles):
    w = _resident_bf16(w_ref, wb_ref)
    acc = jnp.dot(a_ref[...], w, preferred_element_type=F32)
    x = jnp.where(pl.program_id(1) < n_lat_tiles, xl_ref[...], xc_ref[...])
    o_ref[...] = x + gt_ref[...] * acc


def _proj_residual(a, w, x, mod3, gate_chunk, n_rows, dims, tn_pref=512):
    k = a.shape[1]
    d = w[0].shape[2]
    tm = _tile(dims["ctx_rows"], 512, 8)
    tn = _tile(d, tn_pref, LANES)
    grp = dims["group_of"](tm)
    gblk = gate_chunk * (d // tn)
    if isinstance(x, tuple):
        x_specs = _split_rows_specs((tm, tn), lambda j, i: j, dims["lat_rows"] // tm, 1)
        kern = functools.partial(_proj_res_split_kernel, n_lat_tiles=dims["lat_rows"] // tm)
        x_args = list(x)
    else:
        x_specs = [pl.BlockSpec((tm, tn), lambda j, i: (i, j))]
        kern = _proj_res_kernel
        x_args = [x]
    return pl.pallas_call(
        kern,
        grid=(d // tn, n_rows // tm),
        in_specs=[pl.BlockSpec((tm, k), lambda j, i: (i, 0)),
                  _weight_spec(w, k, tn, lambda j: j)] + x_specs
                 + [pl.BlockSpec((None, 1, tn), lambda j, i: (grp(i), 0, gblk + j))],
        out_specs=pl.BlockSpec((tm, tn), lambda j, i: (i, j)),
        out_shape=jax.ShapeDtypeStruct((n_rows, d), F32),
        scratch_shapes=[pltpu.VMEM((k, tn), BF16)],
        compiler_params=_params("parallel", "arbitrary"), name="proj_residual",
    )(a, w[0], *x_args, mod3)


def _proj_qk_kernel(a_ref, w_ref, nw_ref, cos_ref, sin_ref, o_ref, wb_ref, *, head, scale):
    w = _resident_bf16(w_ref, wb_ref)
    nw = nw_ref[...] * scale
    tm = a_ref.shape[0]
    rows = _tile(tm, 128, 8)
    for r in range(0, tm, rows):
        acc = jnp.dot(a_ref[r:r + rows, :], w, preferred_element_type=F32)
        cs = cos_ref[r:r + rows, :]
        sn = sin_ref[r:r + rows, :]
        for s in range(acc.shape[1] // head):
            x = acc[:, s * head:(s + 1) * head]
            xn = x * lax.rsqrt(jnp.mean(x * x, axis=-1, keepdims=True) + RMS_EPS) * nw
            xr = xn * cs + pltpu.roll(xn, head // 2, 1) * sn
            o_ref[r:r + rows, s * head:(s + 1) * head] = xr.astype(o_ref.dtype)


def _proj_qk(a, w, col_off, n_cols, norm_w, cos_t, sin_t, scale, n_rows):
    k = a.shape[1]
    head = norm_w.shape[0]
    tm = _tile(n_rows, 512, 8)
    tn = _tile(int(np.gcd(n_cols, col_off)) if col_off else n_cols, 512, head)
    off = col_off // tn
    return pl.pallas_call(
        functools.partial(_proj_qk_kernel, head=head, scale=scale),
        grid=(n_cols // tn, n_rows // tm),
        in_specs=[pl.BlockSpec((tm, k), lambda j, i: (i, 0)),
                  _weight_spec(w, k, tn, lambda j: j + off),
                  pl.BlockSpec((1, head), lambda j, i: (0, 0)),
                  pl.BlockSpec((tm, head), lambda j, i: (i, 0)),
                  pl.BlockSpec((tm, head), lambda j, i: (i, 0))],
        out_specs=pl.BlockSpec((tm, tn), lambda j, i: (i, j)),
        out_shape=jax.ShapeDtypeStruct((n_rows, n_cols), BF16),
        scratch_shapes=[pltpu.VMEM((k, tn), BF16)],
        compiler_params=_params("parallel", "arbitrary"), name="proj_qk",
    )(a, w[0], norm_w.reshape(1, head), cos_t, sin_t)


def _seg_sum(x, seg):
    n = x.shape[1]
    lane_blk = LANES if n % LANES == 0 else n
    r = lax.broadcasted_iota(jnp.int32, (lane_blk, lane_blk), 0) // seg
    c = lax.broadcasted_iota(jnp.int32, (lane_blk, lane_blk), 1) // seg
    same_seg = r == c
    parts = [_dot_mask(x[:, s:s + lane_blk], same_seg) for s in range(0, n, lane_blk)]
    return parts[0] if len(parts) == 1 else jnp.concatenate(parts, axis=1)


def _rwkv_prep_kernel(z_ref, zp_ref, zn_ref, t_ref, tp_ref, tn_ref, mu_ref, mut_ref, w0_ref, wup_ref,
                      a0_ref, aup_ref, gup_ref, kk_ref, ka_ref, rk_ref,
                      r_out, v_out, kkn_out, lw_out, km_out, bb_out, bon_out, gm_out,
                      *, rd, lg, lwp, seq, ctx, n_lat_rows):
    tm = z_ref.shape[0]
    row0 = pl.program_id(0) * tm
    in_lat = row0 < n_lat_rows
    seq_len = jnp.where(in_lat, seq, ctx)
    pos0 = jnp.where(in_lat, row0 % seq, (row0 - n_lat_rows) % ctx)
    has_prev = (pos0 != 0).astype(F32)
    has_next = (pos0 + tm != seq_len).astype(F32)

    def token_shift(cur_ref, prev_ref, next_ref, mix_ref):
        cur = cur_ref[...]
        rows = lax.broadcasted_iota(jnp.int32, cur.shape, 0)
        prev_row = prev_ref[7:8, :] * has_prev
        next_row = next_ref[0:1, :] * has_next
        before = jnp.where(rows == 0, prev_row, pltpu.roll(cur, 1, 0))
        after = jnp.where(rows == tm - 1, next_row, pltpu.roll(cur, tm - 1, 0))
        return cur + mix_ref[...] * (0.5 * (before + after) - cur)

    zs = token_shift(z_ref, zp_ref, zn_ref, mu_ref)
    ts = token_shift(t_ref, tp_ref, tn_ref, mut_ref)
    r = zs[:, 0:rd]
    k = zs[:, rd:2 * rd]
    v = zs[:, 2 * rd:3 * rd]
    g_down = ts[:, 0:lg]
    w_down = ts[:, lg:lg + lwp]
    a_down = ts[:, lg + lwp:lg + 2 * lwp]

    kk = k * kk_ref[...]
    kk = kk * lax.rsqrt(jnp.maximum(_seg_sum(kk * kk, RWKV_HEAD), 1e-24))
    r_out[...] = r
    v_out[...] = v
    kkn_out[...] = kk
    tw = jnp.tanh(w_down)
    bonus = jnp.zeros_like(r)
    for d in range(2):
        lw_out[d] = -np.float32(np.exp(-0.5)) * jax.nn.sigmoid(w0_ref[d] + _dot(tw, wup_ref[d]))
        a = jax.nn.sigmoid(a0_ref[d] + _dot(a_down, aup_ref[d]))
        k_mod = k * (1.0 + (a - 1.0) * ka_ref[...])
        km_out[d] = k_mod
        bb_out[d] = kk * a
        bonus = bonus + _seg_sum(r * k_mod * rk_ref[...], RWKV_HEAD) * v
    bon_out[...] = bonus
    gm_out[...] = _dot(jax.nn.sigmoid(g_down), gup_ref[...])


def _rwkv_prep(z_rkv, z, n_rows, lay, dims, mu_main, mu_tail, w0, w_up_p, a0, a_up_p, g_up, k_k, k_a, r_k):
    rd, lg, lwp = lay["rd"], lay["lg"], lay["lwp"]
    mw = 3 * rd
    tw = lg + 2 * lwp
    tblk = 0
    tm = _tile(dims["ctx"], 64, 8)
    nb8 = n_rows // 8
    full = lambda shape: pl.BlockSpec(shape, lambda i: (0,) * len(shape))
    row_spec = pl.BlockSpec((tm, rd), lambda i: (i, 0))
    dir_spec = pl.BlockSpec((2, tm, rd), lambda i: (0, i, 0))
    sds = jax.ShapeDtypeStruct
    kern = functools.partial(_rwkv_prep_kernel, rd=rd, lg=lg, lwp=lwp, seq=dims["seq"], ctx=dims["ctx"],
                             n_lat_rows=dims["lat_rows"])
    return pl.pallas_call(
        kern,
        grid=(n_rows // tm,),
        in_specs=[pl.BlockSpec((tm, mw), lambda i: (i, 0)),
                  pl.BlockSpec((8, mw), lambda i: (jnp.maximum(i * (tm // 8) - 1, 0), 0)),
                  pl.BlockSpec((8, mw), lambda i: (jnp.minimum((i + 1) * (tm // 8), nb8 - 1), 0)),
                  pl.BlockSpec((tm, tw), lambda i: (i, tblk)),
                  pl.BlockSpec((8, tw), lambda i: (jnp.maximum(i * (tm // 8) - 1, 0), tblk)),
                  pl.BlockSpec((8, tw), lambda i: (jnp.minimum((i + 1) * (tm // 8), nb8 - 1), tblk)),
                  full((1, mw)), full((1, tw)), full((2, 1, rd)), full((2, lwp, rd)), full((2, 1, rd)), full((2, lwp, rd)),
                  full((lg, rd)), full((1, rd)), full((1, rd)), full((1, rd))],
        out_specs=[row_spec, row_spec, row_spec, dir_spec, dir_spec, dir_spec, row_spec, row_spec],
        out_shape=[sds((n_rows, rd), F32)] * 3 + [sds((2, n_rows, rd), F32)] * 3 + [sds((n_rows, rd), F32)] * 2,
        compiler_params=_params("parallel"), name="rwkv_prep",
    )(z_rkv, z_rkv, z_rkv, z, z, z, mu_main, mu_tail, w0.reshape(2, 1, rd), w_up_p, a0.reshape(2, 1, rd), a_up_p, g_up,
      k_k.reshape(1, rd), k_a.reshape(1, rd), r_k.reshape(1, rd))


def _rwkv_scan_kernel(r_ref, v_ref, kk_ref, lw_ref, km_ref, bb_ref, y_ref, s_ref, *, heads):
    d = pl.program_id(0)
    j = pl.program_id(3)

    @pl.when(j == 0)
    def _():
        s_ref[...] = jnp.zeros_like(s_ref)

    n = RWKV_HEAD
    chunk = lw_ref.shape[0]
    t_idx = lax.broadcasted_iota(jnp.int32, (chunk, chunk), 0)
    s_idx = lax.broadcasted_iota(jnp.int32, (chunk, chunk), 1)
    lead = jnp.where(d == 0, t_idx - s_idx, s_idx - t_idx)
    incl = lead >= 0
    assert chunk & (chunk - 1) == 0

    lw = lw_ref[...]
    cum = _dot_f32(incl.astype(F32), lw)
    tot = jnp.sum(lw, axis=0, keepdims=True)
    e_in = jnp.exp(cum)
    e_neg = jnp.exp(-cum)
    e_last = jnp.exp(tot - cum)
    kkn = kk_ref[...]
    km = km_ref[...]
    bb = bb_ref[...]
    r_t = r_ref[...] * e_in
    a_t = -kkn * jnp.exp(cum - lw)
    b_t = bb * e_neg
    k_t = km * e_neg
    b_l = bb * e_last
    k_l = km * e_last
    w_l = jnp.exp(tot)
    v_all = v_ref[...]

    hs = range(heads)
    sl = [slice(h * n, (h + 1) * n) for h in hs]
    t2 = lax.broadcasted_iota(jnp.int32, (2 * chunk, 2 * chunk), 0)
    s2 = lax.broadcasted_iota(jnp.int32, (2 * chunk, 2 * chunk), 1)
    tt = jnp.where(t2 >= chunk, t2 - chunk, t2)
    ss = jnp.where(s2 >= chunk, s2 - chunk, s2)
    keep = jnp.where(d == 0, tt - ss, ss - tt) >= jnp.where(t2 < chunk, 1, 0)
    right = lax.broadcasted_iota(jnp.int32, (chunk, 2 * chunk), 1) >= chunk
    eye_right = (lax.broadcasted_iota(jnp.int32, (chunk, 2 * chunk), 1)
                 == lax.broadcasted_iota(jnp.int32, (chunk, 2 * chunk), 0) + chunk).astype(F32)
    zeros_v = jnp.zeros((chunk, n), F32)
    stack = lambda top, bottom: jnp.concatenate([top, bottom], axis=0)

    s0 = [s_ref[h] for h in hs]
    v = [v_all[:, sl[h]] for h in hs]
    pair = [jnp.where(keep, _dot_nt(stack(a_t[:, sl[h]], r_t[:, sl[h]]), stack(b_t[:, sl[h]], k_t[:, sl[h]])), 0.0)
            for h in hs]
    a_side = [pair[h][:chunk] for h in hs]
    r_side = [pair[h][chunk:] for h in hs]
    x = [_dot_nt(a_t[:, sl[h]], s0[h]) + _dot(a_side[h], stack(zeros_v, v[h])) for h in hs]
    y0 = [_dot_nt(r_t[:, sl[h]], s0[h]) for h in hs]
    q = [jnp.where(right, eye_right, a_side[h]) for h in hs]
    for _ in range(chunk.bit_length() - 1):
        q = [_dot(q[h][:, :chunk], q[h]) + jnp.where(right, q[h], 0.0) for h in hs]
    uv = [stack(_dot(q[h][:, chunk:], x[h]), v[h]) for h in hs]
    y_ref[...] = jnp.concatenate([y0[h] + _dot(r_side[h], uv[h]) for h in hs], axis=1)
    for h in hs:
        s_ref[h] = s0[h] * w_l[:, sl[h]] + _dot_tn(uv[h], stack(b_l[:, sl[h]], k_l[:, sl[h]]))


def _chunk_index(d, b, j, dims):
    nc_ctx, nc_lat, nb = dims["ctx"] // CHUNK, dims["seq"] // CHUNK, dims["batch"]
    jc = jnp.where(d == 0, j, nc_ctx - 1 - j)
    jl = jnp.where(d == 0, j - nc_ctx, nc_lat - 1 - (j - nc_ctx))
    return jnp.where(j < nc_ctx, nb * nc_lat + b * nc_ctx + jc, b * nc_lat + jl)


def _rwkv_scan(r, v, kkn, lw, km, bb, dims, heads_per_step):
    n_rows, rd = r.shape
    gw = heads_per_step * RWKV_HEAD
    n_chunks = (dims["ctx"] + dims["seq"]) // CHUNK
    cidx = lambda d, b, g, j: _chunk_index(d, b, j, dims)
    row_spec = pl.BlockSpec((CHUNK, gw), lambda d, b, g, j: (cidx(d, b, g, j), g))
    dir_spec = pl.BlockSpec((None, CHUNK, gw), lambda d, b, g, j: (d, cidx(d, b, g, j), g))
    return pl.pallas_call(
        functools.partial(_rwkv_scan_kernel, heads=heads_per_step),
        grid=(2, dims["batch"], rd // gw, n_chunks),
        in_specs=[row_spec, row_spec, row_spec, dir_spec, dir_spec, dir_spec],
        out_specs=dir_spec,
        out_shape=jax.ShapeDtypeStruct((2, n_rows, rd), F32),
        scratch_shapes=[pltpu.VMEM((heads_per_step, RWKV_HEAD, RWKV_HEAD), F32)],
        compiler_params=_params("parallel", "parallel", "parallel", "arbitrary"), name="rwkv_scan",
    )(r, v, kkn, lw, km, bb)


def _mlstm_kernel(q_ref, k_ref, v_ref, gc_ref, gr_ref, bc_ref, br_ref, h_ref, c_ref, n_ref, m_ref,
                  *, heads, dqk, dv):
    d = pl.program_id(0)
    j = pl.program_id(2)

    @pl.when(j == 0)
    def _():
        c_ref[...] = jnp.zeros_like(c_ref)
        n_ref[...] = jnp.zeros_like(n_ref)
        m_ref[...] = jnp.zeros_like(m_ref)

    chunk = q_ref.shape[0]
    t_idx = lax.broadcasted_iota(jnp.int32, (chunk, chunk), 0)
    s_idx = lax.broadcasted_iota(jnp.int32, (chunk, chunk), 1)
    lead = jnp.where(d == 0, t_idx - s_idx, s_idx - t_idx)
    incl = lead >= 0
    tri = incl.astype(F32)
    tri_t = (lead <= 0).astype(F32)

    gcol = gc_ref[...] + bc_ref[...]
    grow = gr_ref[...] + br_ref[...]
    i_col = gcol[:, :heads]
    f_col = jax.nn.log_sigmoid(gcol[:, heads:])
    i_row = grow[:heads, :]
    f_row = jax.nn.log_sigmoid(grow[heads:, :])
    b_col = _dot_f32(tri, f_col)
    b_row = _dot_f32(f_row, tri_t)
    b_last = jnp.sum(f_col, axis=0, keepdims=True)
    scale = np.float32(dqk ** -0.5)

    hs = range(heads)
    q = [q_ref[:, h * dqk:(h + 1) * dqk] * scale for h in hs]
    k = [k_ref[:, h * dqk:(h + 1) * dqk] for h in hs]
    v = [v_ref[:, h * dv:(h + 1) * dv] for h in hs]
    c_st = [c_ref[h] for h in hs]
    n_st = [n_ref[h] for h in hs]
    m_st = [m_ref[h][:, 0:1] for h in hs]
    bc = [b_col[:, h:h + 1] for h in hs]
    qk = [_dot_nt(q[h], k[h]) for h in hs]
    qc = [_dot(q[h], c_st[h]) for h in hs]
    dmat = [jnp.where(incl, bc[h] + (i_row[h:h + 1, :] - b_row[h:h + 1, :]), -jnp.inf) for h in hs]
    inter = [bc[h] + m_st[h] for h in hs]
    m_t = [jnp.maximum(inter[h], jnp.max(dmat[h], axis=-1, keepdims=True)) for h in hs]
    w_inter = [jnp.exp(inter[h] - m_t[h]) for h in hs]
    s = [qk[h] * jnp.exp(dmat[h] - m_t[h]) for h in hs]
    num = [w_inter[h] * qc[h] + _dot(s[h], v[h]) for h in hs]
    den = [w_inter[h] * jnp.sum(q[h] * n_st[h], axis=-1, keepdims=True) + jnp.sum(s[h], axis=-1, keepdims=True)
           for h in hs]
    h_ref[...] = jnp.concatenate([num[h] / jnp.maximum(jnp.abs(den[h]), jnp.exp(-m_t[h])) for h in hs], axis=1)
    bl = [b_last[:, h:h + 1] for h in hs]
    g = [bl[h] - bc[h] + i_col[:, h:h + 1] for h in hs]
    m_new = [jnp.maximum(bl[h] + m_st[h], jnp.max(g[h], axis=0, keepdims=True)) for h in hs]
    decay = [jnp.exp(bl[h] + m_st[h] - m_new[h]) for h in hs]
    kw = [k[h] * jnp.exp(g[h] - m_new[h]) for h in hs]
    for h in hs:
        c_ref[h] = decay[h] * c_st[h] + _dot_tn(kw[h], v[h])
        n_ref[h] = decay[h] * n_st[h] + jnp.sum(kw[h], axis=0, keepdims=True)
        m_ref[h] = jnp.broadcast_to(m_new[h], m_ref.shape[1:])


def _mlstm_scan(z, lay, dims, gates_col, gates_row, bias_col, bias_row, heads, dqk, dv):
    n_rows = z.shape[0]
    n_chunks = (dims["ctx"] + dims["seq"]) // CHUNK
    md = heads * dv
    qw = heads * dqk
    cidx = lambda d, b, j: _chunk_index(d, b, j, dims)
    q_blk, k_blk, v_blk = lay["mq"] // qw, lay["mk"] // qw, lay["mv"] // md
    return pl.pallas_call(
        functools.partial(_mlstm_kernel, heads=heads, dqk=dqk, dv=dv),
        grid=(2, dims["batch"], n_chunks),
        in_specs=[pl.BlockSpec((CHUNK, qw), lambda d, b, j: (cidx(d, b, j), q_blk)),
                  pl.BlockSpec((CHUNK, qw), lambda d, b, j: (cidx(d, b, j), k_blk)),
                  pl.BlockSpec((CHUNK, md), lambda d, b, j: (cidx(d, b, j), v_blk)),
                  pl.BlockSpec((None, None, CHUNK, 2 * heads), lambda d, b, j: (d, cidx(d, b, j), 0, 0)),
                  pl.BlockSpec((None, None, 2 * heads, CHUNK), lambda d, b, j: (d, cidx(d, b, j), 0, 0)),
                  pl.BlockSpec((None, 1, 2 * heads), lambda d, b, j: (d, 0, 0)),
                  pl.BlockSpec((None, 2 * heads, 1), lambda d, b, j: (d, 0, 0))],
        out_specs=pl.BlockSpec((None, CHUNK, md), lambda d, b, j: (d, cidx(d, b, j), 0)),
        out_shape=jax.ShapeDtypeStruct((2, n_rows, md), F32),
        scratch_shapes=[pltpu.VMEM((heads, dqk, dv), F32), pltpu.VMEM((heads, 1, dqk), F32),
                        pltpu.VMEM((heads, 1, LANES), F32)],
        compiler_params=_params("parallel", "parallel", "arbitrary"), name="mlstm_scan",
    )(z, z, z, gates_col, gates_row, bias_col, bias_row)


def _even_finish_kernel(y_ref, bon_ref, gm_ref, gnw_ref, gnb_ref, h_ref, o_ref, nw_ref, out_ref,
                        *, rd, heads, dv):
    y = y_ref[0] + y_ref[1]
    inv_n = np.float32(1.0 / RWKV_HEAD)
    mean = _seg_sum(y, RWKV_HEAD) * inv_n
    yc = y - mean
    var = _seg_sum(yc * yc, RWKV_HEAD) * inv_n
    yn = yc * lax.rsqrt(var + GN_EPS) * gnw_ref[...] + gnb_ref[...] + bon_ref[...]
    out_ref[:, 0:rd] = (yn * gm_ref[...]).astype(out_ref.dtype)
    hm = h_ref[0] + h_ref[1]
    for h in range(heads):
        sl = slice(h * dv, (h + 1) * dv)
        x = hm[:, sl]
        xn = x * lax.rsqrt(jnp.mean(x * x, axis=-1, keepdims=True) + RMS_EPS) * nw_ref[:, sl]
        out_ref[:, rd + h * dv:rd + (h + 1) * dv] = (jax.nn.sigmoid(o_ref[:, sl]) * xn).astype(out_ref.dtype)


def _even_finish(y, bonus, gmul, gn_w, gn_b, hm, z, lay, ml_norm, n_rows, dims, heads, dv):
    rd = y.shape[2]
    md = hm.shape[2]
    tm = _tile(dims["ctx_rows"], 256, 8)
    o_blk = lay["mo"] // md
    return pl.pallas_call(
        functools.partial(_even_finish_kernel, rd=rd, heads=heads, dv=dv),
        grid=(n_rows // tm,),
        in_specs=[pl.BlockSpec((2, tm, rd), lambda i: (0, i, 0)),
                  pl.BlockSpec((tm, rd), lambda i: (i, 0)),
                  pl.BlockSpec((tm, rd), lambda i: (i, 0)),
                  pl.BlockSpec((1, rd), lambda i: (0, 0)),
                  pl.BlockSpec((1, rd), lambda i: (0, 0)),
                  pl.BlockSpec((2, tm, md), lambda i: (0, i, 0)),
                  pl.BlockSpec((tm, md), lambda i: (i, o_blk)),
                  pl.BlockSpec((1, md), lambda i: (0, 0))],
        out_specs=pl.BlockSpec((tm, rd + md), lambda i: (i, 0)),
        out_shape=jax.ShapeDtypeStruct((n_rows, rd + md), BF16),
        compiler_params=_params("parallel"), name="even_finish",
    )(y, bonus, gmul, gn_w.reshape(1, rd), gn_b.reshape(1, rd), hm, z, ml_norm.reshape(1, md))


def _attn_kernel(q_ref, kl_ref, kc_ref, vl_ref, vc_ref, o_ref, *, group, head):
    tq = q_ref.shape[0]
    q = jnp.concatenate([q_ref[:, g * head:(g + 1) * head] for g in range(group)], axis=0)
    ck = _tile(kl_ref.shape[0], 256, 8)
    keys = [kc_ref[...]] + [kl_ref[c * ck:(c + 1) * ck, :] for c in range(kl_ref.shape[0] // ck)]
    vals = [vc_ref[...]] + [vl_ref[c * ck:(c + 1) * ck, :] for c in range(kl_ref.shape[0] // ck)]
    m = acc = None
    for kc, vc in zip(keys, vals):
        s = _dot_nt(q, kc)
        row_max = jnp.max(s, axis=-1, keepdims=True)
        v_aug = jnp.concatenate([vc, jnp.ones_like(vc)], axis=1)
        if m is None:
            m = row_max
            acc = _dot(jnp.exp2(s - m), v_aug)
        else:
            m_new = jnp.maximum(m, row_max)
            acc = acc * jnp.exp2(m - m_new) + _dot(jnp.exp2(s - m_new), v_aug)
            m = m_new
    o = acc[:, :head] / acc[:, head:head + 1]
    for g in range(group):
        o_ref[:, g * head:(g + 1) * head] = o[g * tq:(g + 1) * tq].astype(o_ref.dtype)


def _attention(q, k, v, dims, head, group):
    seq, ctx, nb = dims["seq"], dims["ctx"], dims["batch"]
    kvh = k.shape[1] // head
    tq = _tile(seq, 256, 8)
    nq = seq // tq
    gw = group * head
    return pl.pallas_call(
        functools.partial(_attn_kernel, group=group, head=head),
        grid=(nb, kvh, nq),
        in_specs=[pl.BlockSpec((tq, gw), lambda b, h, i: (b * nq + i, h)),
                  pl.BlockSpec((seq, head), lambda b, h, i: (b, h)),
                  pl.BlockSpec((ctx, head), lambda b, h, i: (nb * (seq // ctx) + b, h)),
                  pl.BlockSpec((seq, head), lambda b, h, i: (b, h)),
                  pl.BlockSpec((ctx, head), lambda b, h, i: (nb * (seq // ctx) + b, h))],
        out_specs=pl.BlockSpec((tq, gw), lambda b, h, i: (b * nq + i, h)),
        out_shape=jax.ShapeDtypeStruct(q.shape, BF16),
        compiler_params=_params("parallel", "parallel", "parallel"), name="attention",
    )(q, k, k, v, v)


def _new_expert(i, be_ref):
    return jnp.logical_or(i == 0, be_ref[i] != be_ref[jnp.maximum(i - 1, 0)])


def _moe_up_kernel(be_ref, nu_ref, x_ref, w1_ref, w3_ref, mid_ref, w1_bf, w3_bf):
    i = pl.program_id(1)

    @pl.when(_new_expert(i, be_ref))
    def _():
        w1_bf[...] = w1_ref[...].astype(BF16)
        w3_bf[...] = w3_ref[...].astype(BF16)

    @pl.when(i < nu_ref[0])
    def _():
        x = _unpack_bf16_pairs(x_ref[...])
        a = jnp.dot(x, w1_bf[...], preferred_element_type=F32)
        b = jnp.dot(x, w3_bf[...], preferred_element_type=F32)
        mid_ref[...] = (a * jax.nn.sigmoid(a) * b).astype(mid_ref.dtype)

    @pl.when(i >= nu_ref[0])
    def _():
        mid_ref[...] = jnp.zeros_like(mid_ref)


def _moe_down_kernel(be_ref, nu_ref, mid_ref, w2_ref, o_ref, w2_bf):
    i = pl.program_id(1)

    @pl.when(_new_expert(i, be_ref))
    def _():
        w2_bf[...] = w2_ref[...].astype(BF16)

    @pl.when(i < nu_ref[0])
    def _():
        o_ref[...] = jnp.dot(mid_ref[...], w2_bf[...], preferred_element_type=F32)

    @pl.when(i >= nu_ref[0])
    def _():
        o_ref[...] = jnp.zeros_like(o_ref)


def _moe_ffn(xs, block_expert, n_used, w1, w3, w2, layer, bm):
    n_rows = xs.shape[0]
    d, de = w1.shape[2], w1.shape[3]
    tde = _tile(de, 256, LANES)
    tn = _tile(d, 2048, LANES)
    nblk = n_rows // bm
    mid = pl.pallas_call(
        _moe_up_kernel,
        grid_spec=pltpu.PrefetchScalarGridSpec(
            num_scalar_prefetch=2, grid=(de // tde, nblk),
            in_specs=[pl.BlockSpec((bm, d // 2), lambda k, i, be, nu: (i, 0)),
                      pl.BlockSpec((None, None, d, tde), lambda k, i, be, nu: (layer, be[i], 0, k)),
                      pl.BlockSpec((None, None, d, tde), lambda k, i, be, nu: (layer, be[i], 0, k))],
            out_specs=pl.BlockSpec((bm, tde), lambda k, i, be, nu: (i, k)),
            scratch_shapes=[pltpu.VMEM((d, tde), BF16), pltpu.VMEM((d, tde), BF16)]),
        out_shape=jax.ShapeDtypeStruct((n_rows, de), BF16),
        compiler_params=_params("arbitrary", "arbitrary"), name="moe_up",
    )(block_expert, n_used, xs, w1, w3)
    return pl.pallas_call(
        _moe_down_kernel,
        grid_spec=pltpu.PrefetchScalarGridSpec(
            num_scalar_prefetch=2, grid=(d // tn, nblk),
            in_specs=[pl.BlockSpec((bm, de), lambda n, i, be, nu: (i, 0)),
                      pl.BlockSpec((None, None, de, tn), lambda n, i, be, nu: (layer, be[i], 0, n))],
            out_specs=pl.BlockSpec((bm, tn), lambda n, i, be, nu: (i, n)),
            scratch_shapes=[pltpu.VMEM((de, tn), BF16)]),
        out_shape=jax.ShapeDtypeStruct((n_rows, d), F32),
        compiler_params=_params("arbitrary", "arbitrary"), name="moe_down",
    )(block_expert, n_used, mid, w2)


def _dispatch_plan(idx, n_experts, bm):
    n_tok = idx.shape[1]
    n_assign = n_tok * TOP_K
    expert = idx.T.reshape(-1)
    onehot = (expert[:, None] == jnp.arange(n_experts)[None, :]).astype(jnp.int32)
    rank = jnp.take_along_axis(jnp.cumsum(onehot, axis=0) - onehot, expert[:, None], 1)[:, 0]
    counts = jnp.sum(onehot, axis=0)
    padded = (counts + bm - 1) // bm * bm
    pad_end = jnp.cumsum(padded)
    pos = (pad_end - padded)[expert] + rank
    n_blocks = -(-n_assign // bm) + n_experts
    token = jnp.repeat(jnp.arange(n_tok, dtype=jnp.int32), TOP_K)
    filler = jnp.arange(n_blocks * bm, dtype=jnp.int32) % n_tok
    row_token = filler.at[pos].set(token)
    block_start = jnp.arange(n_blocks, dtype=jnp.int32) * bm
    block_expert = jnp.minimum(jnp.sum((pad_end[None, :] <= block_start[:, None]).astype(jnp.int32), axis=1),
                               n_experts - 1)
    n_used = (pad_end[-1] // bm).astype(jnp.int32).reshape(1)
    block_expert = jnp.where(jnp.arange(n_blocks) < n_used[0], block_expert,
                             block_expert[jnp.maximum(n_used[0] - 1, 0)])
    return pos.reshape(n_tok, TOP_K), row_token, block_expert, n_used


def _ffn_res_kernel(x_ref, f0_ref, f1_ref, g_ref, gt_ref, o_ref):
    g = g_ref[...]
    f = g[:, 0:1] * f0_ref[...] + g[:, 1:2] * f1_ref[...]
    o_ref[...] = x_ref[...] + gt_ref[...] * f


def _ffn_res_norm_kernel(x_ref, f0_ref, f1_ref, g_ref, gt_ref, nw_ref, o_ref):
    g = g_ref[...]
    f = g[:, 0:1] * f0_ref[...] + g[:, 1:2] * f1_ref[...]
    x = x_ref[...] + gt_ref[...] * f
    o_ref[...] = x * lax.rsqrt(jnp.mean(x * x, axis=-1, keepdims=True) + RMS_EPS) * nw_ref[...]


def _ffn_residual(x, f0, f1, gate, mod3, gate_chunk, n_rows, dims, final_norm=None):
    d = x.shape[1]
    tm = _tile(dims["ctx_rows"], 256, 8)
    grp = dims["group_of"](tm)
    row = pl.BlockSpec((tm, d), lambda i: (i, 0))
    in_specs = [row, row, row, pl.BlockSpec((tm, TOP_K), lambda i: (i, 0)),
                pl.BlockSpec((None, 1, d), lambda i: (grp(i), 0, gate_chunk))]
    args = [x, f0, f1, gate, mod3]
    kern = _ffn_res_kernel
    if final_norm is not None:
        in_specs.append(pl.BlockSpec((1, d), lambda i: (0, 0)))
        args.append(final_norm.reshape(1, d))
        kern = _ffn_res_norm_kernel
    return pl.pallas_call(
        kern, grid=(n_rows // tm,), in_specs=in_specs, out_specs=row,
        out_shape=jax.ShapeDtypeStruct((n_rows, d), F32),
        compiler_params=_params("parallel"), name="ffn_residual",
    )(*args)


def _even_layout(rd, lw, la, lg, mh, dqk, md):
    lwp = _rup(lw + 4 * mh, LANES)
    assert _rup(la, LANES) == lwp
    lay = {"rd": rd, "lg": lg, "lwp": lwp}
    off = 0
    for name, width in (("mq", mh * dqk), ("mk", mh * dqk), ("mv", md), ("mo", md)):
        assert off % width == 0
        lay[name] = off
        off += width
    lay["width"] = off
    return lay


def _pad_cols(seg, width):
    return jnp.pad(seg, ((0, 0), (0, width - seg.shape[1])))


def _even_projection(h, w_in, n_rows, lay, lw, la, mh):
    rd, lg, lwp = lay["rd"], lay["lg"], lay["lwp"]
    o = 3 * rd
    rww = o + lw + la + lg
    m_end = rww + lay["width"]
    z_rkv = _proj(h, w_in, n_rows, F32, n_cols=o)
    tn = _tile(lay["width"], 512, LANES)
    shift, base, per_tile = rww % LANES, rww // LANES, tn // LANES
    if shift:
        plan = [(0, shift, LANES)] + [(q, 0, LANES) for q in range(1, per_tile)] + [(per_tile, 0, shift)]
    else:
        plan = [(q, 0, LANES) for q in range(per_tile)]
    slots = per_tile + (1 if shift else 0)
    z_m = _proj_cols(h, w_in, n_rows, lay["width"] // tn,
                     [lambda j, q=q: base + per_tile * j + q for q in range(slots)], plan)
    ids, plan_t = _column_pieces([(o + lw + la, rww), (o, o + lw), (m_end, m_end + 4 * mh), (o + lw, o + lw + la)],
                                 [lg, lw, lwp - lw, lwp])
    z_t = _proj_cols(h, w_in, n_rows, 1, [lambda j, b=b: b for b in ids], plan_t)
    return z_rkv, z_m, z_t


def _mix_vectors(mu, lay, lw, la):
    rd, lg, lwp = lay["rd"], lay["lg"], lay["lwp"]
    o = 3 * rd
    rww = o + lw + la + lg
    mu2 = mu.reshape(1, -1)
    mu_tail = jnp.concatenate([mu2[:, o + lw + la:rww], _pad_cols(mu2[:, o:o + lw], lwp),
                               _pad_cols(mu2[:, o + lw:o + lw + la], lwp)], axis=1)
    return mu2[:, :o], mu_tail


def _pad_rows(w, rows):
    return jnp.pad(w, ((0, 0), (0, rows - w.shape[1]), (0, 0)))


def _rope_tables(dims, head):
    seq, nb = dims["seq"], dims["batch"]
    pairs = head // 4
    rows = seq // GRID_W
    row = np.repeat(np.arange(rows), GRID_W).astype(np.float32)
    col = np.tile(np.arange(GRID_W), rows).astype(np.float32)
    inv = (np.float32(ROPE_THETA) ** (-np.arange(pairs, dtype=np.float32) / np.float32(pairs))).astype(np.float32)
    ang = np.concatenate([row[:, None] * inv, col[:, None] * inv], -1)
    cos, sin = np.cos(ang).astype(np.float32), np.sin(ang).astype(np.float32)
    cos_t = np.tile(np.concatenate([cos, cos], -1), (nb, 1))
    sin_t = np.tile(np.concatenate([-sin, sin], -1), (nb, 1))
    n_ctx = dims["ctx_rows"]
    return (jnp.asarray(np.concatenate([cos_t, np.ones((n_ctx, head), np.float32)], 0)),
            jnp.asarray(np.concatenate([sin_t, np.zeros((n_ctx, head), np.float32)], 0)))


def kernel(x, c, ctx, c_ctx, mod_w, mod_b, norm_mix, norm_ffn, norm_final, router_w, router_bias,
           exp_w1, exp_w3, exp_w2, ev_w_in, ev_w_out, rw_mu, rw_w0, rw_w_up, rw_a0, rw_a_up, rw_g_up,
           rw_k_k, rw_k_a, rw_r_k, rw_gn_w, rw_gn_b, ml_gate_b, ml_norm, at_w_qkv, at_q_norm, at_k_norm,
           at_w_o):
    nb, seq, d = x.shape
    n_ctx = ctx.shape[1]
    depth = mod_w.shape[0]
    lat_rows, ctx_rows = nb * seq, nb * n_ctx
    all_rows = lat_rows + ctx_rows
    assert nb + 1 <= MOD_ROWS and seq % n_ctx == 0 and n_ctx % CHUNK == 0
    dims = {"batch": nb, "seq": seq, "ctx": n_ctx, "lat_rows": lat_rows, "ctx_rows": ctx_rows,
            "group_of": lambda tm: (lambda i: jnp.minimum(i * tm // seq, nb))}

    n_experts = router_w.shape[1]
    rd = rw_w0.shape[-1]
    lw, la, lg = rw_w_up.shape[2], rw_a_up.shape[2], rw_g_up.shape[1]
    mh = ml_gate_b.shape[-1]
    md = ml_norm.shape[-1]
    dv = md // mh
    dqk = dv // 2
    head = at_q_norm.shape[-1]
    kv_dim = (at_w_qkv.shape[-1] - d) // 2
    group = (d // head) // (kv_dim // head)
    moe_bm = 512

    xa = (x.reshape(lat_rows, d), ctx.reshape(ctx_rows, d))
    cond = jnp.zeros((MOD_ROWS, d), F32).at[:nb].set(c).at[nb].set(c_ctx)
    lay = _even_layout(rd, lw, la, lg, mh, dqk, md)
    mods = _mod_tables(cond, mod_w, mod_b)
    w_in_bf16 = ev_w_in.astype(BF16)
    out = None

    for layer in range(depth):
        ctx_out = layer < depth - 1
        j = layer // 2
        rows_out = all_rows if ctx_out else lat_rows
        mod3 = mods[layer].reshape(MOD_ROWS, 1, 6 * d)

        h = _norm_mod(xa, norm_mix[layer], mod3, 0, all_rows, dims)[0]
        if layer % 2 == 0:
            mu_main, mu_tail = _mix_vectors(rw_mu[j], lay, lw, la)
            z_rkv, z, z_t = _even_projection(h, (w_in_bf16, j), all_rows, lay, lw, la, mh)
            r, v, kkn, lwd, km, bb, bonus, gmul = _rwkv_prep(
                z_rkv, z_t, all_rows, lay, dims, mu_main, mu_tail, rw_w0[j], _pad_rows(rw_w_up[j], lay["lwp"]).astype(BF16),
                rw_a0[j], _pad_rows(rw_a_up[j], lay["lwp"]).astype(BF16), rw_g_up[j].astype(BF16),
                rw_k_k[j], rw_k_a[j], rw_r_k[j])
            y = _rwkv_scan(r, v, kkn, lwd, km, bb, dims, heads_per_step=min(32, rd // RWKV_HEAD))
            g_off = lay["lg"] + lw
            gates = z_t[:, g_off:g_off + 4 * mh].reshape(all_rows, 2, 2 * mh)
            gates_col = jnp.moveaxis(gates, 1, 0).reshape(2, all_rows // CHUNK, CHUNK, 2 * mh)
            gates_row = jnp.swapaxes(gates_col, 2, 3)
            bias = ml_gate_b[j].reshape(2, 2 * mh)
            hm = _mlstm_scan(z, lay, dims, gates_col, gates_row, bias.reshape(2, 1, 2 * mh),
                             bias.reshape(2, 2 * mh, 1), mh, dqk, dv)
            mix = _even_finish(y, bonus, gmul, rw_gn_w[j], rw_gn_b[j], hm, z, lay, ml_norm[j],
                               rows_out, dims, mh, dv)
            xa_new = _proj_residual(mix, (ev_w_out, j), xa, mod3, 2, rows_out, dims)
        else:
            w_qkv = (at_w_qkv, j)
            cos_t, sin_t = _rope_tables(dims, head)
            q_scale = np.float32(head ** -0.5 * np.log2(np.e))
            q = _proj_qk(h, w_qkv, 0, d, at_q_norm[j], cos_t, sin_t, q_scale, lat_rows)
            k = _proj_qk(h, w_qkv, d, kv_dim, at_k_norm[j], cos_t, sin_t, np.float32(1.0), all_rows)
            v = _proj(h, w_qkv, all_rows, BF16, col_off=d + kv_dim, n_cols=kv_dim)
            att = _attention(q, k, v, dims, head, group)
            xa_new = _proj_residual(att, (at_w_o, j), xa, mod3, 2, lat_rows, dims)
            if ctx_out:
                raise NotImplementedError("context output of an attention layer")
        xa = xa_new

        h2, idx, gate_t = _norm_mod(xa, norm_ffn[layer], mod3, 3, rows_out, dims,
                                    router=(router_w, router_bias))
        gate = gate_t.T
        pos, row_token, block_expert, n_used = _dispatch_plan(idx, n_experts, moe_bm)
        xs = h2.at[row_token].get(mode="promise_in_bounds")
        ys = _moe_ffn(xs, block_expert, n_used, exp_w1, exp_w3, exp_w2, layer, moe_bm)
        f0 = ys.at[pos[:, 0]].get(mode="promise_in_bounds")
        f1 = ys.at[pos[:, 1]].get(mode="promise_in_bounds")
        last = layer == depth - 1
        xa = _ffn_residual(xa, f0, f1, gate, mod3, 5, rows_out, dims,
                           final_norm=norm_final if last else None)
        if last:
            out = xa[:lat_rows].reshape(nb, seq, d)
    return out
```

```python
import functools

import jax
import jax.numpy as jnp
import numpy as np
from jax import lax
from jax.experimental import pallas as pl
from jax.experimental.pallas import tpu as pltpu

F32 = jnp.float32
BF16 = jnp.bfloat16
HIGHEST = lax.Precision.HIGHEST

GRID_W = 64
RMS_EPS = 1e-6
GN_EPS = 64e-5
RWKV_HEAD = 64
CHUNK = 64
N_GROUPS = 4
TOP_K = 2
ROPE_THETA = 10000.0
LANES = 128
MOD_ROWS = 8
VMEM_LIMIT = 56 * 1024 * 1024


def _rup(n, m):
    return (n + m - 1) // m * m


def _tile(n, pref, quantum):
    t = min(pref, n) // quantum * quantum
    while t >= quantum:
        if n % t == 0:
            return t
        t -= quantum
    return n


def _params(*sem):
    return pltpu.CompilerParams(dimension_semantics=sem, vmem_limit_bytes=VMEM_LIMIT)


def _dot(a, b):
    return jnp.dot(a.astype(BF16), b.astype(BF16), preferred_element_type=F32)


def _dot_nt(a, b):
    return lax.dot_general(a.astype(BF16), b.astype(BF16), (((1,), (1,)), ((), ())),
                           preferred_element_type=F32)


def _dot_tn(a, b):
    return lax.dot_general(a.astype(BF16), b.astype(BF16), (((0,), (0,)), ((), ())),
                           preferred_element_type=F32)


def _split3(x):
    hi = x.astype(BF16)
    rest = x - hi.astype(F32)
    mid = rest.astype(BF16)
    return hi, mid, (rest - mid.astype(F32)).astype(BF16)


def _dot_f32(a, b):
    return jnp.dot(a, b, preferred_element_type=F32, precision=HIGHEST)


def _dot_mask(x, mask):
    m = mask.astype(BF16)
    hi, mid, lo = _split3(x)
    return (jnp.dot(hi, m, preferred_element_type=F32) + jnp.dot(mid, m, preferred_element_type=F32)
            + jnp.dot(lo, m, preferred_element_type=F32))


def _pack_bf16_pairs(x):
    half = x.shape[1] // 2
    bits = lax.bitcast_convert_type(x.astype(BF16).astype(F32), jnp.uint32)
    return (bits[:, :half] >> 16) | bits[:, half:]


def _unpack_bf16_pairs(p):
    left = lax.bitcast_convert_type(p << 16, F32)
    right = lax.bitcast_convert_type(p & jnp.uint32(0xFFFF0000), F32)
    return jnp.concatenate([left, right], axis=1).astype(BF16)


def _mod_kernel(c_ref, w_ref, b_ref, o_ref):
    c = c_ref[...]
    c = c * jax.nn.sigmoid(c)
    o_ref[...] = _dot(c, w_ref[...]) + b_ref[...]


def _mod_tables(cond, mod_w, mod_b):
    d = cond.shape[1]
    depth, _, n = mod_w.shape
    tn = _tile(n, 512, LANES)
    return pl.pallas_call(
        _mod_kernel,
        grid=(depth, n // tn),
        in_specs=[pl.BlockSpec((MOD_ROWS, d), lambda l, j: (0, 0)),
                  pl.BlockSpec((None, d, tn), lambda l, j: (l, 0, j)),
                  pl.BlockSpec((None, 1, tn), lambda l, j: (l, 0, j))],
        out_specs=pl.BlockSpec((None, MOD_ROWS, tn), lambda l, j: (l, 0, j)),
        out_shape=jax.ShapeDtypeStruct((depth, MOD_ROWS, n), F32),
        compiler_params=_params("parallel", "parallel"),
        name="mod_tables",
    )(cond, mod_w, mod_b.reshape(depth, 1, n))


def _norm_mod_kernel(x_ref, g_ref, sh_ref, sc_ref, h_ref):
    x = x_ref[...]
    y = x * lax.rsqrt(jnp.mean(x * x, axis=-1, keepdims=True) + RMS_EPS) * g_ref[...]
    h_ref[...] = (y * (1.0 + sc_ref[...]) + sh_ref[...]).astype(h_ref.dtype)


def _split_rows_specs(block, col_of, n_lat_tiles, row_axis):
    def lat(*idx):
        return (jnp.minimum(idx[row_axis], n_lat_tiles - 1), col_of(*idx))

    def ctx(*idx):
        return (jnp.maximum(idx[row_axis] - n_lat_tiles, 0), col_of(*idx))
    return [pl.BlockSpec(block, lat), pl.BlockSpec(block, ctx)]


def _norm_mod_split_kernel(xl_ref, xc_ref, g_ref, sh_ref, sc_ref, h_ref, *, n_lat_tiles):
    x = jnp.where(pl.program_id(0) < n_lat_tiles, xl_ref[...], xc_ref[...])
    y = x * lax.rsqrt(jnp.mean(x * x, axis=-1, keepdims=True) + RMS_EPS) * g_ref[...]
    h_ref[...] = (y * (1.0 + sc_ref[...]) + sh_ref[...]).astype(h_ref.dtype)


def _top2_sum(a, b, c, d):
    hi1, lo1 = jnp.maximum(a, b), jnp.minimum(a, b)
    hi2, lo2 = jnp.maximum(c, d), jnp.minimum(c, d)
    return jnp.maximum(hi1, hi2) + jnp.maximum(jnp.minimum(hi1, hi2), jnp.maximum(lo1, lo2))


def _first_argmax(vals):
    best_v = vals[0]
    best_i = jnp.zeros(vals[0].shape, jnp.int32)
    for i in range(1, len(vals)):
        better = vals[i] > best_v
        best_i = jnp.where(better, i, best_i)
        best_v = jnp.where(better, vals[i], best_v)
    return best_i, best_v


def _pick(rows, index):
    out = rows[0]
    for i in range(1, len(rows)):
        out = jnp.where(index == i, rows[i], out)
    return out


def _norm_mod_router_kernel(x_ref, g_ref, sh_ref, sc_ref, rwt_ref, rb_ref, h_ref, idx_ref, gate_ref,
                            *, n_experts):
    x = x_ref[...]
    y = x * lax.rsqrt(jnp.mean(x * x, axis=-1, keepdims=True) + RMS_EPS) * g_ref[...]
    h = y * (1.0 + sc_ref[...]) + sh_ref[...]
    h_ref[...] = _pack_bf16_pairs(h)
    logits = lax.dot_general(rwt_ref[...], h, (((1,), (1,)), ((), ())), preferred_element_type=F32,
                             precision=HIGHEST)
    aff_all = jax.nn.sigmoid(logits)
    sel_all = aff_all + rb_ref[...]
    per_group = n_experts // N_GROUPS
    aff = [aff_all[e:e + 1, :] for e in range(n_experts)]
    sel = [sel_all[e:e + 1, :] for e in range(n_experts)]
    assert per_group == 4 and TOP_K == 2
    best, _ = _first_argmax([_top2_sum(*sel[g * per_group:(g + 1) * per_group]) for g in range(N_GROUPS)])
    cand = [_pick([sel[g * per_group + i] for g in range(N_GROUPS)], best) for i in range(per_group)]
    cand_aff = [_pick([aff[g * per_group + i] for g in range(N_GROUPS)], best) for i in range(per_group)]
    i1, _ = _first_argmax(cand)
    i2, _ = _first_argmax([jnp.where(i1 == i, -jnp.inf, cand[i]) for i in range(per_group)])
    g1 = _pick(cand_aff, i1)
    g2 = _pick(cand_aff, i2)
    idx_ref[0:1, :] = best * per_group + i1
    idx_ref[1:2, :] = best * per_group + i2
    gate_ref[0:1, :] = g1 / (g1 + g2)
    gate_ref[1:2, :] = g2 / (g1 + g2)


def _norm_mod(x, g, mod3, shift_chunk, n_rows, dims, router=None):
    split = isinstance(x, tuple)
    d = x[0].shape[1] if split else x.shape[1]
    tm = _tile(dims["ctx_rows"], 256, 8)
    grp = dims["group_of"](tm)
    if split:
        x_specs = _split_rows_specs((tm, d), lambda i: 0, dims["lat_rows"] // tm, 0)
        kern = functools.partial(_norm_mod_split_kernel, n_lat_tiles=dims["lat_rows"] // tm)
    else:
        x_specs = [pl.BlockSpec((tm, d), lambda i: (i, 0))]
        kern = _norm_mod_kernel
    in_specs = x_specs + [pl.BlockSpec((1, d), lambda i: (0, 0)),
                          pl.BlockSpec((None, 1, d), lambda i: (grp(i), 0, shift_chunk)),
                          pl.BlockSpec((None, 1, d), lambda i: (grp(i), 0, shift_chunk + 1))]
    args = (list(x) if split else [x]) + [g.reshape(1, d), mod3, mod3]
    out_specs = [pl.BlockSpec((tm, d), lambda i: (i, 0))]
    out_shape = [jax.ShapeDtypeStruct((n_rows, d), BF16)]
    if router is not None:
        router_w, router_bias = router
        n_experts = router_w.shape[1]
        in_specs += [pl.BlockSpec((n_experts, d), lambda i: (0, 0)),
                     pl.BlockSpec((n_experts, 1), lambda i: (0, 0))]
        args += [router_w.T, router_bias.astype(F32).reshape(n_experts, 1)]
        out_specs = [pl.BlockSpec((tm, d // 2), lambda i: (i, 0))] + [pl.BlockSpec((TOP_K, tm), lambda i: (0, i))] * 2
        out_shape = [jax.ShapeDtypeStruct((n_rows, d // 2), jnp.uint32),
                     jax.ShapeDtypeStruct((TOP_K, n_rows), jnp.int32),
                     jax.ShapeDtypeStruct((TOP_K, n_rows), F32)]
        kern = functools.partial(_norm_mod_router_kernel, n_experts=n_experts)
    return pl.pallas_call(
        kern, grid=(n_rows // tm,), in_specs=in_specs, out_specs=out_specs, out_shape=out_shape,
        compiler_params=_params("parallel"), name="norm_mod",
    )(*args)


def _resident_bf16(w_ref, wb_ref):
    @pl.when(pl.program_id(1) == 0)
    def _():
        wb_ref[...] = w_ref[...].astype(BF16)
    return wb_ref[...]


def _proj_kernel(a_ref, w_ref, o_ref, wb_ref):
    w = _resident_bf16(w_ref, wb_ref)
    o_ref[...] = jnp.dot(a_ref[...], w, preferred_element_type=F32).astype(o_ref.dtype)


def _weight_spec(w, k, tn, col_block):
    layer = w[1]
    return pl.BlockSpec((None, k, tn), lambda j, i: (layer, 0, col_block(j)))


def _proj(a, w, n_rows, out_dtype, col_off=0, n_cols=None, tm_pref=512, tn_pref=512):
    k = a.shape[1]
    n_cols = w[0].shape[2] - col_off if n_cols is None else n_cols
    tm = _tile(n_rows, tm_pref, 8)
    tn = _tile(int(np.gcd(n_cols, col_off)) if col_off else n_cols, tn_pref, LANES)
    off = col_off // tn
    return pl.pallas_call(
        _proj_kernel,
        grid=(n_cols // tn, n_rows // tm),
        in_specs=[pl.BlockSpec((tm, k), lambda j, i: (i, 0)),
                  _weight_spec(w, k, tn, lambda j: j + off)],
        out_specs=pl.BlockSpec((tm, tn), lambda j, i: (i, j)),
        out_shape=jax.ShapeDtypeStruct((n_rows, n_cols), out_dtype),
        scratch_shapes=[pltpu.VMEM((k, tn), BF16)],
        compiler_params=_params("parallel", "arbitrary"), name="proj",
    )(a, w[0])


def _proj_cols_kernel(a_ref, *refs, plan):
    n_w = len(refs) - 2
    w_refs, o_ref, wb_ref = refs[:n_w], refs[n_w], refs[n_w + 1]

    @pl.when(pl.program_id(1) == 0)
    def _():
        blocks = [r[...] for r in w_refs]
        pieces = [jnp.zeros((wb_ref.shape[0], hi - lo), F32) if slot is None else blocks[slot][:, lo:hi]
                  for slot, lo, hi in plan]
        wb_ref[...] = jnp.concatenate(pieces, axis=1).astype(BF16)

    o_ref[...] = jnp.dot(a_ref[...], wb_ref[...], preferred_element_type=F32)


def _proj_cols(a, w, n_rows, n_tiles, block_of, plan):
    k = a.shape[1]
    tn = sum(hi - lo for _, lo, hi in plan)
    tm = _tile(n_rows, 512, 8)
    return pl.pallas_call(
        functools.partial(_proj_cols_kernel, plan=tuple(plan)),
        grid=(n_tiles, n_rows // tm),
        in_specs=[pl.BlockSpec((tm, k), lambda j, i: (i, 0))]
                 + [_weight_spec(w, k, LANES, f) for f in block_of],
        out_specs=pl.BlockSpec((tm, tn), lambda j, i: (i, j)),
        out_shape=jax.ShapeDtypeStruct((n_rows, n_tiles * tn), F32),
        scratch_shapes=[pltpu.VMEM((k, tn), BF16)],
        compiler_params=_params("parallel", "arbitrary"), name="proj_cols",
    )(a, *([w[0]] * len(block_of)))


def _column_pieces(ranges, pad_to):
    block_ids = sorted({b for s, e in ranges for b in range(s // LANES, (e - 1) // LANES + 1)})
    plan = []
    for (s, e), width in zip(ranges, pad_to):
        for b in range(s // LANES, (e - 1) // LANES + 1):
            plan.append((block_ids.index(b), max(s, b * LANES) - b * LANES, min(e, (b + 1) * LANES) - b * LANES))
        if width > e - s:
            plan.append((None, 0, width - (e - s)))
    return block_ids, plan


def _proj_res_kernel(a_ref, w_ref, x_ref, gt_ref, o_ref, wb_ref):
    w = _resident_bf16(w_ref, wb_ref)
    acc = jnp.dot(a_ref[...], w, preferred_element_type=F32)
    o_ref[...] = x_ref[...] + gt_ref[...] * acc


def _proj_res_split_kernel(a_ref, w_ref, xl_ref, xc_ref, gt_ref, o_ref, wb_ref, *, n_lat_tiles):
    w = _resident_bf16(w_ref, wb_ref)
    acc = jnp.dot(a_ref[...], w, preferred_element_type=F32)
    x = jnp.where(pl.program_id(1) < n_lat_tiles, xl_ref[...], xc_ref[...])
    o_ref[...] = x + gt_ref[...] * acc


def _proj_residual(a, w, x, mod3, gate_chunk, n_rows, dims, tn_pref=512):
    k = a.shape[1]
    d = w[0].shape[2]
    tm = _tile(dims["ctx_rows"], 512, 8)
    tn = _tile(d, tn_pref, LANES)
    grp = dims["group_of"](tm)
    gblk = gate_chunk * (d // tn)
    if isinstance(x, tuple):
        x_specs = _split_rows_specs((tm, tn), lambda j, i: j, dims["lat_rows"] // tm, 1)
        kern = functools.partial(_proj_res_split_kernel, n_lat_tiles=dims["lat_rows"] // tm)
        x_args = list(x)
    else:
        x_specs = [pl.BlockSpec((tm, tn), lambda j, i: (i, j))]
        kern = _proj_res_kernel
        x_args = [x]
    return pl.pallas_call(
        kern,
        grid=(d // tn, n_rows // tm),
        in_specs=[pl.BlockSpec((tm, k), lambda j, i: (i, 0)),
                  _weight_spec(w, k, tn, lambda j: j)] + x_specs
                 + [pl.BlockSpec((None, 1, tn), lambda j, i: (grp(i), 0, gblk + j))],
        out_specs=pl.BlockSpec((tm, tn), lambda j, i: (i, j)),
        out_shape=jax.ShapeDtypeStruct((n_rows, d), F32),
        scratch_shapes=[pltpu.VMEM((k, tn), BF16)],
        compiler_params=_params("parallel", "arbitrary"), name="proj_residual",
    )(a, w[0], *x_args, mod3)


def _proj_qk_kernel(a_ref, w_ref, nw_ref, cos_ref, sin_ref, o_ref, wb_ref, *, head, scale):
    w = _resident_bf16(w_ref, wb_ref)
    nw = nw_ref[...] * scale
    tm = a_ref.shape[0]
    rows = _tile(tm, 128, 8)
    for r in range(0, tm, rows):
        acc = jnp.dot(a_ref[r:r + rows, :], w, preferred_element_type=F32)
        cs = cos_ref[r:r + rows, :]
        sn = sin_ref[r:r + rows, :]
        for s in range(acc.shape[1] // head):
            x = acc[:, s * head:(s + 1) * head]
            xn = x * lax.rsqrt(jnp.mean(x * x, axis=-1, keepdims=True) + RMS_EPS) * nw
            xr = xn * cs + pltpu.roll(xn, head // 2, 1) * sn
            o_ref[r:r + rows, s * head:(s + 1) * head] = xr.astype(o_ref.dtype)


def _proj_qk(a, w, col_off, n_cols, norm_w, cos_t, sin_t, scale, n_rows):
    k = a.shape[1]
    head = norm_w.shape[0]
    tm = _tile(n_rows, 512, 8)
    tn = _tile(int(np.gcd(n_cols, col_off)) if col_off else n_cols, 512, head)
    off = col_off // tn
    return pl.pallas_call(
        functools.partial(_proj_qk_kernel, head=head, scale=scale),
        grid=(n_cols // tn, n_rows // tm),
        in_specs=[pl.BlockSpec((tm, k), lambda j, i: (i, 0)),
                  _weight_spec(w, k, tn, lambda j: j + off),
                  pl.BlockSpec((1, head), lambda j, i: (0, 0)),
                  pl.BlockSpec((tm, head), lambda j, i: (i, 0)),
                  pl.BlockSpec((tm, head), lambda j, i: (i, 0))],
        out_specs=pl.BlockSpec((tm, tn), lambda j, i: (i, j)),
        out_shape=jax.ShapeDtypeStruct((n_rows, n_cols), BF16),
        scratch_shapes=[pltpu.VMEM((k, tn), BF16)],
        compiler_params=_params("parallel", "arbitrary"), name="proj_qk",
    )(a, w[0], norm_w.reshape(1, head), cos_t, sin_t)


def _seg_sum(x, seg):
    n = x.shape[1]
    lane_blk = LANES if n % LANES == 0 else n
    r = lax.broadcasted_iota(jnp.int32, (lane_blk, lane_blk), 0) // seg
    c = lax.broadcasted_iota(jnp.int32, (lane_blk, lane_blk), 1) // seg
    same_seg = r == c
    parts = [_dot_mask(x[:, s:s + lane_blk], same_seg) for s in range(0, n, lane_blk)]
    return parts[0] if len(parts) == 1 else jnp.concatenate(parts, axis=1)


def _rwkv_prep_kernel(z_ref, zp_ref, zn_ref, t_ref, tp_ref, tn_ref, mu_ref, mut_ref, w0_ref, wup_ref,
                      a0_ref, aup_ref, gup_ref, kk_ref, ka_ref, rk_ref,
                      r_out, v_out, kkn_out, lw_out, km_out, bb_out, bon_out, gm_out,
                      *, rd, lg, lwp, seq, ctx, n_lat_rows):
    tm = z_ref.shape[0]
    row0 = pl.program_id(0) * tm
    in_lat = row0 < n_lat_rows
    seq_len = jnp.where(in_lat, seq, ctx)
    pos0 = jnp.where(in_lat, row0 % seq, (row0 - n_lat_rows) % ctx)
    has_prev = (pos0 != 0).astype(F32)
    has_next = (pos0 + tm != seq_len).astype(F32)

    def token_shift(cur_ref, prev_ref, next_ref, mix_ref):
        cur = cur_ref[...]
        rows = lax.broadcasted_iota(jnp.int32, cur.shape, 0)
        prev_row = prev_ref[7:8, :] * has_prev
        next_row = next_ref[0:1, :] * has_next
        before = jnp.where(rows == 0, prev_row, pltpu.roll(cur, 1, 0))
        after = jnp.where(rows == tm - 1, next_row, pltpu.roll(cur, tm - 1, 0))
        return cur + mix_ref[...] * (0.5 * (before + after) - cur)

    zs = token_shift(z_ref, zp_ref, zn_ref, mu_ref)
    ts = token_shift(t_ref, tp_ref, tn_ref, mut_ref)
    r = zs[:, 0:rd]
    k = zs[:, rd:2 * rd]
    v = zs[:, 2 * rd:3 * rd]
    g_down = ts[:, 0:lg]
    w_down = ts[:, lg:lg + lwp]
    a_down = ts[:, lg + lwp:lg + 2 * lwp]

    kk = k * kk_ref[...]
    kk = kk * lax.rsqrt(jnp.maximum(_seg_sum(kk * kk, RWKV_HEAD), 1e-24))
    r_out[...] = r
    v_out[...] = v
    kkn_out[...] = kk
    tw = jnp.tanh(w_down)
    bonus = jnp.zeros_like(r)
    for d in range(2):
        lw_out[d] = -np.float32(np.exp(-0.5)) * jax.nn.sigmoid(w0_ref[d] + _dot(tw, wup_ref[d]))
        a = jax.nn.sigmoid(a0_ref[d] + _dot(a_down, aup_ref[d]))
        k_mod = k * (1.0 + (a - 1.0) * ka_ref[...])
        km_out[d] = k_mod
        bb_out[d] = kk * a
        bonus = bonus + _seg_sum(r * k_mod * rk_ref[...], RWKV_HEAD) * v
    bon_out[...] = bonus
    gm_out[...] = _dot(jax.nn.sigmoid(g_down), gup_ref[...])


def _rwkv_prep(z_rkv, z, n_rows, lay, dims, mu_main, mu_tail, w0, w_up_p, a0, a_up_p, g_up, k_k, k_a, r_k):
    rd, lg, lwp = lay["rd"], lay["lg"], lay["lwp"]
    mw = 3 * rd
    tw = lg + 2 * lwp
    tblk = 0
    tm = _tile(dims["ctx"], 64, 8)
    nb8 = n_rows // 8
    full = lambda shape: pl.BlockSpec(shape, lambda i: (0,) * len(shape))
    row_spec = pl.BlockSpec((tm, rd), lambda i: (i, 0))
    dir_spec = pl.BlockSpec((2, tm, rd), lambda i: (0, i, 0))
    sds = jax.ShapeDtypeStruct
    kern = functools.partial(_rwkv_prep_kernel, rd=rd, lg=lg, lwp=lwp, seq=dims["seq"], ctx=dims["ctx"],
                             n_lat_rows=dims["lat_rows"])
    return pl.pallas_call(
        kern,
        grid=(n_rows // tm,),
        in_specs=[pl.BlockSpec((tm, mw), lambda i: (i, 0)),
                  pl.BlockSpec((8, mw), lambda i: (jnp.maximum(i * (tm // 8) - 1, 0), 0)),
                  pl.BlockSpec((8, mw), lambda i: (jnp.minimum((i + 1) * (tm // 8), nb8 - 1), 0)),
                  pl.BlockSpec((tm, tw), lambda i: (i, tblk)),
                  pl.BlockSpec((8, tw), lambda i: (jnp.maximum(i * (tm // 8) - 1, 0), tblk)),
                  pl.BlockSpec((8, tw), lambda i: (jnp.minimum((i + 1) * (tm // 8), nb8 - 1), tblk)),
                  full((1, mw)), full((1, tw)), full((2, 1, rd)), full((2, lwp, rd)), full((2, 1, rd)), full((2, lwp, rd)),
                  full((lg, rd)), full((1, rd)), full((1, rd)), full((1, rd))],
        out_specs=[row_spec, row_spec, row_spec, dir_spec, dir_spec, dir_spec, row_spec, row_spec],
        out_shape=[sds((n_rows, rd), F32)] * 3 + [sds((2, n_rows, rd), F32)] * 3 + [sds((n_rows, rd), F32)] * 2,
        compiler_params=_params("parallel"), name="rwkv_prep",
    )(z_rkv, z_rkv, z_rkv, z, z, z, mu_main, mu_tail, w0.reshape(2, 1, rd), w_up_p, a0.reshape(2, 1, rd), a_up_p, g_up,
      k_k.reshape(1, rd), k_a.reshape(1, rd), r_k.reshape(1, rd))


def _rwkv_scan_kernel(r_ref, v_ref, kk_ref, lw_ref, km_ref, bb_ref, y_ref, s_ref, *, heads):
    d = pl.program_id(0)
    j = pl.program_id(3)

    @pl.when(j == 0)
    def _():
        s_ref[...] = jnp.zeros_like(s_ref)

    n = RWKV_HEAD
    chunk = lw_ref.shape[0]
    t_idx = lax.broadcasted_iota(jnp.int32, (chunk, chunk), 0)
    s_idx = lax.broadcasted_iota(jnp.int32, (chunk, chunk), 1)
    lead = jnp.where(d == 0, t_idx - s_idx, s_idx - t_idx)
    incl = lead >= 0
    assert chunk & (chunk - 1) == 0

    lw = lw_ref[...]
    cum = _dot_f32(incl.astype(F32), lw)
    tot = jnp.sum(lw, axis=0, keepdims=True)
    e_in = jnp.exp(cum)
    e_neg = jnp.exp(-cum)
    e_last = jnp.exp(tot - cum)
    kkn = kk_ref[...]
    km = km_ref[...]
    bb = bb_ref[...]
    r_t = r_ref[...] * e_in
    a_t = -kkn * jnp.exp(cum - lw)
    b_t = bb * e_neg
    k_t = km * e_neg
    b_l = bb * e_last
    k_l = km * e_last
    w_l = jnp.exp(tot)
    v_all = v_ref[...]

    hs = range(heads)
    sl = [slice(h * n, (h + 1) * n) for h in hs]
    t2 = lax.broadcasted_iota(jnp.int32, (2 * chunk, 2 * chunk), 0)
    s2 = lax.broadcasted_iota(jnp.int32, (2 * chunk, 2 * chunk), 1)
    tt = jnp.where(t2 >= chunk, t2 - chunk, t2)
    ss = jnp.where(s2 >= chunk, s2 - chunk, s2)
    keep = jnp.where(d == 0, tt - ss, ss - tt) >= jnp.where(t2 < chunk, 1, 0)
    right = lax.broadcasted_iota(jnp.int32, (chunk, 2 * chunk), 1) >= chunk
    eye_right = (lax.broadcasted_iota(jnp.int32, (chunk, 2 * chunk), 1)
                 == lax.broadcasted_iota(jnp.int32, (chunk, 2 * chunk), 0) + chunk).astype(F32)
    zeros_v = jnp.zeros((chunk, n), F32)
    stack = lambda top, bottom: jnp.concatenate([top, bottom], axis=0)

    s0 = [s_ref[h] for h in hs]
    v = [v_all[:, sl[h]] for h in hs]
    pair = [jnp.where(keep, _dot_nt(stack(a_t[:, sl[h]], r_t[:, sl[h]]), stack(b_t[:, sl[h]], k_t[:, sl[h]])), 0.0)
            for h in hs]
    a_side = [pair[h][:chunk] for h in hs]
    r_side = [pair[h][chunk:] for h in hs]
    x = [_dot_nt(a_t[:, sl[h]], s0[h]) + _dot(a_side[h], stack(zeros_v, v[h])) for h in hs]
    y0 = [_dot_nt(r_t[:, sl[h]], s0[h]) for h in hs]
    q = [jnp.where(right, eye_right, a_side[h]) for h in hs]
    for _ in range(chunk.bit_length() - 1):
        q = [_dot(q[h][:, :chunk], q[h]) + jnp.where(right, q[h], 0.0) for h in hs]
    uv = [stack(_dot(q[h][:, chunk:], x[h]), v[h]) for h in hs]
    y_ref[...] = jnp.concatenate([y0[h] + _dot(r_side[h], uv[h]) for h in hs], axis=1)
    for h in hs:
        s_ref[h] = s0[h] * w_l[:, sl[h]] + _dot_tn(uv[h], stack(b_l[:, sl[h]], k_l[:, sl[h]]))


def _chunk_index(d, b, j, dims):
    nc_ctx, nc_lat, nb = dims["ctx"] // CHUNK, dims["seq"] // CHUNK, dims["batch"]
    jc = jnp.where(d == 0, j, nc_ctx - 1 - j)
    jl = jnp.where(d == 0, j - nc_ctx, nc_lat - 1 - (j - nc_ctx))
    return jnp.where(j < nc_ctx, nb * nc_lat + b * nc_ctx + jc, b * nc_lat + jl)


def _rwkv_scan(r, v, kkn, lw, km, bb, dims, heads_per_step):
    n_rows, rd = r.shape
    gw = heads_per_step * RWKV_HEAD
    n_chunks = (dims["ctx"] + dims["seq"]) // CHUNK
    cidx = lambda d, b, g, j: _chunk_index(d, b, j, dims)
    row_spec = pl.BlockSpec((CHUNK, gw), lambda d, b, g, j: (cidx(d, b, g, j), g))
    dir_spec = pl.BlockSpec((None, CHUNK, gw), lambda d, b, g, j: (d, cidx(d, b, g, j), g))
    return pl.pallas_call(
        functools.partial(_rwkv_scan_kernel, heads=heads_per_step),
        grid=(2, dims["batch"], rd // gw, n_chunks),
        in_specs=[row_spec, row_spec, row_spec, dir_spec, dir_spec, dir_spec],
        out_specs=dir_spec,
        out_shape=jax.ShapeDtypeStruct((2, n_rows, rd), F32),
        scratch_shapes=[pltpu.VMEM((heads_per_step, RWKV_HEAD, RWKV_HEAD), F32)],
        compiler_params=_params("parallel", "parallel", "parallel", "arbitrary"), name="rwkv_scan",
    )(r, v, kkn, lw, km, bb)


def _mlstm_kernel(q_ref, k_ref, v_ref, gc_ref, gr_ref, bc_ref, br_ref, h_ref, c_ref, n_ref, m_ref,
                  *, heads, dqk, dv):
    d = pl.program_id(0)
    j = pl.program_id(2)

    @pl.when(j == 0)
    def _():
        c_ref[...] = jnp.zeros_like(c_ref)
        n_ref[...] = jnp.zeros_like(n_ref)
        m_ref[...] = jnp.zeros_like(m_ref)

    chunk = q_ref.shape[0]
    t_idx = lax.broadcasted_iota(jnp.int32, (chunk, chunk), 0)
    s_idx = lax.broadcasted_iota(jnp.int32, (chunk, chunk), 1)
    lead = jnp.where(d == 0, t_idx - s_idx, s_idx - t_idx)
    incl = lead >= 0
    tri = incl.astype(F32)
    tri_t = (lead <= 0).astype(F32)

    gcol = gc_ref[...] + bc_ref[...]
    grow = gr_ref[...] + br_ref[...]
    i_col = gcol[:, :heads]
    f_col = jax.nn.log_sigmoid(gcol[:, heads:])
    i_row = grow[:heads, :]
    f_row = jax.nn.log_sigmoid(grow[heads:, :])
    b_col = _dot_f32(tri, f_col)
    b_row = _dot_f32(f_row, tri_t)
    b_last = jnp.sum(f_col, axis=0, keepdims=True)
    scale = np.float32(dqk ** -0.5)

    hs = range(heads)
    q = [q_ref[:, h * dqk:(h + 1) * dqk] * scale for h in hs]
    k = [k_ref[:, h * dqk:(h + 1) * dqk] for h in hs]
    v = [v_ref[:, h * dv:(h + 1) * dv] for h in hs]
    c_st = [c_ref[h] for h in hs]
    n_st = [n_ref[h] for h in hs]
    m_st = [m_ref[h][:, 0:1] for h in hs]
    bc = [b_col[:, h:h + 1] for h in hs]
    qk = [_dot_nt(q[h], k[h]) for h in hs]
    qc = [_dot(q[h], c_st[h]) for h in hs]
    dmat = [jnp.where(incl, bc[h] + (i_row[h:h + 1, :] - b_row[h:h + 1, :]), -jnp.inf) for h in hs]
    inter = [bc[h] + m_st[h] for h in hs]
    m_t = [jnp.maximum(inter[h], jnp.max(dmat[h], axis=-1, keepdims=True)) for h in hs]
    w_inter = [jnp.exp(inter[h] - m_t[h]) for h in hs]
    s = [qk[h] * jnp.exp(dmat[h] - m_t[h]) for h in hs]
    num = [w_inter[h] * qc[h] + _dot(s[h], v[h]) for h in hs]
    den = [w_inter[h] * jnp.sum(q[h] * n_st[h], axis=-1, keepdims=True) + jnp.sum(s[h], axis=-1, keepdims=True)
           for h in hs]
    h_ref[...] = jnp.concatenate([num[h] / jnp.maximum(jnp.abs(den[h]), jnp.exp(-m_t[h])) for h in hs], axis=1)
    bl = [b_last[:, h:h + 1] for h in hs]
    g = [bl[h] - bc[h] + i_col[:, h:h + 1] for h in hs]
    m_new = [jnp.maximum(bl[h] + m_st[h], jnp.max(g[h], axis=0, keepdims=True)) for h in hs]
    decay = [jnp.exp(bl[h] + m_st[h] - m_new[h]) for h in hs]
    kw = [k[h] * jnp.exp(g[h] - m_new[h]) for h in hs]
    for h in hs:
        c_ref[h] = decay[h] * c_st[h] + _dot_tn(kw[h], v[h])
        n_ref[h] = decay[h] * n_st[h] + jnp.sum(kw[h], axis=0, keepdims=True)
        m_ref[h] = jnp.broadcast_to(m_new[h], m_ref.shape[1:])


def _mlstm_scan(z, lay, dims, gates_col, gates_row, bias_col, bias_row, heads, dqk, dv):
    n_rows = z.shape[0]
    n_chunks = (dims["ctx"] + dims["seq"]) // CHUNK
    md = heads * dv
    qw = heads * dqk
    cidx = lambda d, b, j: _chunk_index(d, b, j, dims)
    q_blk, k_blk, v_blk = lay["mq"] // qw, lay["mk"] // qw, lay["mv"] // md
    return pl.pallas_call(
        functools.partial(_mlstm_kernel, heads=heads, dqk=dqk, dv=dv),
        grid=(2, dims["batch"], n_chunks),
        in_specs=[pl.BlockSpec((CHUNK, qw), lambda d, b, j: (cidx(d, b, j), q_blk)),
                  pl.BlockSpec((CHUNK, qw), lambda d, b, j: (cidx(d, b, j), k_blk)),
                  pl.BlockSpec((CHUNK, md), lambda d, b, j: (cidx(d, b, j), v_blk)),
                  pl.BlockSpec((None, None, CHUNK, 2 * heads), lambda d, b, j: (d, cidx(d, b, j), 0, 0)),
                  pl.BlockSpec((None, None, 2 * heads, CHUNK), lambda d, b, j: (d, cidx(d, b, j), 0, 0)),
                  pl.BlockSpec((None, 1, 2 * heads), lambda d, b, j: (d, 0, 0)),
                  pl.BlockSpec((None, 2 * heads, 1), lambda d, b, j: (d, 0, 0))],
        out_specs=pl.BlockSpec((None, CHUNK, md), lambda d, b, j: (d, cidx(d, b, j), 0)),
        out_shape=jax.ShapeDtypeStruct((2, n_rows, md), F32),
        scratch_shapes=[pltpu.VMEM((heads, dqk, dv), F32), pltpu.VMEM((heads, 1, dqk), F32),
                        pltpu.VMEM((heads, 1, LANES), F32)],
        compiler_params=_params("parallel", "parallel", "arbitrary"), name="mlstm_scan",
    )(z, z, z, gates_col, gates_row, bias_col, bias_row)


def _even_finish_kernel(y_ref, bon_ref, gm_ref, gnw_ref, gnb_ref, h_ref, o_ref, nw_ref, out_ref,
                        *, rd, heads, dv):
    y = y_ref[0] + y_ref[1]
    inv_n = np.float32(1.0 / RWKV_HEAD)
    mean = _seg_sum(y, RWKV_HEAD) * inv_n
    yc = y - mean
    var = _seg_sum(yc * yc, RWKV_HEAD) * inv_n
    yn = yc * lax.rsqrt(var + GN_EPS) * gnw_ref[...] + gnb_ref[...] + bon_ref[...]
    out_ref[:, 0:rd] = (yn * gm_ref[...]).astype(out_ref.dtype)
    hm = h_ref[0] + h_ref[1]
    for h in range(heads):
        sl = slice(h * dv, (h + 1) * dv)
        x = hm[:, sl]
        xn = x * lax.rsqrt(jnp.mean(x * x, axis=-1, keepdims=True) + RMS_EPS) * nw_ref[:, sl]
        out_ref[:, rd + h * dv:rd + (h + 1) * dv] = (jax.nn.sigmoid(o_ref[:, sl]) * xn).astype(out_ref.dtype)


def _even_finish(y, bonus, gmul, gn_w, gn_b, hm, z, lay, ml_norm, n_rows, dims, heads, dv):
    rd = y.shape[2]
    md = hm.shape[2]
    tm = _tile(dims["ctx_rows"], 256, 8)
    o_blk = lay["mo"] // md
    return pl.pallas_call(
        functools.partial(_even_finish_kernel, rd=rd, heads=heads, dv=dv),
        grid=(n_rows // tm,),
        in_specs=[pl.BlockSpec((2, tm, rd), lambda i: (0, i, 0)),
                  pl.BlockSpec((tm, rd), lambda i: (i, 0)),
                  pl.BlockSpec((tm, rd), lambda i: (i, 0)),
                  pl.BlockSpec((1, rd), lambda i: (0, 0)),
                  pl.BlockSpec((1, rd), lambda i: (0, 0)),
                  pl.BlockSpec((2, tm, md), lambda i: (0, i, 0)),
                  pl.BlockSpec((tm, md), lambda i: (i, o_blk)),
                  pl.BlockSpec((1, md), lambda i: (0, 0))],
        out_specs=pl.BlockSpec((tm, rd + md), lambda i: (i, 0)),
        out_shape=jax.ShapeDtypeStruct((n_rows, rd + md), BF16),
        compiler_params=_params("parallel"), name="even_finish",
    )(y, bonus, gmul, gn_w.reshape(1, rd), gn_b.reshape(1, rd), hm, z, ml_norm.reshape(1, md))


def _attn_kernel(q_ref, kl_ref, kc_ref, vl_ref, vc_ref, o_ref, *, group, head):
    tq = q_ref.shape[0]
    q = jnp.concatenate([q_ref[:, g * head:(g + 1) * head] for g in range(group)], axis=0)
    ck = _tile(kl_ref.shape[0], 256, 8)
    keys = [kc_ref[...]] + [kl_ref[c * ck:(c + 1) * ck, :] for c in range(kl_ref.shape[0] // ck)]
    vals = [vc_ref[...]] + [vl_ref[c * ck:(c + 1) * ck, :] for c in range(kl_ref.shape[0] // ck)]
    m = acc = None
    for kc, vc in zip(keys, vals):
        s = _dot_nt(q, kc)
        row_max = jnp.max(s, axis=-1, keepdims=True)
        v_aug = jnp.concatenate([vc, jnp.ones_like(vc)], axis=1)
        if m is None:
            m = row_max
            acc = _dot(jnp.exp2(s - m), v_aug)
        else:
            m_new = jnp.maximum(m, row_max)
            acc = acc * jnp.exp2(m - m_new) + _dot(jnp.exp2(s - m_new), v_aug)
            m = m_new
    o = acc[:, :head] / acc[:, head:head + 1]
    for g in range(group):
        o_ref[:, g * head:(g + 1) * head] = o[g * tq:(g + 1) * tq].astype(o_ref.dtype)


def _attention(q, k, v, dims, head, group):
    seq, ctx, nb = dims["seq"], dims["ctx"], dims["batch"]
    kvh = k.shape[1] // head
    tq = _tile(seq, 256, 8)
    nq = seq // tq
    gw = group * head
    return pl.pallas_call(
        functools.partial(_attn_kernel, group=group, head=head),
        grid=(nb, kvh, nq),
        in_specs=[pl.BlockSpec((tq, gw), lambda b, h, i: (b * nq + i, h)),
                  pl.BlockSpec((seq, head), lambda b, h, i: (b, h)),
                  pl.BlockSpec((ctx, head), lambda b, h, i: (nb * (seq // ctx) + b, h)),
                  pl.BlockSpec((seq, head), lambda b, h, i: (b, h)),
                  pl.BlockSpec((ctx, head), lambda b, h, i: (nb * (seq // ctx) + b, h))],
        out_specs=pl.BlockSpec((tq, gw), lambda b, h, i: (b * nq + i, h)),
        out_shape=jax.ShapeDtypeStruct(q.shape, BF16),
        compiler_params=_params("parallel", "parallel", "parallel"), name="attention",
    )(q, k, k, v, v)


def _new_expert(i, be_ref):
    return jnp.logical_or(i == 0, be_ref[i] != be_ref[jnp.maximum(i - 1, 0)])


def _moe_up_kernel(be_ref, nu_ref, x_ref, w1_ref, w3_ref, mid_ref, w1_bf, w3_bf):
    i = pl.program_id(1)

    @pl.when(_new_expert(i, be_ref))
    def _():
        w1_bf[...] = w1_ref[...].astype(BF16)
        w3_bf[...] = w3_ref[...].astype(BF16)

    @pl.when(i < nu_ref[0])
    def _():
        x = _unpack_bf16_pairs(x_ref[...])
        a = jnp.dot(x, w1_bf[...], preferred_element_type=F32)
        b = jnp.dot(x, w3_bf[...], preferred_element_type=F32)
        mid_ref[...] = (a * jax.nn.sigmoid(a) * b).astype(mid_ref.dtype)

    @pl.when(i >= nu_ref[0])
    def _():
        mid_ref[...] = jnp.zeros_like(mid_ref)


def _moe_down_kernel(be_ref, nu_ref, mid_ref, w2_ref, o_ref, w2_bf):
    i = pl.program_id(1)

    @pl.when(_new_expert(i, be_ref))
    def _():
        w2_bf[...] = w2_ref[...].astype(BF16)

    @pl.when(i < nu_ref[0])
    def _():
        o_ref[...] = jnp.dot(mid_ref[...], w2_bf[...], preferred_element_type=F32)

    @pl.when(i >= nu_ref[0])
    def _():
        o_ref[...] = jnp.zeros_like(o_ref)


def _moe_ffn(xs, block_expert, n_used, w1, w3, w2, layer, bm):
    n_rows = xs.shape[0]
    d, de = w1.shape[2], w1.shape[3]
    tde = _tile(de, 512, LANES)
    tn = _tile(d, 4096, LANES)
    nblk = n_rows // bm
    mid = pl.pallas_call(
        _moe_up_kernel,
        grid_spec=pltpu.PrefetchScalarGridSpec(
            num_scalar_prefetch=2, grid=(de // tde, nblk),
            in_specs=[pl.BlockSpec((bm, d // 2), lambda k, i, be, nu: (i, 0)),
                      pl.BlockSpec((None, None, d, tde), lambda k, i, be, nu: (layer, be[i], 0, k)),
                      pl.BlockSpec((None, None, d, tde), lambda k, i, be, nu: (layer, be[i], 0, k))],
            out_specs=pl.BlockSpec((bm, tde), lambda k, i, be, nu: (i, k)),
            scratch_shapes=[pltpu.VMEM((d, tde), BF16), pltpu.VMEM((d, tde), BF16)]),
        out_shape=jax.ShapeDtypeStruct((n_rows, de), BF16),
        compiler_params=_params("arbitrary", "arbitrary"), name="moe_up",
    )(block_expert, n_used, xs, w1, w3)
    return pl.pallas_call(
        _moe_down_kernel,
        grid_spec=pltpu.PrefetchScalarGridSpec(
            num_scalar_prefetch=2, grid=(d // tn, nblk),
            in_specs=[pl.BlockSpec((bm, de), lambda n, i, be, nu: (i, 0)),
                      pl.BlockSpec((None, None, de, tn), lambda n, i, be, nu: (layer, be[i], 0, n))],
            out_specs=pl.BlockSpec((bm, tn), lambda n, i, be, nu: (i, n)),
            scratch_shapes=[pltpu.VMEM((de, tn), BF16)]),
        out_shape=jax.ShapeDtypeStruct((n_rows, d), F32),
        compiler_params=_params("arbitrary", "arbitrary"), name="moe_down",
    )(block_expert, n_used, mid, w2)


def _dispatch_plan(idx, n_experts, bm):
    n_tok = idx.shape[1]
    n_assign = n_tok * TOP_K
    expert = idx.T.reshape(-1)
    onehot = (expert[:, None] == jnp.arange(n_experts)[None, :]).astype(jnp.int32)
    rank = jnp.take_along_axis(jnp.cumsum(onehot, axis=0) - onehot, expert[:, None], 1)[:, 0]
    counts = jnp.sum(onehot, axis=0)
    padded = (counts + bm - 1) // bm * bm
    pad_end = jnp.cumsum(padded)
    pos = (pad_end - padded)[expert] + rank
    n_blocks = -(-n_assign // bm) + n_experts
    token = jnp.repeat(jnp.arange(n_tok, dtype=jnp.int32), TOP_K)
    filler = jnp.arange(n_blocks * bm, dtype=jnp.int32) % n_tok
    row_token = filler.at[pos].set(token)
    block_start = jnp.arange(n_blocks, dtype=jnp.int32) * bm
    block_expert = jnp.minimum(jnp.sum((pad_end[None, :] <= block_start[:, None]).astype(jnp.int32), axis=1),
                               n_experts - 1)
    n_used = (pad_end[-1] // bm).astype(jnp.int32).reshape(1)
    block_expert = jnp.where(jnp.arange(n_blocks) < n_used[0], block_expert,
                             block_expert[jnp.maximum(n_used[0] - 1, 0)])
    return pos.reshape(n_tok, TOP_K), row_token, block_expert, n_used


def _ffn_res_kernel(x_ref, f0_ref, f1_ref, g_ref, gt_ref, o_ref):
    g = g_ref[...]
    f = g[:, 0:1] * f0_ref[...] + g[:, 1:2] * f1_ref[...]
    o_ref[...] = x_ref[...] + gt_ref[...] * f


def _ffn_res_norm_kernel(x_ref, f0_ref, f1_ref, g_ref, gt_ref, nw_ref, o_ref):
    g = g_ref[...]
    f = g[:, 0:1] * f0_ref[...] + g[:, 1:2] * f1_ref[...]
    x = x_ref[...] + gt_ref[...] * f
    o_ref[...] = x * lax.rsqrt(jnp.mean(x * x, axis=-1, keepdims=True) + RMS_EPS) * nw_ref[...]


def _ffn_residual(x, f0, f1, gate, mod3, gate_chunk, n_rows, dims, final_norm=None):
    d = x.shape[1]
    tm = _tile(dims["ctx_rows"], 256, 8)
    grp = dims["group_of"](tm)
    row = pl.BlockSpec((tm, d), lambda i: (i, 0))
    in_specs = [row, row, row, pl.BlockSpec((tm, TOP_K), lambda i: (i, 0)),
                pl.BlockSpec((None, 1, d), lambda i: (grp(i), 0, gate_chunk))]
    args = [x, f0, f1, gate, mod3]
    kern = _ffn_res_kernel
    if final_norm is not None:
        in_specs.append(pl.BlockSpec((1, d), lambda i: (0, 0)))
        args.append(final_norm.reshape(1, d))
        kern = _ffn_res_norm_kernel
    return pl.pallas_call(
        kern, grid=(n_rows // tm,), in_specs=in_specs, out_specs=row,
        out_shape=jax.ShapeDtypeStruct((n_rows, d), F32),
        compiler_params=_params("parallel"), name="ffn_residual",
    )(*args)


def _even_layout(rd, lw, la, lg, mh, dqk, md):
    lwp = _rup(lw + 4 * mh, LANES)
    assert _rup(la, LANES) == lwp
    lay = {"rd": rd, "lg": lg, "lwp": lwp}
    off = 0
    for name, width in (("mq", mh * dqk), ("mk", mh * dqk), ("mv", md), ("mo", md)):
        assert off % width == 0
        lay[name] = off
        off += width
    lay["width"] = off
    return lay


def _pad_cols(seg, width):
    return jnp.pad(seg, ((0, 0), (0, width - seg.shape[1])))


def _even_projection(h, w_in, n_rows, lay, lw, la, mh):
    rd, lg, lwp = lay["rd"], lay["lg"], lay["lwp"]
    o = 3 * rd
    rww = o + lw + la + lg
    m_end = rww + lay["width"]
    z_rkv = _proj(h, w_in, n_rows, F32, n_cols=o)
    tn = _tile(lay["width"], 512, LANES)
    shift, base, per_tile = rww % LANES, rww // LANES, tn // LANES
    if shift:
        plan = [(0, shift, LANES)] + [(q, 0, LANES) for q in range(1, per_tile)] + [(per_tile, 0, shift)]
    else:
        plan = [(q, 0, LANES) for q in range(per_tile)]
    slots = per_tile + (1 if shift else 0)
    z_m = _proj_cols(h, w_in, n_rows, lay["width"] // tn,
                     [lambda j, q=q: base + per_tile * j + q for q in range(slots)], plan)
    ids, plan_t = _column_pieces([(o + lw + la, rww), (o, o + lw), (m_end, m_end + 4 * mh), (o + lw, o + lw + la)],
                                 [lg, lw, lwp - lw, lwp])
    z_t = _proj_cols(h, w_in, n_rows, 1, [lambda j, b=b: b for b in ids], plan_t)
    return z_rkv, z_m, z_t


def _mix_vectors(mu, lay, lw, la):
    rd, lg, lwp = lay["rd"], lay["lg"], lay["lwp"]
    o = 3 * rd
    rww = o + lw + la + lg
    mu2 = mu.reshape(1, -1)
    mu_tail = jnp.concatenate([mu2[:, o + lw + la:rww], _pad_cols(mu2[:, o:o + lw], lwp),
                               _pad_cols(mu2[:, o + lw:o + lw + la], lwp)], axis=1)
    return mu2[:, :o], mu_tail


def _pad_rows(w, rows):
    return jnp.pad(w, ((0, 0), (0, rows - w.shape[1]), (0, 0)))


def _rope_tables(dims, head):
    seq, nb = dims["seq"], dims["batch"]
    pairs = head // 4
    rows = seq // GRID_W
    row = np.repeat(np.arange(rows), GRID_W).astype(np.float32)
    col = np.tile(np.arange(GRID_W), rows).astype(np.float32)
    inv = (np.float32(ROPE_THETA) ** (-np.arange(pairs, dtype=np.float32) / np.float32(pairs))).astype(np.float32)
    ang = np.concatenate([row[:, None] * inv, col[:, None] * inv], -1)
    cos, sin = np.cos(ang).astype(np.float32), np.sin(ang).astype(np.float32)
    cos_t = np.tile(np.concatenate([cos, cos], -1), (nb, 1))
    sin_t = np.tile(np.concatenate([-sin, sin], -1), (nb, 1))
    n_ctx = dims["ctx_rows"]
    return (jnp.asarray(np.concatenate([cos_t, np.ones((n_ctx, head), np.float32)], 0)),
            jnp.asarray(np.concatenate([sin_t, np.zeros((n_ctx, head), np.float32)], 0)))


def kernel(x, c, ctx, c_ctx, mod_w, mod_b, norm_mix, norm_ffn, norm_final, router_w, router_bias,
           exp_w1, exp_w3, exp_w2, ev_w_in, ev_w_out, rw_mu, rw_w0, rw_w_up, rw_a0, rw_a_up, rw_g_up,
           rw_k_k, rw_k_a, rw_r_k, rw_gn_w, rw_gn_b, ml_gate_b, ml_norm, at_w_qkv, at_q_norm, at_k_norm,
           at_w_o):
    nb, seq, d = x.shape
    n_ctx = ctx.shape[1]
    depth = mod_w.shape[0]
    lat_rows, ctx_rows = nb * seq, nb * n_ctx
    all_rows = lat_rows + ctx_rows
    assert nb + 1 <= MOD_ROWS and seq % n_ctx == 0 and n_ctx % CHUNK == 0
    dims = {"batch": nb, "seq": seq, "ctx": n_ctx, "lat_rows": lat_rows, "ctx_rows": ctx_rows,
            "group_of": lambda tm: (lambda i: jnp.minimum(i * tm // seq, nb))}

    n_experts = router_w.shape[1]
    rd = rw_w0.shape[-1]
    lw, la, lg = rw_w_up.shape[2], rw_a_up.shape[2], rw_g_up.shape[1]
    mh = ml_gate_b.shape[-1]
    md = ml_norm.shape[-1]
    dv = md // mh
    dqk = dv // 2
    head = at_q_norm.shape[-1]
    kv_dim = (at_w_qkv.shape[-1] - d) // 2
    group = (d // head) // (kv_dim // head)
    moe_bm = 256

    xa = (x.reshape(lat_rows, d), ctx.reshape(ctx_rows, d))
    cond = jnp.zeros((MOD_ROWS, d), F32).at[:nb].set(c).at[nb].set(c_ctx)
    lay = _even_layout(rd, lw, la, lg, mh, dqk, md)
    mods = _mod_tables(cond, mod_w, mod_b)
    w_in_bf16 = ev_w_in.astype(BF16)
    out = None

    for layer in range(depth):
        ctx_out = layer < depth - 1
        j = layer // 2
        rows_out = all_rows if ctx_out else lat_rows
        mod3 = mods[layer].reshape(MOD_ROWS, 1, 6 * d)

        h = _norm_mod(xa, norm_mix[layer], mod3, 0, all_rows, dims)[0]
        if layer % 2 == 0:
            mu_main, mu_tail = _mix_vectors(rw_mu[j], lay, lw, la)
            z_rkv, z, z_t = _even_projection(h, (w_in_bf16, j), all_rows, lay, lw, la, mh)
            r, v, kkn, lwd, km, bb, bonus, gmul = _rwkv_prep(
                z_rkv, z_t, all_rows, lay, dims, mu_main, mu_tail, rw_w0[j], _pad_rows(rw_w_up[j], lay["lwp"]).astype(BF16),
                rw_a0[j], _pad_rows(rw_a_up[j], lay["lwp"]).astype(BF16), rw_g_up[j].astype(BF16),
                rw_k_k[j], rw_k_a[j], rw_r_k[j])
            y = _rwkv_scan(r, v, kkn, lwd, km, bb, dims, heads_per_step=min(32, rd // RWKV_HEAD))
            g_off = lay["lg"] + lw
            gates = z_t[:, g_off:g_off + 4 * mh].reshape(all_rows, 2, 2 * mh)
            gates_col = jnp.moveaxis(gates, 1, 0).reshape(2, all_rows // CHUNK, CHUNK, 2 * mh)
            gates_row = jnp.swapaxes(gates_col, 2, 3)
            bias = ml_gate_b[j].reshape(2, 2 * mh)
            hm = _mlstm_scan(z, lay, dims, gates_col, gates_row, bias.reshape(2, 1, 2 * mh),
                             bias.reshape(2, 2 * mh, 1), mh, dqk, dv)
            mix = _even_finish(y, bonus, gmul, rw_gn_w[j], rw_gn_b[j], hm, z, lay, ml_norm[j],
                               rows_out, dims, mh, dv)
            xa_new = _proj_residual(mix, (ev_w_out, j), xa, mod3, 2, rows_out, dims)
        else:
            if ctx_out:
                raise NotImplementedError("context output of an attention layer (depth > 2)")
            w_qkv = (at_w_qkv, j)
            cos_t, sin_t = _rope_tables(dims, head)
            q_scale = np.float32(head ** -0.5 * np.log2(np.e))
            q = _proj_qk(h, w_qkv, 0, d, at_q_norm[j], cos_t, sin_t, q_scale, lat_rows)
            k = _proj_qk(h, w_qkv, d, kv_dim, at_k_norm[j], cos_t, sin_t, np.float32(1.0), all_rows)
            v = _proj(h, w_qkv, all_rows, BF16, col_off=d + kv_dim, n_cols=kv_dim)
            att = _attention(q, k, v, dims, head, group)
            xa_new = _proj_residual(att, (at_w_o, j), xa, mod3, 2, lat_rows, dims)
        xa = xa_new

        h2, idx, gate_t = _norm_mod(xa, norm_ffn[layer], mod3, 3, rows_out, dims,
                                    router=(router_w, router_bias))
        gate = gate_t.T
        pos, row_token, block_expert, n_used = _dispatch_plan(idx, n_experts, moe_bm)
        xs = h2.at[row_token].get(mode="promise_in_bounds")
        ys = _moe_ffn(xs, block_expert, n_used, exp_w1, exp_w3, exp_w2, layer, moe_bm)
        f0 = ys.at[pos[:, 0]].get(mode="promise_in_bounds")
        f1 = ys.at[pos[:, 1]].get(mode="promise_in_bounds")
        last = layer == depth - 1
        xa = _ffn_residual(xa, f0, f1, gate, mod3, 5, rows_out, dims,
                           final_norm=norm_final if last else None)
        if last:
            out = xa[:lat_rows].reshape(nb, seq, d)
    return out
```

```python
import functools

import jax
import jax.numpy as jnp
import numpy as np
from jax import lax
from jax.experimental import pallas as pl
from jax.experimental.pallas import tpu as pltpu

F32 = jnp.float32
BF16 = jnp.bfloat16
HIGHEST = lax.Precision.HIGHEST

GRID_W = 64
RMS_EPS = 1e-6
GN_EPS = 64e-5
RWKV_HEAD = 64
CHUNK = 64
N_GROUPS = 4
TOP_K = 2
ROPE_THETA = 10000.0
LANES = 128
MOD_ROWS = 8
VMEM_LIMIT = 56 * 1024 * 1024


def _rup(n, m):
    return (n + m - 1) // m * m


def _tile(n, pref, quantum):
    t = min(pref, n) // quantum * quantum
    while t >= quantum:
        if n % t == 0:
            return t
        t -= quantum
    return n


def _params(*sem):
    return pltpu.CompilerParams(dimension_semantics=sem, vmem_limit_bytes=VMEM_LIMIT)


def _dot(a, b):
    return jnp.dot(a.astype(BF16), b.astype(BF16), preferred_element_type=F32)


def _dot_nt(a, b):
    return lax.dot_general(a.astype(BF16), b.astype(BF16), (((1,), (1,)), ((), ())),
                           preferred_element_type=F32)


def _dot_tn(a, b):
    return lax.dot_general(a.astype(BF16), b.astype(BF16), (((0,), (0,)), ((), ())),
                           preferred_element_type=F32)


def _split3(x):
    hi = x.astype(BF16)
    rest = x - hi.astype(F32)
    mid = rest.astype(BF16)
    return hi, mid, (rest - mid.astype(F32)).astype(BF16)


def _dot_f32(a, b):
    return jnp.dot(a, b, preferred_element_type=F32, precision=HIGHEST)


def _dot_mask(x, mask):
    m = mask.astype(BF16)
    hi, mid, lo = _split3(x)
    return (jnp.dot(hi, m, preferred_element_type=F32) + jnp.dot(mid, m, preferred_element_type=F32)
            + jnp.dot(lo, m, preferred_element_type=F32))


def _pack_bf16_pairs(x):
    half = x.shape[1] // 2
    bits = lax.bitcast_convert_type(x.astype(BF16).astype(F32), jnp.uint32)
    return (bits[:, :half] >> 16) | bits[:, half:]


def _unpack_bf16_pairs(p):
    left = lax.bitcast_convert_type(p << 16, F32)
    right = lax.bitcast_convert_type(p & jnp.uint32(0xFFFF0000), F32)
    return jnp.concatenate([left, right], axis=1).astype(BF16)


def _mod_kernel(c_ref, w_ref, b_ref, o_ref):
    c = c_ref[...]
    c = c * jax.nn.sigmoid(c)
    o_ref[...] = _dot(c, w_ref[...]) + b_ref[...]


def _mod_tables(cond, mod_w, mod_b):
    d = cond.shape[1]
    depth, _, n = mod_w.shape
    tn = _tile(n, 512, LANES)
    return pl.pallas_call(
        _mod_kernel,
        grid=(depth, n // tn),
        in_specs=[pl.BlockSpec((MOD_ROWS, d), lambda l, j: (0, 0)),
                  pl.BlockSpec((None, d, tn), lambda l, j: (l, 0, j)),
                  pl.BlockSpec((None, 1, tn), lambda l, j: (l, 0, j))],
        out_specs=pl.BlockSpec((None, MOD_ROWS, tn), lambda l, j: (l, 0, j)),
        out_shape=jax.ShapeDtypeStruct((depth, MOD_ROWS, n), F32),
        compiler_params=_params("parallel", "parallel"),
        name="mod_tables",
    )(cond, mod_w, mod_b.reshape(depth, 1, n))


def _norm_mod_kernel(x_ref, g_ref, sh_ref, sc_ref, h_ref):
    x = x_ref[...]
    y = x * lax.rsqrt(jnp.mean(x * x, axis=-1, keepdims=True) + RMS_EPS) * g_ref[...]
    h_ref[...] = (y * (1.0 + sc_ref[...]) + sh_ref[...]).astype(h_ref.dtype)


def _split_rows_specs(block, col_of, n_lat_tiles, row_axis):
    def lat(*idx):
        return (jnp.minimum(idx[row_axis], n_lat_tiles - 1), col_of(*idx))

    def ctx(*idx):
        return (jnp.maximum(idx[row_axis] - n_lat_tiles, 0), col_of(*idx))
    return [pl.BlockSpec(block, lat), pl.BlockSpec(block, ctx)]


def _norm_mod_split_kernel(xl_ref, xc_ref, g_ref, sh_ref, sc_ref, h_ref, *, n_lat_tiles):
    x = jnp.where(pl.program_id(0) < n_lat_tiles, xl_ref[...], xc_ref[...])
    y = x * lax.rsqrt(jnp.mean(x * x, axis=-1, keepdims=True) + RMS_EPS) * g_ref[...]
    h_ref[...] = (y * (1.0 + sc_ref[...]) + sh_ref[...]).astype(h_ref.dtype)


def _top2_sum(a, b, c, d):
    hi1, lo1 = jnp.maximum(a, b), jnp.minimum(a, b)
    hi2, lo2 = jnp.maximum(c, d), jnp.minimum(c, d)
    return jnp.maximum(hi1, hi2) + jnp.maximum(jnp.minimum(hi1, hi2), jnp.maximum(lo1, lo2))


def _first_argmax(vals):
    best_v = vals[0]
    best_i = jnp.zeros(vals[0].shape, jnp.int32)
    for i in range(1, len(vals)):
        better = vals[i] > best_v
        best_i = jnp.where(better, i, best_i)
        best_v = jnp.where(better, vals[i], best_v)
    return best_i, best_v


def _pick(rows, index):
    out = rows[0]
    for i in range(1, len(rows)):
        out = jnp.where(index == i, rows[i], out)
    return out


def _norm_mod_router_kernel(x_ref, g_ref, sh_ref, sc_ref, rwt_ref, rb_ref, h_ref, idx_ref, gate_ref,
                            *, n_experts):
    x = x_ref[...]
    y = x * lax.rsqrt(jnp.mean(x * x, axis=-1, keepdims=True) + RMS_EPS) * g_ref[...]
    h = y * (1.0 + sc_ref[...]) + sh_ref[...]
    h_ref[...] = _pack_bf16_pairs(h)
    logits = lax.dot_general(rwt_ref[...], h, (((1,), (1,)), ((), ())), preferred_element_type=F32,
                             precision=HIGHEST)
    aff_all = jax.nn.sigmoid(logits)
    sel_all = aff_all + rb_ref[...]
    per_group = n_experts // N_GROUPS
    aff = [aff_all[e:e + 1, :] for e in range(n_experts)]
    sel = [sel_all[e:e + 1, :] for e in range(n_experts)]
    assert per_group == 4 and TOP_K == 2
    best, _ = _first_argmax([_top2_sum(*sel[g * per_group:(g + 1) * per_group]) for g in range(N_GROUPS)])
    cand = [_pick([sel[g * per_group + i] for g in range(N_GROUPS)], best) for i in range(per_group)]
    cand_aff = [_pick([aff[g * per_group + i] for g in range(N_GROUPS)], best) for i in range(per_group)]
    i1, _ = _first_argmax(cand)
    i2, _ = _first_argmax([jnp.where(i1 == i, -jnp.inf, cand[i]) for i in range(per_group)])
    g1 = _pick(cand_aff, i1)
    g2 = _pick(cand_aff, i2)
    idx_ref[0:1, :] = best * per_group + i1
    idx_ref[1:2, :] = best * per_group + i2
    gate_ref[0:1, :] = g1 / (g1 + g2)
    gate_ref[1:2, :] = g2 / (g1 + g2)


def _norm_mod(x, g, mod3, shift_chunk, n_rows, dims, router=None):
    split = isinstance(x, tuple)
    d = x[0].shape[1] if split else x.shape[1]
    tm = _tile(dims["ctx_rows"], 256, 8)
    grp = dims["group_of"](tm)
    if split:
        x_specs = _split_rows_specs((tm, d), lambda i: 0, dims["lat_rows"] // tm, 0)
        kern = functools.partial(_norm_mod_split_kernel, n_lat_tiles=dims["lat_rows"] // tm)
    else:
        x_specs = [pl.BlockSpec((tm, d), lambda i: (i, 0))]
        kern = _norm_mod_kernel
    in_specs = x_specs + [pl.BlockSpec((1, d), lambda i: (0, 0)),
                          pl.BlockSpec((None, 1, d), lambda i: (grp(i), 0, shift_chunk)),
                          pl.BlockSpec((None, 1, d), lambda i: (grp(i), 0, shift_chunk + 1))]
    args = (list(x) if split else [x]) + [g.reshape(1, d), mod3, mod3]
    out_specs = [pl.BlockSpec((tm, d), lambda i: (i, 0))]
    out_shape = [jax.ShapeDtypeStruct((n_rows, d), BF16)]
    if router is not None:
        router_w, router_bias = router
        n_experts = router_w.shape[1]
        in_specs += [pl.BlockSpec((n_experts, d), lambda i: (0, 0)),
                     pl.BlockSpec((n_experts, 1), lambda i: (0, 0))]
        args += [router_w.T, router_bias.astype(F32).reshape(n_experts, 1)]
        out_specs = [pl.BlockSpec((tm, d // 2), lambda i: (i, 0))] + [pl.BlockSpec((TOP_K, tm), lambda i: (0, i))] * 2
        out_shape = [jax.ShapeDtypeStruct((n_rows, d // 2), jnp.uint32),
                     jax.ShapeDtypeStruct((TOP_K, n_rows), jnp.int32),
                     jax.ShapeDtypeStruct((TOP_K, n_rows), F32)]
        kern = functools.partial(_norm_mod_router_kernel, n_experts=n_experts)
    return pl.pallas_call(
        kern, grid=(n_rows // tm,), in_specs=in_specs, out_specs=out_specs, out_shape=out_shape,
        compiler_params=_params("parallel"), name="norm_mod",
    )(*args)


def _resident_bf16(w_ref, wb_ref):
    @pl.when(pl.program_id(1) == 0)
    def _():
        wb_ref[...] = w_ref[...].astype(BF16)
    return wb_ref[...]


def _proj_kernel(a_ref, w_ref, o_ref, wb_ref):
    w = _resident_bf16(w_ref, wb_ref)
    o_ref[...] = jnp.dot(a_ref[...], w, preferred_element_type=F32).astype(o_ref.dtype)


def _weight_spec(w, k, tn, col_block):
    layer = w[1]
    return pl.BlockSpec((None, k, tn), lambda j, i: (layer, 0, col_block(j)))


def _proj(a, w, n_rows, out_dtype, col_off=0, n_cols=None, tm_pref=512, tn_pref=512):
    k = a.shape[1]
    n_cols = w[0].shape[2] - col_off if n_cols is None else n_cols
    tm = _tile(n_rows, tm_pref, 8)
    tn = _tile(int(np.gcd(n_cols, col_off)) if col_off else n_cols, tn_pref, LANES)
    off = col_off // tn
    return pl.pallas_call(
        _proj_kernel,
        grid=(n_cols // tn, n_rows // tm),
        in_specs=[pl.BlockSpec((tm, k), lambda j, i: (i, 0)),
                  _weight_spec(w, k, tn, lambda j: j + off)],
        out_specs=pl.BlockSpec((tm, tn), lambda j, i: (i, j)),
        out_shape=jax.ShapeDtypeStruct((n_rows, n_cols), out_dtype),
        scratch_shapes=[pltpu.VMEM((k, tn), BF16)],
        compiler_params=_params("parallel", "arbitrary"), name="proj",
    )(a, w[0])


def _proj_cols_kernel(a_ref, *refs, plan):
    n_w = len(refs) - 2
    w_refs, o_ref, wb_ref = refs[:n_w], refs[n_w], refs[n_w + 1]

    @pl.when(pl.program_id(1) == 0)
    def _():
        blocks = [r[...] for r in w_refs]
        pieces = [jnp.zeros((wb_ref.shape[0], hi - lo), F32) if slot is None else blocks[slot][:, lo:hi]
                  for slot, lo, hi in plan]
        wb_ref[...] = jnp.concatenate(pieces, axis=1).astype(BF16)

    o_ref[...] = jnp.dot(a_ref[...], wb_ref[...], preferred_element_type=F32)


def _proj_cols(a, w, n_rows, n_tiles, block_of, plan):
    k = a.shape[1]
    tn = sum(hi - lo for _, lo, hi in plan)
    tm = _tile(n_rows, 512, 8)
    return pl.pallas_call(
        functools.partial(_proj_cols_kernel, plan=tuple(plan)),
        grid=(n_tiles, n_rows // tm),
        in_specs=[pl.BlockSpec((tm, k), lambda j, i: (i, 0))]
                 + [_weight_spec(w, k, LANES, f) for f in block_of],
        out_specs=pl.BlockSpec((tm, tn), lambda j, i: (i, j)),
        out_shape=jax.ShapeDtypeStruct((n_rows, n_tiles * tn), F32),
        scratch_shapes=[pltpu.VMEM((k, tn), BF16)],
        compiler_params=_params("parallel", "arbitrary"), name="proj_cols",
    )(a, *([w[0]] * len(block_of)))


def _column_pieces(ranges, pad_to):
    block_ids = sorted({b for s, e in ranges for b in range(s // LANES, (e - 1) // LANES + 1)})
    plan = []
    for (s, e), width in zip(ranges, pad_to):
        for b in range(s // LANES, (e - 1) // LANES + 1):
            plan.append((block_ids.index(b), max(s, b * LANES) - b * LANES, min(e, (b + 1) * LANES) - b * LANES))
        if width > e - s:
            plan.append((None, 0, width - (e - s)))
    return block_ids, plan


def _proj_res_kernel(a_ref, w_ref, x_ref, gt_ref, o_ref, wb_ref):
    w = _resident_bf16(w_ref, wb_ref)
    acc = jnp.dot(a_ref[...], w, preferred_element_type=F32)
    o_ref[...] = x_ref[...] + gt_ref[...] * acc


def _proj_res_split_kernel(a_ref, w_ref, xl_ref, xc_ref, gt_ref, o_ref, wb_ref, *, n_lat_tiles):
    w = _resident_bf16(w_ref, wb_ref)
    acc = jnp.dot(a_ref[...], w, preferred_element_type=F32)
    x = jnp.where(pl.program_id(1) < n_lat_tiles, xl_ref[...], xc_ref[...])
    o_ref[...] = x + gt_ref[...] * acc


def _proj_residual(a, w, x, mod3, gate_chunk, n_rows, dims, tn_pref=512):
    k = a.shape[1]
    d = w[0].shape[2]
    tm = _tile(dims["ctx_rows"], 512, 8)
    tn = _tile(d, tn_pref, LANES)
    grp = dims["group_of"](tm)
    gblk = gate_chunk * (d // tn)
    if isinstance(x, tuple):
        x_specs = _split_rows_specs((tm, tn), lambda j, i: j, dims["lat_rows"] // tm, 1)
        kern = functools.partial(_proj_res_split_kernel, n_lat_tiles=dims["lat_rows"] // tm)
        x_args = list(x)
    else:
        x_specs = [pl.BlockSpec((tm, tn), lambda j, i: (i, j))]
        kern = _proj_res_kernel
        x_args = [x]
    return pl.pallas_call(
        kern,
        grid=(d // tn, n_rows // tm),
        in_specs=[pl.BlockSpec((tm, k), lambda j, i: (i, 0)),
                  _weight_spec(w, k, tn, lambda j: j)] + x_specs
                 + [pl.BlockSpec((None, 1, tn), lambda j, i: (grp(i), 0, gblk + j))],
        out_specs=pl.BlockSpec((tm, tn), lambda j, i: (i, j)),
        out_shape=jax.ShapeDtypeStruct((n_rows, d), F32),
        scratch_shapes=[pltpu.VMEM((k, tn), BF16)],
        compiler_params=_params("parallel", "arbitrary"), name="proj_residual",
    )(a, w[0], *x_args, mod3)


def _proj_qk_kernel(a_ref, w_ref, nw_ref, cos_ref, sin_ref, o_ref, wb_ref, *, head, scale):
    w = _resident_bf16(w_ref, wb_ref)
    nw = nw_ref[...] * scale
    tm = a_ref.shape[0]
    rows = _tile(tm, 128, 8)
    for r in range(0, tm, rows):
        acc = jnp.dot(a_ref[r:r + rows, :], w, preferred_element_type=F32)
        cs = cos_ref[r:r + rows, :]
        sn = sin_ref[r:r + rows, :]
        for s in range(acc.shape[1] // head):
            x = acc[:, s * head:(s + 1) * head]
            xn = x * lax.rsqrt(jnp.mean(x * x, axis=-1, keepdims=True) + RMS_EPS) * nw
            xr = xn * cs + pltpu.roll(xn, head // 2, 1) * sn
            o_ref[r:r + rows, s * head:(s + 1) * head] = xr.astype(o_ref.dtype)


def _proj_qk(a, w, col_off, n_cols, norm_w, cos_t, sin_t, scale, n_rows):
    k = a.shape[1]
    head = norm_w.shape[0]
    tm = _tile(n_rows, 512, 8)
    tn = _tile(int(np.gcd(n_cols, col_off)) if col_off else n_cols, 512, head)
    off = col_off // tn
    return pl.pallas_call(
        functools.partial(_proj_qk_kernel, head=head, scale=scale),
        grid=(n_cols // tn, n_rows // tm),
        in_specs=[pl.BlockSpec((tm, k), lambda j, i: (i, 0)),
                  _weight_spec(w, k, tn, lambda j: j + off),
                  pl.BlockSpec((1, head), lambda j, i: (0, 0)),
                  pl.BlockSpec((tm, head), lambda j, i: (i, 0)),
                  pl.BlockSpec((tm, head), lambda j, i: (i, 0))],
        out_specs=pl.BlockSpec((tm, tn), lambda j, i: (i, j)),
        out_shape=jax.ShapeDtypeStruct((n_rows, n_cols), BF16),
        scratch_shapes=[pltpu.VMEM((k, tn), BF16)],
        compiler_params=_params("parallel", "arbitrary"), name="proj_qk",
    )(a, w[0], norm_w.reshape(1, head), cos_t, sin_t)


def _seg_sum(x, seg):
    n = x.shape[1]
    lane_blk = LANES if n % LANES == 0 else n
    r = lax.broadcasted_iota(jnp.int32, (lane_blk, lane_blk), 0) // seg
    c = lax.broadcasted_iota(jnp.int32, (lane_blk, lane_blk), 1) // seg
    same_seg = r == c
    parts = [_dot_mask(x[:, s:s + lane_blk], same_seg) for s in range(0, n, lane_blk)]
    return parts[0] if len(parts) == 1 else jnp.concatenate(parts, axis=1)


def _rwkv_prep_kernel(z_ref, zp_ref, zn_ref, t_ref, tp_ref, tn_ref, mu_ref, mut_ref, w0_ref, wup_ref,
                      a0_ref, aup_ref, gup_ref, kk_ref, ka_ref, rk_ref,
                      r_out, v_out, kkn_out, lw_out, km_out, bb_out, bon_out, gm_out,
                      *, rd, lg, lwp, seq, ctx, n_lat_rows):
    tm = z_ref.shape[0]
    row0 = pl.program_id(0) * tm
    in_lat = row0 < n_lat_rows
    seq_len = jnp.where(in_lat, seq, ctx)
    pos0 = jnp.where(in_lat, row0 % seq, (row0 - n_lat_rows) % ctx)
    has_prev = (pos0 != 0).astype(F32)
    has_next = (pos0 + tm != seq_len).astype(F32)

    def token_shift(cur_ref, prev_ref, next_ref, mix_ref):
        cur = cur_ref[...]
        rows = lax.broadcasted_iota(jnp.int32, cur.shape, 0)
        prev_row = prev_ref[7:8, :] * has_prev
        next_row = next_ref[0:1, :] * has_next
        before = jnp.where(rows == 0, prev_row, pltpu.roll(cur, 1, 0))
        after = jnp.where(rows == tm - 1, next_row, pltpu.roll(cur, tm - 1, 0))
        return cur + mix_ref[...] * (0.5 * (before + after) - cur)

    zs = token_shift(z_ref, zp_ref, zn_ref, mu_ref)
    ts = token_shift(t_ref, tp_ref, tn_ref, mut_ref)
    r = zs[:, 0:rd]
    k = zs[:, rd:2 * rd]
    v = zs[:, 2 * rd:3 * rd]
    g_down = ts[:, 0:lg]
    w_down = ts[:, lg:lg + lwp]
    a_down = ts[:, lg + lwp:lg + 2 * lwp]

    kk = k * kk_ref[...]
    kk = kk * lax.rsqrt(jnp.maximum(_seg_sum(kk * kk, RWKV_HEAD), 1e-24))
    r_out[...] = r
    v_out[...] = v
    kkn_out[...] = kk
    tw = jnp.tanh(w_down)
    bonus = jnp.zeros_like(r)
    for d in range(2):
        lw_out[d] = -np.float32(np.exp(-0.5)) * jax.nn.sigmoid(w0_ref[d] + _dot(tw, wup_ref[d]))
        a = jax.nn.sigmoid(a0_ref[d] + _dot(a_down, aup_ref[d]))
        k_mod = k * (1.0 + (a - 1.0) * ka_ref[...])
        km_out[d] = k_mod
        bb_out[d] = kk * a
        bonus = bonus + _seg_sum(r * k_mod * rk_ref[...], RWKV_HEAD) * v
    bon_out[...] = bonus
    gm_out[...] = _dot(jax.nn.sigmoid(g_down), gup_ref[...])


def _rwkv_prep(z_rkv, z, n_rows, lay, dims, mu_main, mu_tail, w0, w_up_p, a0, a_up_p, g_up, k_k, k_a, r_k):
    rd, lg, lwp = lay["rd"], lay["lg"], lay["lwp"]
    mw = 3 * rd
    tw = lg + 2 * lwp
    tblk = 0
    tm = _tile(dims["ctx"], 128, 8)
    nb8 = n_rows // 8
    full = lambda shape: pl.BlockSpec(shape, lambda i: (0,) * len(shape))
    row_spec = pl.BlockSpec((tm, rd), lambda i: (i, 0))
    dir_spec = pl.BlockSpec((2, tm, rd), lambda i: (0, i, 0))
    sds = jax.ShapeDtypeStruct
    kern = functools.partial(_rwkv_prep_kernel, rd=rd, lg=lg, lwp=lwp, seq=dims["seq"], ctx=dims["ctx"],
                             n_lat_rows=dims["lat_rows"])
    return pl.pallas_call(
        kern,
        grid=(n_rows // tm,),
        in_specs=[pl.BlockSpec((tm, mw), lambda i: (i, 0)),
                  pl.BlockSpec((8, mw), lambda i: (jnp.maximum(i * (tm // 8) - 1, 0), 0)),
                  pl.BlockSpec((8, mw), lambda i: (jnp.minimum((i + 1) * (tm // 8), nb8 - 1), 0)),
                  pl.BlockSpec((tm, tw), lambda i: (i, tblk)),
                  pl.BlockSpec((8, tw), lambda i: (jnp.maximum(i * (tm // 8) - 1, 0), tblk)),
                  pl.BlockSpec((8, tw), lambda i: (jnp.minimum((i + 1) * (tm // 8), nb8 - 1), tblk)),
                  full((1, mw)), full((1, tw)), full((2, 1, rd)), full((2, lwp, rd)), full((2, 1, rd)), full((2, lwp, rd)),
                  full((lg, rd)), full((1, rd)), full((1, rd)), full((1, rd))],
        out_specs=[row_spec, row_spec, row_spec, dir_spec, dir_spec, dir_spec, row_spec, row_spec],
        out_shape=[sds((n_rows, rd), F32)] * 3 + [sds((2, n_rows, rd), F32)] * 3 + [sds((n_rows, rd), F32)] * 2,
        compiler_params=_params("parallel"), name="rwkv_prep",
    )(z_rkv, z_rkv, z_rkv, z, z, z, mu_main, mu_tail, w0.reshape(2, 1, rd), w_up_p, a0.reshape(2, 1, rd), a_up_p, g_up,
      k_k.reshape(1, rd), k_a.reshape(1, rd), r_k.reshape(1, rd))


def _rwkv_scan_kernel(r_ref, v_ref, kk_ref, lw_ref, km_ref, bb_ref, y_ref, s_ref, *, heads):
    d = pl.program_id(0)
    j = pl.program_id(3)

    @pl.when(j == 0)
    def _():
        s_ref[...] = jnp.zeros_like(s_ref)

    n = RWKV_HEAD
    chunk = lw_ref.shape[0]
    t_idx = lax.broadcasted_iota(jnp.int32, (chunk, chunk), 0)
    s_idx = lax.broadcasted_iota(jnp.int32, (chunk, chunk), 1)
    lead = jnp.where(d == 0, t_idx - s_idx, s_idx - t_idx)
    incl = lead >= 0
    assert chunk & (chunk - 1) == 0

    lw = lw_ref[...]
    cum = _dot_f32(incl.astype(F32), lw)
    tot = jnp.sum(lw, axis=0, keepdims=True)
    e_in = jnp.exp(cum)
    e_neg = jnp.exp(-cum)
    e_last = jnp.exp(tot - cum)
    kkn = kk_ref[...]
    km = km_ref[...]
    bb = bb_ref[...]
    r_t = r_ref[...] * e_in
    a_t = -kkn * jnp.exp(cum - lw)
    b_t = bb * e_neg
    k_t = km * e_neg
    b_l = bb * e_last
    k_l = km * e_last
    w_l = jnp.exp(tot)
    v_all = v_ref[...]

    hs = range(heads)
    sl = [slice(h * n, (h + 1) * n) for h in hs]
    t2 = lax.broadcasted_iota(jnp.int32, (2 * chunk, 2 * chunk), 0)
    s2 = lax.broadcasted_iota(jnp.int32, (2 * chunk, 2 * chunk), 1)
    tt = jnp.where(t2 >= chunk, t2 - chunk, t2)
    ss = jnp.where(s2 >= chunk, s2 - chunk, s2)
    keep = jnp.where(d == 0, tt - ss, ss - tt) >= jnp.where(t2 < chunk, 1, 0)
    right = lax.broadcasted_iota(jnp.int32, (chunk, 2 * chunk), 1) >= chunk
    eye_right = (lax.broadcasted_iota(jnp.int32, (chunk, 2 * chunk), 1)
                 == lax.broadcasted_iota(jnp.int32, (chunk, 2 * chunk), 0) + chunk).astype(F32)
    zeros_v = jnp.zeros((chunk, n), F32)
    stack = lambda top, bottom: jnp.concatenate([top, bottom], axis=0)

    s0 = [s_ref[h] for h in hs]
    v = [v_all[:, sl[h]] for h in hs]
    pair = [jnp.where(keep, _dot_nt(stack(a_t[:, sl[h]], r_t[:, sl[h]]), stack(b_t[:, sl[h]], k_t[:, sl[h]])), 0.0)
            for h in hs]
    a_side = [pair[h][:chunk] for h in hs]
    r_side = [pair[h][chunk:] for h in hs]
    x = [_dot_nt(a_t[:, sl[h]], s0[h]) + _dot(a_side[h], stack(zeros_v, v[h])) for h in hs]
    y0 = [_dot_nt(r_t[:, sl[h]], s0[h]) for h in hs]
    q = [jnp.where(right, eye_right, a_side[h]) for h in hs]
    for _ in range(chunk.bit_length() - 1):
        q = [_dot(q[h][:, :chunk], q[h]) + jnp.where(right, q[h], 0.0) for h in hs]
    uv = [stack(_dot(q[h][:, chunk:], x[h]), v[h]) for h in hs]
    y_ref[...] = jnp.concatenate([y0[h] + _dot(r_side[h], uv[h]) for h in hs], axis=1)
    for h in hs:
        s_ref[h] = s0[h] * w_l[:, sl[h]] + _dot_tn(uv[h], stack(b_l[:, sl[h]], k_l[:, sl[h]]))


def _chunk_index(d, b, j, dims):
    nc_ctx, nc_lat, nb = dims["ctx"] // CHUNK, dims["seq"] // CHUNK, dims["batch"]
    jc = jnp.where(d == 0, j, nc_ctx - 1 - j)
    jl = jnp.where(d == 0, j - nc_ctx, nc_lat - 1 - (j - nc_ctx))
    return jnp.where(j < nc_ctx, nb * nc_lat + b * nc_ctx + jc, b * nc_lat + jl)


def _rwkv_scan(r, v, kkn, lw, km, bb, dims, heads_per_step):
    n_rows, rd = r.shape
    gw = heads_per_step * RWKV_HEAD
    n_chunks = (dims["ctx"] + dims["seq"]) // CHUNK
    cidx = lambda d, b, g, j: _chunk_index(d, b, j, dims)
    row_spec = pl.BlockSpec((CHUNK, gw), lambda d, b, g, j: (cidx(d, b, g, j), g))
    dir_spec = pl.BlockSpec((None, CHUNK, gw), lambda d, b, g, j: (d, cidx(d, b, g, j), g))
    return pl.pallas_call(
        functools.partial(_rwkv_scan_kernel, heads=heads_per_step),
        grid=(2, dims["batch"], rd // gw, n_chunks),
        in_specs=[row_spec, row_spec, row_spec, dir_spec, dir_spec, dir_spec],
        out_specs=dir_spec,
        out_shape=jax.ShapeDtypeStruct((2, n_rows, rd), F32),
        scratch_shapes=[pltpu.VMEM((heads_per_step, RWKV_HEAD, RWKV_HEAD), F32)],
        compiler_params=_params("parallel", "parallel", "parallel", "arbitrary"), name="rwkv_scan",
    )(r, v, kkn, lw, km, bb)


def _mlstm_kernel(q_ref, k_ref, v_ref, gc_ref, gr_ref, bc_ref, br_ref, h_ref, c_ref, n_ref, m_ref,
                  *, heads, dqk, dv):
    d = pl.program_id(0)
    j = pl.program_id(2)

    @pl.when(j == 0)
    def _():
        c_ref[...] = jnp.zeros_like(c_ref)
        n_ref[...] = jnp.zeros_like(n_ref)
        m_ref[...] = jnp.zeros_like(m_ref)

    chunk = q_ref.shape[0]
    t_idx = lax.broadcasted_iota(jnp.int32, (chunk, chunk), 0)
    s_idx = lax.broadcasted_iota(jnp.int32, (chunk, chunk), 1)
    lead = jnp.where(d == 0, t_idx - s_idx, s_idx - t_idx)
    incl = lead >= 0
    tri = incl.astype(F32)
    tri_t = (lead <= 0).astype(F32)

    gcol = gc_ref[...] + bc_ref[...]
    grow = gr_ref[...] + br_ref[...]
    i_col = gcol[:, :heads]
    f_col = jax.nn.log_sigmoid(gcol[:, heads:])
    i_row = grow[:heads, :]
    f_row = jax.nn.log_sigmoid(grow[heads:, :])
    b_col = _dot_f32(tri, f_col)
    b_row = _dot_f32(f_row, tri_t)
    b_last = jnp.sum(f_col, axis=0, keepdims=True)
    scale = np.float32(dqk ** -0.5)

    hs = range(heads)
    q = [q_ref[:, h * dqk:(h + 1) * dqk] * scale for h in hs]
    k = [k_ref[:, h * dqk:(h + 1) * dqk] for h in hs]
    v = [v_ref[:, h * dv:(h + 1) * dv] for h in hs]
    c_st = [c_ref[h] for h in hs]
    n_st = [n_ref[h] for h in hs]
    m_st = [m_ref[h][:, 0:1] for h in hs]
    bc = [b_col[:, h:h + 1] for h in hs]
    qk = [_dot_nt(q[h], k[h]) for h in hs]
    qc = [_dot(q[h], c_st[h]) for h in hs]
    dmat = [jnp.where(incl, bc[h] + (i_row[h:h + 1, :] - b_row[h:h + 1, :]), -jnp.inf) for h in hs]
    inter = [bc[h] + m_st[h] for h in hs]
    m_t = [jnp.maximum(inter[h], jnp.max(dmat[h], axis=-1, keepdims=True)) for h in hs]
    w_inter = [jnp.exp(inter[h] - m_t[h]) for h in hs]
    s = [qk[h] * jnp.exp(dmat[h] - m_t[h]) for h in hs]
    num = [w_inter[h] * qc[h] + _dot(s[h], v[h]) for h in hs]
    den = [w_inter[h] * jnp.sum(q[h] * n_st[h], axis=-1, keepdims=True) + jnp.sum(s[h], axis=-1, keepdims=True)
           for h in hs]
    h_ref[...] = jnp.concatenate([num[h] / jnp.maximum(jnp.abs(den[h]), jnp.exp(-m_t[h])) for h in hs], axis=1)
    bl = [b_last[:, h:h + 1] for h in hs]
    g = [bl[h] - bc[h] + i_col[:, h:h + 1] for h in hs]
    m_new = [jnp.maximum(bl[h] + m_st[h], jnp.max(g[h], axis=0, keepdims=True)) for h in hs]
    decay = [jnp.exp(bl[h] + m_st[h] - m_new[h]) for h in hs]
    kw = [k[h] * jnp.exp(g[h] - m_new[h]) for h in hs]
    for h in hs:
        c_ref[h] = decay[h] * c_st[h] + _dot_tn(kw[h], v[h])
        n_ref[h] = decay[h] * n_st[h] + jnp.sum(kw[h], axis=0, keepdims=True)
        m_ref[h] = jnp.broadcast_to(m_new[h], m_ref.shape[1:])


def _mlstm_scan(z, lay, dims, gates_col, gates_row, bias_col, bias_row, heads, dqk, dv):
    n_rows = z.shape[0]
    n_chunks = (dims["ctx"] + dims["seq"]) // CHUNK
    md = heads * dv
    qw = heads * dqk
    cidx = lambda d, b, j: _chunk_index(d, b, j, dims)
    q_blk, k_blk, v_blk = lay["mq"] // qw, lay["mk"] // qw, lay["mv"] // md
    return pl.pallas_call(
        functools.partial(_mlstm_kernel, heads=heads, dqk=dqk, dv=dv),
        grid=(2, dims["batch"], n_chunks),
        in_specs=[pl.BlockSpec((CHUNK, qw), lambda d, b, j: (cidx(d, b, j), q_blk)),
                  pl.BlockSpec((CHUNK, qw), lambda d, b, j: (cidx(d, b, j), k_blk)),
                  pl.BlockSpec((CHUNK, md), lambda d, b, j: (cidx(d, b, j), v_blk)),
                  pl.BlockSpec((None, None, CHUNK, 2 * heads), lambda d, b, j: (d, cidx(d, b, j), 0, 0)),
                  pl.BlockSpec((None, None, 2 * heads, CHUNK), lambda d, b, j: (d, cidx(d, b, j), 0, 0)),
                  pl.BlockSpec((None, 1, 2 * heads), lambda d, b, j: (d, 0, 0)),
                  pl.BlockSpec((None, 2 * heads, 1), lambda d, b, j: (d, 0, 0))],
        out_specs=pl.BlockSpec((None, CHUNK, md), lambda d, b, j: (d, cidx(d, b, j), 0)),
        out_shape=jax.ShapeDtypeStruct((2, n_rows, md), F32),
        scratch_shapes=[pltpu.VMEM((heads, dqk, dv), F32), pltpu.VMEM((heads, 1, dqk), F32),
                        pltpu.VMEM((heads, 1, LANES), F32)],
        compiler_params=_params("parallel", "parallel", "arbitrary"), name="mlstm_scan",
    )(z, z, z, gates_col, gates_row, bias_col, bias_row)


def _even_finish_kernel(y_ref, bon_ref, gm_ref, gnw_ref, gnb_ref, h_ref, o_ref, nw_ref, out_ref,
                        *, rd, heads, dv):
    y = y_ref[0] + y_ref[1]
    inv_n = np.float32(1.0 / RWKV_HEAD)
    mean = _seg_sum(y, RWKV_HEAD) * inv_n
    yc = y - mean
    var = _seg_sum(yc * yc, RWKV_HEAD) * inv_n
    yn = yc * lax.rsqrt(var + GN_EPS) * gnw_ref[...] + gnb_ref[...] + bon_ref[...]
    out_ref[:, 0:rd] = (yn * gm_ref[...]).astype(out_ref.dtype)
    hm = h_ref[0] + h_ref[1]
    for h in range(heads):
        sl = slice(h * dv, (h + 1) * dv)
        x = hm[:, sl]
        xn = x * lax.rsqrt(jnp.mean(x * x, axis=-1, keepdims=True) + RMS_EPS) * nw_ref[:, sl]
        out_ref[:, rd + h * dv:rd + (h + 1) * dv] = (jax.nn.sigmoid(o_ref[:, sl]) * xn).astype(out_ref.dtype)


def _even_finish(y, bonus, gmul, gn_w, gn_b, hm, z, lay, ml_norm, n_rows, dims, heads, dv):
    rd = y.shape[2]
    md = hm.shape[2]
    tm = _tile(dims["ctx_rows"], 256, 8)
    o_blk = lay["mo"] // md
    return pl.pallas_call(
        functools.partial(_even_finish_kernel, rd=rd, heads=heads, dv=dv),
        grid=(n_rows // tm,),
        in_specs=[pl.BlockSpec((2, tm, rd), lambda i: (0, i, 0)),
                  pl.BlockSpec((tm, rd), lambda i: (i, 0)),
                  pl.BlockSpec((tm, rd), lambda i: (i, 0)),
                  pl.BlockSpec((1, rd), lambda i: (0, 0)),
                  pl.BlockSpec((1, rd), lambda i: (0, 0)),
                  pl.BlockSpec((2, tm, md), lambda i: (0, i, 0)),
                  pl.BlockSpec((tm, md), lambda i: (i, o_blk)),
                  pl.BlockSpec((1, md), lambda i: (0, 0))],
        out_specs=pl.BlockSpec((tm, rd + md), lambda i: (i, 0)),
        out_shape=jax.ShapeDtypeStruct((n_rows, rd + md), BF16),
        compiler_params=_params("parallel"), name="even_finish",
    )(y, bonus, gmul, gn_w.reshape(1, rd), gn_b.reshape(1, rd), hm, z, ml_norm.reshape(1, md))


def _attn_kernel(q_ref, kl_ref, kc_ref, vl_ref, vc_ref, o_ref, *, group, head):
    tq = q_ref.shape[0]
    q = jnp.concatenate([q_ref[:, g * head:(g + 1) * head] for g in range(group)], axis=0)
    ck = _tile(kl_ref.shape[0], 256, 8)
    keys = [kc_ref[...]] + [kl_ref[c * ck:(c + 1) * ck, :] for c in range(kl_ref.shape[0] // ck)]
    vals = [vc_ref[...]] + [vl_ref[c * ck:(c + 1) * ck, :] for c in range(kl_ref.shape[0] // ck)]
    m = acc = None
    for kc, vc in zip(keys, vals):
        s = _dot_nt(q, kc)
        row_max = jnp.max(s, axis=-1, keepdims=True)
        v_aug = jnp.concatenate([vc, jnp.ones_like(vc)], axis=1)
        if m is None:
            m = row_max
            acc = _dot(jnp.exp2(s - m), v_aug)
        else:
            m_new = jnp.maximum(m, row_max)
            acc = acc * jnp.exp2(m - m_new) + _dot(jnp.exp2(s - m_new), v_aug)
            m = m_new
    o = acc[:, :head] / acc[:, head:head + 1]
    for g in range(group):
        o_ref[:, g * head:(g + 1) * head] = o[g * tq:(g + 1) * tq].astype(o_ref.dtype)


def _attention(q, k, v, dims, head, group):
    seq, ctx, nb = dims["seq"], dims["ctx"], dims["batch"]
    kvh = k.shape[1] // head
    tq = _tile(seq, 256, 8)
    nq = seq // tq
    gw = group * head
    return pl.pallas_call(
        functools.partial(_attn_kernel, group=group, head=head),
        grid=(nb, kvh, nq),
        in_specs=[pl.BlockSpec((tq, gw), lambda b, h, i: (b * nq + i, h)),
                  pl.BlockSpec((seq, head), lambda b, h, i: (b, h)),
                  pl.BlockSpec((ctx, head), lambda b, h, i: (nb * (seq // ctx) + b, h)),
                  pl.BlockSpec((seq, head), lambda b, h, i: (b, h)),
                  pl.BlockSpec((ctx, head), lambda b, h, i: (nb * (seq // ctx) + b, h))],
        out_specs=pl.BlockSpec((tq, gw), lambda b, h, i: (b * nq + i, h)),
        out_shape=jax.ShapeDtypeStruct(q.shape, BF16),
        compiler_params=_params("parallel", "parallel", "parallel"), name="attention",
    )(q, k, k, v, v)


def _new_expert(i, be_ref):
    return jnp.logical_or(i == 0, be_ref[i] != be_ref[jnp.maximum(i - 1, 0)])


def _moe_up_kernel(be_ref, nu_ref, x_ref, w1_ref, w3_ref, mid_ref, w1_bf, w3_bf):
    i = pl.program_id(1)

    @pl.when(_new_expert(i, be_ref))
    def _():
        w1_bf[...] = w1_ref[...].astype(BF16)
        w3_bf[...] = w3_ref[...].astype(BF16)

    @pl.when(i < nu_ref[0])
    def _():
        x = _unpack_bf16_pairs(x_ref[...])
        a = jnp.dot(x, w1_bf[...], preferred_element_type=F32)
        b = jnp.dot(x, w3_bf[...], preferred_element_type=F32)
        mid_ref[...] = (a * jax.nn.sigmoid(a) * b).astype(mid_ref.dtype)

    @pl.when(i >= nu_ref[0])
    def _():
        mid_ref[...] = jnp.zeros_like(mid_ref)


def _moe_down_kernel(be_ref, nu_ref, mid_ref, w2_ref, o_ref, w2_bf):
    i = pl.program_id(1)

    @pl.when(_new_expert(i, be_ref))
    def _():
        w2_bf[...] = w2_ref[...].astype(BF16)

    @pl.when(i < nu_ref[0])
    def _():
        o_ref[...] = jnp.dot(mid_ref[...], w2_bf[...], preferred_element_type=F32)

    @pl.when(i >= nu_ref[0])
    def _():
        o_ref[...] = jnp.zeros_like(o_ref)


def _moe_ffn(xs, block_expert, n_used, w1, w3, w2, layer, bm):
    n_rows = xs.shape[0]
    d, de = w1.shape[2], w1.shape[3]
    tde = _tile(de, 512, LANES)
    tn = _tile(d, 4096, LANES)
    nblk = n_rows // bm
    mid = pl.pallas_call(
        _moe_up_kernel,
        grid_spec=pltpu.PrefetchScalarGridSpec(
            num_scalar_prefetch=2, grid=(de // tde, nblk),
            in_specs=[pl.BlockSpec((bm, d // 2), lambda k, i, be, nu: (i, 0)),
                      pl.BlockSpec((None, None, d, tde), lambda k, i, be, nu: (layer, be[i], 0, k)),
                      pl.BlockSpec((None, None, d, tde), lambda k, i, be, nu: (layer, be[i], 0, k))],
            out_specs=pl.BlockSpec((bm, tde), lambda k, i, be, nu: (i, k)),
            scratch_shapes=[pltpu.VMEM((d, tde), BF16), pltpu.VMEM((d, tde), BF16)]),
        out_shape=jax.ShapeDtypeStruct((n_rows, de), BF16),
        compiler_params=_params("arbitrary", "arbitrary"), name="moe_up",
    )(block_expert, n_used, xs, w1, w3)
    return pl.pallas_call(
        _moe_down_kernel,
        grid_spec=pltpu.PrefetchScalarGridSpec(
            num_scalar_prefetch=2, grid=(d // tn, nblk),
            in_specs=[pl.BlockSpec((bm, de), lambda n, i, be, nu: (i, 0)),
                      pl.BlockSpec((None, None, de, tn), lambda n, i, be, nu: (layer, be[i], 0, n))],
            out_specs=pl.BlockSpec((bm, tn), lambda n, i, be, nu: (i, n)),
            scratch_shapes=[pltpu.VMEM((de, tn), BF16)]),
        out_shape=jax.ShapeDtypeStruct((n_rows, d), F32),
        compiler_params=_params("arbitrary", "arbitrary"), name="moe_down",
    )(block_expert, n_used, mid, w2)


def _dispatch_plan(idx, n_experts, bm):
    n_tok = idx.shape[1]
    n_assign = n_tok * TOP_K
    expert = idx.T.reshape(-1)
    onehot = (expert[:, None] == jnp.arange(n_experts)[None, :]).astype(jnp.int32)
    rank = jnp.take_along_axis(jnp.cumsum(onehot, axis=0) - onehot, expert[:, None], 1)[:, 0]
    counts = jnp.sum(onehot, axis=0)
    padded = (counts + bm - 1) // bm * bm
    pad_end = jnp.cumsum(padded)
    pos = (pad_end - padded)[expert] + rank
    n_blocks = -(-n_assign // bm) + n_experts
    token = jnp.repeat(jnp.arange(n_tok, dtype=jnp.int32), TOP_K)
    filler = jnp.arange(n_blocks * bm, dtype=jnp.int32) % n_tok
    row_token = filler.at[pos].set(token)
    block_start = jnp.arange(n_blocks, dtype=jnp.int32) * bm
    block_expert = jnp.minimum(jnp.sum((pad_end[None, :] <= block_start[:, None]).astype(jnp.int32), axis=1),
                               n_experts - 1)
    n_used = (pad_end[-1] // bm).astype(jnp.int32).reshape(1)
    block_expert = jnp.where(jnp.arange(n_blocks) < n_used[0], block_expert,
                             block_expert[jnp.maximum(n_used[0] - 1, 0)])
    return pos.reshape(n_tok, TOP_K), row_token, block_expert, n_used


def _ffn_res_kernel(x_ref, f0_ref, f1_ref, g_ref, gt_ref, o_ref):
    g = g_ref[...]
    f = g[:, 0:1] * f0_ref[...] + g[:, 1:2] * f1_ref[...]
    o_ref[...] = x_ref[...] + gt_ref[...] * f


def _ffn_res_norm_kernel(x_ref, f0_ref, f1_ref, g_ref, gt_ref, nw_ref, o_ref):
    g = g_ref[...]
    f = g[:, 0:1] * f0_ref[...] + g[:, 1:2] * f1_ref[...]
    x = x_ref[...] + gt_ref[...] * f
    o_ref[...] = x * lax.rsqrt(jnp.mean(x * x, axis=-1, keepdims=True) + RMS_EPS) * nw_ref[...]


def _ffn_residual(x, f0, f1, gate, mod3, gate_chunk, n_rows, dims, final_norm=None):
    d = x.shape[1]
    tm = _tile(dims["ctx_rows"], 256, 8)
    grp = dims["group_of"](tm)
    row = pl.BlockSpec((tm, d), lambda i: (i, 0))
    in_specs = [row, row, row, pl.BlockSpec((tm, TOP_K), lambda i: (i, 0)),
                pl.BlockSpec((None, 1, d), lambda i: (grp(i), 0, gate_chunk))]
    args = [x, f0, f1, gate, mod3]
    kern = _ffn_res_kernel
    if final_norm is not None:
        in_specs.append(pl.BlockSpec((1, d), lambda i: (0, 0)))
        args.append(final_norm.reshape(1, d))
        kern = _ffn_res_norm_kernel
    return pl.pallas_call(
        kern, grid=(n_rows // tm,), in_specs=in_specs, out_specs=row,
        out_shape=jax.ShapeDtypeStruct((n_rows, d), F32),
        compiler_params=_params("parallel"), name="ffn_residual",
    )(*args)


def _even_layout(rd, lw, la, lg, mh, dqk, md):
    lwp = _rup(lw + 4 * mh, LANES)
    assert _rup(la, LANES) == lwp
    lay = {"rd": rd, "lg": lg, "lwp": lwp}
    off = 0
    for name, width in (("mq", mh * dqk), ("mk", mh * dqk), ("mv", md), ("mo", md)):
        assert off % width == 0
        lay[name] = off
        off += width
    lay["width"] = off
    return lay


def _pad_cols(seg, width):
    return jnp.pad(seg, ((0, 0), (0, width - seg.shape[1])))


def _even_projection(h, w_in, n_rows, lay, lw, la, mh):
    rd, lg, lwp = lay["rd"], lay["lg"], lay["lwp"]
    o = 3 * rd
    rww = o + lw + la + lg
    m_end = rww + lay["width"]
    z_rkv = _proj(h, w_in, n_rows, F32, n_cols=o)
    tn = _tile(lay["width"], 512, LANES)
    shift, base, per_tile = rww % LANES, rww // LANES, tn // LANES
    if shift:
        plan = [(0, shift, LANES)] + [(q, 0, LANES) for q in range(1, per_tile)] + [(per_tile, 0, shift)]
    else:
        plan = [(q, 0, LANES) for q in range(per_tile)]
    slots = per_tile + (1 if shift else 0)
    z_m = _proj_cols(h, w_in, n_rows, lay["width"] // tn,
                     [lambda j, q=q: base + per_tile * j + q for q in range(slots)], plan)
    ids, plan_t = _column_pieces([(o + lw + la, rww), (o, o + lw), (m_end, m_end + 4 * mh), (o + lw, o + lw + la)],
                                 [lg, lw, lwp - lw, lwp])
    z_t = _proj_cols(h, w_in, n_rows, 1, [lambda j, b=b: b for b in ids], plan_t)
    return z_rkv, z_m, z_t


def _mix_vectors(mu, lay, lw, la):
    rd, lg, lwp = lay["rd"], lay["lg"], lay["lwp"]
    o = 3 * rd
    rww = o + lw + la + lg
    mu2 = mu.reshape(1, -1)
    mu_tail = jnp.concatenate([mu2[:, o + lw + la:rww], _pad_cols(mu2[:, o:o + lw], lwp),
                               _pad_cols(mu2[:, o + lw:o + lw + la], lwp)], axis=1)
    return mu2[:, :o], mu_tail


def _pad_rows(w, rows):
    return jnp.pad(w, ((0, 0), (0, rows - w.shape[1]), (0, 0)))


def _rope_tables(dims, head):
    seq, nb = dims["seq"], dims["batch"]
    pairs = head // 4
    rows = seq // GRID_W
    row = np.repeat(np.arange(rows), GRID_W).astype(np.float32)
    col = np.tile(np.arange(GRID_W), rows).astype(np.float32)
    inv = (np.float32(ROPE_THETA) ** (-np.arange(pairs, dtype=np.float32) / np.float32(pairs))).astype(np.float32)
    ang = np.concatenate([row[:, None] * inv, col[:, None] * inv], -1)
    cos, sin = np.cos(ang).astype(np.float32), np.sin(ang).astype(np.float32)
    cos_t = np.tile(np.concatenate([cos, cos], -1), (nb, 1))
    sin_t = np.tile(np.concatenate([-sin, sin], -1), (nb, 1))
    n_ctx = dims["ctx_rows"]
    return (jnp.asarray(np.concatenate([cos_t, np.ones((n_ctx, head), np.float32)], 0)),
            jnp.asarray(np.concatenate([sin_t, np.zeros((n_ctx, head), np.float32)], 0)))


def kernel(x, c, ctx, c_ctx, mod_w, mod_b, norm_mix, norm_ffn, norm_final, router_w, router_bias,
           exp_w1, exp_w3, exp_w2, ev_w_in, ev_w_out, rw_mu, rw_w0, rw_w_up, rw_a0, rw_a_up, rw_g_up,
           rw_k_k, rw_k_a, rw_r_k, rw_gn_w, rw_gn_b, ml_gate_b, ml_norm, at_w_qkv, at_q_norm, at_k_norm,
           at_w_o):
    nb, seq, d = x.shape
    n_ctx = ctx.shape[1]
    depth = mod_w.shape[0]
    lat_rows, ctx_rows = nb * seq, nb * n_ctx
    all_rows = lat_rows + ctx_rows
    assert nb + 1 <= MOD_ROWS and seq % n_ctx == 0 and n_ctx % CHUNK == 0
    dims = {"batch": nb, "seq": seq, "ctx": n_ctx, "lat_rows": lat_rows, "ctx_rows": ctx_rows,
            "group_of": lambda tm: (lambda i: jnp.minimum(i * tm // seq, nb))}

    n_experts = router_w.shape[1]
    rd = rw_w0.shape[-1]
    lw, la, lg = rw_w_up.shape[2], rw_a_up.shape[2], rw_g_up.shape[1]
    mh = ml_gate_b.shape[-1]
    md = ml_norm.shape[-1]
    dv = md // mh
    dqk = dv // 2
    head = at_q_norm.shape[-1]
    kv_dim = (at_w_qkv.shape[-1] - d) // 2
    group = (d // head) // (kv_dim // head)
    moe_bm = 256

    xa = (x.reshape(lat_rows, d), ctx.reshape(ctx_rows, d))
    cond = jnp.zeros((MOD_ROWS, d), F32).at[:nb].set(c).at[nb].set(c_ctx)
    lay = _even_layout(rd, lw, la, lg, mh, dqk, md)
    mods = _mod_tables(cond, mod_w, mod_b)
    w_in_bf16 = ev_w_in.astype(BF16)
    out = None

    for layer in range(depth):
        ctx_out = layer < depth - 1
        j = layer // 2
        rows_out = all_rows if ctx_out else lat_rows
        mod3 = mods[layer].reshape(MOD_ROWS, 1, 6 * d)

        h = _norm_mod(xa, norm_mix[layer], mod3, 0, all_rows, dims)[0]
        if layer % 2 == 0:
            mu_main, mu_tail = _mix_vectors(rw_mu[j], lay, lw, la)
            z_rkv, z, z_t = _even_projection(h, (w_in_bf16, j), all_rows, lay, lw, la, mh)
            r, v, kkn, lwd, km, bb, bonus, gmul = _rwkv_prep(
                z_rkv, z_t, all_rows, lay, dims, mu_main, mu_tail, rw_w0[j], _pad_rows(rw_w_up[j], lay["lwp"]).astype(BF16),
                rw_a0[j], _pad_rows(rw_a_up[j], lay["lwp"]).astype(BF16), rw_g_up[j].astype(BF16),
                rw_k_k[j], rw_k_a[j], rw_r_k[j])
            y = _rwkv_scan(r, v, kkn, lwd, km, bb, dims, heads_per_step=min(32, rd // RWKV_HEAD))
            g_off = lay["lg"] + lw
            gates = z_t[:, g_off:g_off + 4 * mh].reshape(all_rows, 2, 2 * mh)
            gates_col = jnp.moveaxis(gates, 1, 0).reshape(2, all_rows // CHUNK, CHUNK, 2 * mh)
            gates_row = jnp.swapaxes(gates_col, 2, 3)
            bias = ml_gate_b[j].reshape(2, 2 * mh)
            hm = _mlstm_scan(z, lay, dims, gates_col, gates_row, bias.reshape(2, 1, 2 * mh),
                             bias.reshape(2, 2 * mh, 1), mh, dqk, dv)
            mix = _even_finish(y, bonus, gmul, rw_gn_w[j], rw_gn_b[j], hm, z, lay, ml_norm[j],
                               rows_out, dims, mh, dv)
            xa_new = _proj_residual(mix, (ev_w_out, j), xa, mod3, 2, rows_out, dims)
        else:
            if ctx_out:
                raise NotImplementedError("context output of an attention layer (depth > 2)")
            w_qkv = (at_w_qkv, j)
            cos_t, sin_t = _rope_tables(dims, head)
            q_scale = np.float32(head ** -0.5 * np.log2(np.e))
            q = _proj_qk(h, w_qkv, 0, d, at_q_norm[j], cos_t, sin_t, q_scale, lat_rows)
            k = _proj_qk(h, w_qkv, d, kv_dim, at_k_norm[j], cos_t, sin_t, np.float32(1.0), all_rows)
            v = _proj(h, w_qkv, all_rows, BF16, col_off=d + kv_dim, n_cols=kv_dim)
            att = _attention(q, k, v, dims, head, group)
            xa_new = _proj_residual(att, (at_w_o, j), xa, mod3, 2, lat_rows, dims)
        xa = xa_new

        h2, idx, gate_t = _norm_mod(xa, norm_ffn[layer], mod3, 3, rows_out, dims,
                                    router=(router_w, router_bias))
        gate = gate_t.T
        pos, row_token, block_expert, n_used = _dispatch_plan(idx, n_experts, moe_bm)
        xs = h2.at[row_token].get(mode="promise_in_bounds")
        ys = _moe_ffn(xs, block_expert, n_used, exp_w1, exp_w3, exp_w2, layer, moe_bm)
        f0 = ys.at[pos[:, 0]].get(mode="promise_in_bounds")
        f1 = ys.at[pos[:, 1]].get(mode="promise_in_bounds")
        last = layer == depth - 1
        xa = _ffn_residual(xa, f0, f1, gate, mod3, 5, rows_out, dims,
                           final_norm=norm_final if last else None)
        if last:
            out = xa[:lat_rows].reshape(nb, seq, d)
    return out
```

```python
import functools

import jax
import jax.numpy as jnp
import numpy as np
from jax import lax
from jax.experimental import pallas as pl
from jax.experimental.pallas import tpu as pltpu

F32 = jnp.float32
BF16 = jnp.bfloat16
HIGHEST = lax.Precision.HIGHEST

GRID_W = 64
RMS_EPS = 1e-6
GN_EPS = 64e-5
RWKV_HEAD = 64
CHUNK = 64
N_GROUPS = 4
TOP_K = 2
ROPE_THETA = 10000.0
LANES = 128
MOD_ROWS = 8
VMEM_LIMIT = 56 * 1024 * 1024


def _rup(n, m):
    return (n + m - 1) // m * m


def _tile(n, pref, quantum):
    t = min(pref, n) // quantum * quantum
    while t >= quantum:
        if n % t == 0:
            return t
        t -= quantum
    return n


def _params(*sem):
    return pltpu.CompilerParams(dimension_semantics=sem, vmem_limit_bytes=VMEM_LIMIT)


def _dot(a, b):
    return jnp.dot(a.astype(BF16), b.astype(BF16), preferred_element_type=F32)


def _dot_nt(a, b):
    return lax.dot_general(a.astype(BF16), b.astype(BF16), (((1,), (1,)), ((), ())),
                           preferred_element_type=F32)


def _dot_tn(a, b):
    return lax.dot_general(a.astype(BF16), b.astype(BF16), (((0,), (0,)), ((), ())),
                           preferred_element_type=F32)


def _split3(x):
    hi = x.astype(BF16)
    rest = x - hi.astype(F32)
    mid = rest.astype(BF16)
    return hi, mid, (rest - mid.astype(F32)).astype(BF16)


def _dot_f32(a, b):
    return jnp.dot(a, b, preferred_element_type=F32, precision=HIGHEST)


def _dot_mask(x, mask):
    m = mask.astype(BF16)
    hi, mid, lo = _split3(x)
    return (jnp.dot(hi, m, preferred_element_type=F32) + jnp.dot(mid, m, preferred_element_type=F32)
            + jnp.dot(lo, m, preferred_element_type=F32))


def _pack_bf16_pairs(x):
    half = x.shape[1] // 2
    bits = lax.bitcast_convert_type(x.astype(BF16).astype(F32), jnp.uint32)
    return (bits[:, :half] >> 16) | bits[:, half:]


def _unpack_bf16_pairs(p):
    left = lax.bitcast_convert_type(p << 16, F32)
    right = lax.bitcast_convert_type(p & jnp.uint32(0xFFFF0000), F32)
    return jnp.concatenate([left, right], axis=1).astype(BF16)


def _mod_kernel(c_ref, w_ref, b_ref, o_ref):
    c = c_ref[...]
    c = c * jax.nn.sigmoid(c)
    o_ref[...] = _dot(c, w_ref[...]) + b_ref[...]


def _mod_tables(cond, mod_w, mod_b):
    d = cond.shape[1]
    depth, _, n = mod_w.shape
    tn = _tile(n, 512, LANES)
    return pl.pallas_call(
        _mod_kernel,
        grid=(depth, n // tn),
        in_specs=[pl.BlockSpec((MOD_ROWS, d), lambda l, j: (0, 0)),
                  pl.BlockSpec((None, d, tn), lambda l, j: (l, 0, j)),
                  pl.BlockSpec((None, 1, tn), lambda l, j: (l, 0, j))],
        out_specs=pl.BlockSpec((None, MOD_ROWS, tn), lambda l, j: (l, 0, j)),
        out_shape=jax.ShapeDtypeStruct((depth, MOD_ROWS, n), F32),
        compiler_params=_params("parallel", "parallel"),
        name="mod_tables",
    )(cond, mod_w, mod_b.reshape(depth, 1, n))


def _norm_mod_kernel(x_ref, g_ref, sh_ref, sc_ref, h_ref):
    x = x_ref[...]
    y = x * lax.rsqrt(jnp.mean(x * x, axis=-1, keepdims=True) + RMS_EPS) * g_ref[...]
    h_ref[...] = (y * (1.0 + sc_ref[...]) + sh_ref[...]).astype(h_ref.dtype)


def _split_rows_specs(block, col_of, n_lat_tiles, row_axis):
    def lat(*idx):
        return (jnp.minimum(idx[row_axis], n_lat_tiles - 1), col_of(*idx))

    def ctx(*idx):
        return (jnp.maximum(idx[row_axis] - n_lat_tiles, 0), col_of(*idx))
    return [pl.BlockSpec(block, lat), pl.BlockSpec(block, ctx)]


def _norm_mod_split_kernel(xl_ref, xc_ref, g_ref, sh_ref, sc_ref, h_ref, *, n_lat_tiles):
    x = jnp.where(pl.program_id(0) < n_lat_tiles, xl_ref[...], xc_ref[...])
    y = x * lax.rsqrt(jnp.mean(x * x, axis=-1, keepdims=True) + RMS_EPS) * g_ref[...]
    h_ref[...] = (y * (1.0 + sc_ref[...]) + sh_ref[...]).astype(h_ref.dtype)


def _top2_sum(a, b, c, d):
    hi1, lo1 = jnp.maximum(a, b), jnp.minimum(a, b)
    hi2, lo2 = jnp.maximum(c, d), jnp.minimum(c, d)
    return jnp.maximum(hi1, hi2) + jnp.maximum(jnp.minimum(hi1, hi2), jnp.maximum(lo1, lo2))


def _first_argmax(vals):
    best_v = vals[0]
    best_i = jnp.zeros(vals[0].shape, jnp.int32)
    for i in range(1, len(vals)):
        better = vals[i] > best_v
        best_i = jnp.where(better, i, best_i)
        best_v = jnp.where(better, vals[i], best_v)
    return best_i, best_v


def _pick(rows, index):
    out = rows[0]
    for i in range(1, len(rows)):
        out = jnp.where(index == i, rows[i], out)
    return out


def _norm_mod_router_kernel(x_ref, g_ref, sh_ref, sc_ref, rwt_ref, rb_ref, h_ref, idx_ref, gate_ref,
                            *, n_experts):
    x = x_ref[...]
    y = x * lax.rsqrt(jnp.mean(x * x, axis=-1, keepdims=True) + RMS_EPS) * g_ref[...]
    h = y * (1.0 + sc_ref[...]) + sh_ref[...]
    h_ref[...] = _pack_bf16_pairs(h)
    logits = lax.dot_general(rwt_ref[...], h, (((1,), (1,)), ((), ())), preferred_element_type=F32,
                             precision=HIGHEST)
    aff_all = jax.nn.sigmoid(logits)
    sel_all = aff_all + rb_ref[...]
    per_group = n_experts // N_GROUPS
    aff = [aff_all[e:e + 1, :] for e in range(n_experts)]
    sel = [sel_all[e:e + 1, :] for e in range(n_experts)]
    assert per_group == 4 and TOP_K == 2
    best, _ = _first_argmax([_top2_sum(*sel[g * per_group:(g + 1) * per_group]) for g in range(N_GROUPS)])
    cand = [_pick([sel[g * per_group + i] for g in range(N_GROUPS)], best) for i in range(per_group)]
    cand_aff = [_pick([aff[g * per_group + i] for g in range(N_GROUPS)], best) for i in range(per_group)]
    i1, _ = _first_argmax(cand)
    i2, _ = _first_argmax([jnp.where(i1 == i, -jnp.inf, cand[i]) for i in range(per_group)])
    g1 = _pick(cand_aff, i1)
    g2 = _pick(cand_aff, i2)
    idx_ref[0:1, :] = best * per_group + i1
    idx_ref[1:2, :] = best * per_group + i2
    gate_ref[0:1, :] = g1 / (g1 + g2)
    gate_ref[1:2, :] = g2 / (g1 + g2)


def _norm_mod(x, g, mod3, shift_chunk, n_rows, dims, router=None):
    split = isinstance(x, tuple)
    d = x[0].shape[1] if split else x.shape[1]
    tm = _tile(dims["ctx_rows"], 256, 8)
    grp = dims["group_of"](tm)
    if split:
        x_specs = _split_rows_specs((tm, d), lambda i: 0, dims["lat_rows"] // tm, 0)
        kern = functools.partial(_norm_mod_split_kernel, n_lat_tiles=dims["lat_rows"] // tm)
    else:
        x_specs = [pl.BlockSpec((tm, d), lambda i: (i, 0))]
        kern = _norm_mod_kernel
    in_specs = x_specs + [pl.BlockSpec((1, d), lambda i: (0, 0)),
                          pl.BlockSpec((None, 1, d), lambda i: (grp(i), 0, shift_chunk)),
                          pl.BlockSpec((None, 1, d), lambda i: (grp(i), 0, shift_chunk + 1))]
    args = (list(x) if split else [x]) + [g.reshape(1, d), mod3, mod3]
    out_specs = [pl.BlockSpec((tm, d), lambda i: (i, 0))]
    out_shape = [jax.ShapeDtypeStruct((n_rows, d), BF16)]
    if router is not None:
        router_w, router_bias = router
        n_experts = router_w.shape[1]
        in_specs += [pl.BlockSpec((n_experts, d), lambda i: (0, 0)),
                     pl.BlockSpec((n_experts, 1), lambda i: (0, 0))]
        args += [router_w.T, router_bias.astype(F32).reshape(n_experts, 1)]
        out_specs = [pl.BlockSpec((tm, d // 2), lambda i: (i, 0))] + [pl.BlockSpec((TOP_K, tm), lambda i: (0, i))] * 2
        out_shape = [jax.ShapeDtypeStruct((n_rows, d // 2), jnp.uint32),
                     jax.ShapeDtypeStruct((TOP_K, n_rows), jnp.int32),
                     jax.ShapeDtypeStruct((TOP_K, n_rows), F32)]
        kern = functools.partial(_norm_mod_router_kernel, n_experts=n_experts)
    return pl.pallas_call(
        kern, grid=(n_rows // tm,), in_specs=in_specs, out_specs=out_specs, out_shape=out_shape,
        compiler_params=_params("parallel"), name="norm_mod",
    )(*args)


def _resident_bf16(w_ref, wb_ref):
    @pl.when(pl.program_id(1) == 0)
    def _():
        wb_ref[...] = w_ref[...].astype(BF16)
    return wb_ref[...]


def _proj_kernel(a_ref, w_ref, o_ref, wb_ref):
    w = _resident_bf16(w_ref, wb_ref)
    o_ref[...] = jnp.dot(a_ref[...], w, preferred_element_type=F32).astype(o_ref.dtype)


def _weight_spec(w, k, tn, col_block):
    layer = w[1]
    return pl.BlockSpec((None, k, tn), lambda j, i: (layer, 0, col_block(j)))


def _proj(a, w, n_rows, out_dtype, col_off=0, n_cols=None, tm_pref=512, tn_pref=512):
    k = a.shape[1]
    n_cols = w[0].shape[2] - col_off if n_cols is None else n_cols
    tm = _tile(n_rows, tm_pref, 8)
    tn = _tile(int(np.gcd(n_cols, col_off)) if col_off else n_cols, tn_pref, LANES)
    off = col_off // tn
    return pl.pallas_call(
        _proj_kernel,
        grid=(n_cols // tn, n_rows // tm),
        in_specs=[pl.BlockSpec((tm, k), lambda j, i: (i, 0)),
                  _weight_spec(w, k, tn, lambda j: j + off)],
        out_specs=pl.BlockSpec((tm, tn), lambda j, i: (i, j)),
        out_shape=jax.ShapeDtypeStruct((n_rows, n_cols), out_dtype),
        scratch_shapes=[pltpu.VMEM((k, tn), BF16)],
        compiler_params=_params("parallel", "arbitrary"), name="proj",
    )(a, w[0])


def _proj_cols_kernel(a_ref, *refs, plan):
    n_w = len(refs) - 2
    w_refs, o_ref, wb_ref = refs[:n_w], refs[n_w], refs[n_w + 1]

    @pl.when(pl.program_id(1) == 0)
    def _():
        blocks = [r[...] for r in w_refs]
        pieces = [jnp.zeros((wb_ref.shape[0], hi - lo), F32) if slot is None else blocks[slot][:, lo:hi]
                  for slot, lo, hi in plan]
        wb_ref[...] = jnp.concatenate(pieces, axis=1).astype(BF16)

    o_ref[...] = jnp.dot(a_ref[...], wb_ref[...], preferred_element_type=F32)


def _proj_cols(a, w, n_rows, n_tiles, block_of, plan):
    k = a.shape[1]
    tn = sum(hi - lo for _, lo, hi in plan)
    tm = _tile(n_rows, 512, 8)
    return pl.pallas_call(
        functools.partial(_proj_cols_kernel, plan=tuple(plan)),
        grid=(n_tiles, n_rows // tm),
        in_specs=[pl.BlockSpec((tm, k), lambda j, i: (i, 0))]
                 + [_weight_spec(w, k, LANES, f) for f in block_of],
        out_specs=pl.BlockSpec((tm, tn), lambda j, i: (i, j)),
        out_shape=jax.ShapeDtypeStruct((n_rows, n_tiles * tn), F32),
        scratch_shapes=[pltpu.VMEM((k, tn), BF16)],
        compiler_params=_params("parallel", "arbitrary"), name="proj_cols",
    )(a, *([w[0]] * len(block_of)))


def _column_pieces(ranges, pad_to):
    block_ids = sorted({b for s, e in ranges for b in range(s // LANES, (e - 1) // LANES + 1)})
    plan = []
    for (s, e), width in zip(ranges, pad_to):
        for b in range(s // LANES, (e - 1) // LANES + 1):
            plan.append((block_ids.index(b), max(s, b * LANES) - b * LANES, min(e, (b + 1) * LANES) - b * LANES))
        if width > e - s:
            plan.append((None, 0, width - (e - s)))
    return block_ids, plan


def _proj_res_kernel(a_ref, w_ref, x_ref, gt_ref, o_ref, wb_ref):
    w = _resident_bf16(w_ref, wb_ref)
    acc = jnp.dot(a_ref[...], w, preferred_element_type=F32)
    o_ref[...] = x_ref[...] + gt_ref[...] * acc


def _proj_res_split_kernel(a_ref, w_ref, xl_ref, xc_ref, gt_ref, o_ref, wb_ref, *, n_lat_tiles):
    w = _resident_bf16(w_ref, wb_ref)
    acc = jnp.dot(a_ref[...], w, preferred_element_type=F32)
    x = jnp.where(pl.program_id(1) < n_lat_tiles, xl_ref[...], xc_ref[...])
    o_ref[...] = x + gt_ref[...] * acc


def _proj_residual(a, w, x, mod3, gate_chunk, n_rows, dims, tn_pref=512):
    k = a.shape[1]
    d = w[0].shape[2]
    tm = _tile(dims["ctx_rows"], 512, 8)
    tn = _tile(d, tn_pref, LANES)
    grp = dims["group_of"](tm)
    gblk = gate_chunk * (d // tn)
    if isinstance(x, tuple):
        x_specs = _split_rows_specs((tm, tn), lambda j, i: j, dims["lat_rows"] // tm, 1)
        kern = functools.partial(_proj_res_split_kernel, n_lat_tiles=dims["lat_rows"] // tm)
        x_args = list(x)
    else:
        x_specs = [pl.BlockSpec((tm, tn), lambda j, i: (i, j))]
        kern = _proj_res_kernel
        x_args = [x]
    return pl.pallas_call(
        kern,
        grid=(d // tn, n_rows // tm),
        in_specs=[pl.BlockSpec((tm, k), lambda j, i: (i, 0)),
                  _weight_spec(w, k, tn, lambda j: j)] + x_specs
                 + [pl.BlockSpec((None, 1, tn), lambda j, i: (grp(i), 0, gblk + j))],
        out_specs=pl.BlockSpec((tm, tn), lambda j, i: (i, j)),
        out_shape=jax.ShapeDtypeStruct((n_rows, d), F32),
        scratch_shapes=[pltpu.VMEM((k, tn), BF16)],
        compiler_params=_params("parallel", "arbitrary"), name="proj_residual",
    )(a, w[0], *x_args, mod3)


def _proj_qk_kernel(a_ref, w_ref, nw_ref, cos_ref, sin_ref, o_ref, wb_ref, *, head, scale):
    w = _resident_bf16(w_ref, wb_ref)
    nw = nw_ref[...] * scale
    tm = a_ref.shape[0]
    rows = _tile(tm, 128, 8)
    for r in range(0, tm, rows):
        acc = jnp.dot(a_ref[r:r + rows, :], w, preferred_element_type=F32)
        cs = cos_ref[r:r + rows, :]
        sn = sin_ref[r:r + rows, :]
        for s in range(acc.shape[1] // head):
            x = acc[:, s * head:(s + 1) * head]
            xn = x * lax.rsqrt(jnp.mean(x * x, axis=-1, keepdims=True) + RMS_EPS) * nw
            xr = xn * cs + pltpu.roll(xn, head // 2, 1) * sn
            o_ref[r:r + rows, s * head:(s + 1) * head] = xr.astype(o_ref.dtype)


def _proj_qk(a, w, col_off, n_cols, norm_w, cos_t, sin_t, scale, n_rows):
    k = a.shape[1]
    head = norm_w.shape[0]
    tm = _tile(n_rows, 512, 8)
    tn = _tile(int(np.gcd(n_cols, col_off)) if col_off else n_cols, 512, head)
    off = col_off // tn
    return pl.pallas_call(
        functools.partial(_proj_qk_kernel, head=head, scale=scale),
        grid=(n_cols // tn, n_rows // tm),
        in_specs=[pl.BlockSpec((tm, k), lambda j, i: (i, 0)),
                  _weight_spec(w, k, tn, lambda j: j + off),
                  pl.BlockSpec((1, head), lambda j, i: (0, 0)),
                  pl.BlockSpec((tm, head), lambda j, i: (i, 0)),
                  pl.BlockSpec((tm, head), lambda j, i: (i, 0))],
        out_specs=pl.BlockSpec((tm, tn), lambda j, i: (i, j)),
        out_shape=jax.ShapeDtypeStruct((n_rows, n_cols), BF16),
        scratch_shapes=[pltpu.VMEM((k, tn), BF16)],
        compiler_params=_params("parallel", "arbitrary"), name="proj_qk",
    )(a, w[0], norm_w.reshape(1, head), cos_t, sin_t)


def _seg_sum(x, seg):
    n = x.shape[1]
    lane_blk = LANES if n % LANES == 0 else n
    r = lax.broadcasted_iota(jnp.int32, (lane_blk, lane_blk), 0) // seg
    c = lax.broadcasted_iota(jnp.int32, (lane_blk, lane_blk), 1) // seg
    same_seg = r == c
    parts = [_dot_mask(x[:, s:s + lane_blk], same_seg) for s in range(0, n, lane_blk)]
    return parts[0] if len(parts) == 1 else jnp.concatenate(parts, axis=1)


def _rwkv_prep_kernel(z_ref, zp_ref, zn_ref, t_ref, tp_ref, tn_ref, mu_ref, mut_ref, w0_ref, wup_ref,
                      a0_ref, aup_ref, gup_ref, kk_ref, ka_ref, rk_ref,
                      r_out, v_out, kkn_out, lw_out, km_out, bb_out, bon_out, gm_out,
                      *, rd, lg, lwp, seq, ctx, n_lat_rows):
    tm = z_ref.shape[0]
    row0 = pl.program_id(0) * tm
    in_lat = row0 < n_lat_rows
    seq_len = jnp.where(in_lat, seq, ctx)
    pos0 = jnp.where(in_lat, row0 % seq, (row0 - n_lat_rows) % ctx)
    has_prev = (pos0 != 0).astype(F32)
    has_next = (pos0 + tm != seq_len).astype(F32)

    def token_shift(cur_ref, prev_ref, next_ref, mix_ref):
        cur = cur_ref[...]
        rows = lax.broadcasted_iota(jnp.int32, cur.shape, 0)
        prev_row = prev_ref[7:8, :] * has_prev
        next_row = next_ref[0:1, :] * has_next
        before = jnp.where(rows == 0, prev_row, pltpu.roll(cur, 1, 0))
        after = jnp.where(rows == tm - 1, next_row, pltpu.roll(cur, tm - 1, 0))
        return cur + mix_ref[...] * (0.5 * (before + after) - cur)

    zs = token_shift(z_ref, zp_ref, zn_ref, mu_ref)
    ts = token_shift(t_ref, tp_ref, tn_ref, mut_ref)
    r = zs[:, 0:rd]
    k = zs[:, rd:2 * rd]
    v = zs[:, 2 * rd:3 * rd]
    g_down = ts[:, 0:lg]
    w_down = ts[:, lg:lg + lwp]
    a_down = ts[:, lg + lwp:lg + 2 * lwp]

    kk = k * kk_ref[...]
    kk = kk * lax.rsqrt(jnp.maximum(_seg_sum(kk * kk, RWKV_HEAD), 1e-24))
    r_out[...] = r
    v_out[...] = v
    kkn_out[...] = kk
    tw = jnp.tanh(w_down)
    bonus = jnp.zeros_like(r)
    for d in range(2):
        lw_out[d] = -np.float32(np.exp(-0.5)) * jax.nn.sigmoid(w0_ref[d] + _dot(tw, wup_ref[d]))
        a = jax.nn.sigmoid(a0_ref[d] + _dot(a_down, aup_ref[d]))
        k_mod = k * (1.0 + (a - 1.0) * ka_ref[...])
        km_out[d] = k_mod
        bb_out[d] = kk * a
        bonus = bonus + _seg_sum(r * k_mod * rk_ref[...], RWKV_HEAD) * v
    bon_out[...] = bonus
    gm_out[...] = _dot(jax.nn.sigmoid(g_down), gup_ref[...])


def _rwkv_prep(z_rkv, z, n_rows, lay, dims, mu_main, mu_tail, w0, w_up_p, a0, a_up_p, g_up, k_k, k_a, r_k):
    rd, lg, lwp = lay["rd"], lay["lg"], lay["lwp"]
    mw = 3 * rd
    tw = lg + 2 * lwp
    tblk = 0
    tm = _tile(dims["ctx"], 128, 8)
    nb8 = n_rows // 8
    full = lambda shape: pl.BlockSpec(shape, lambda i: (0,) * len(shape))
    row_spec = pl.BlockSpec((tm, rd), lambda i: (i, 0))
    dir_spec = pl.BlockSpec((2, tm, rd), lambda i: (0, i, 0))
    sds = jax.ShapeDtypeStruct
    kern = functools.partial(_rwkv_prep_kernel, rd=rd, lg=lg, lwp=lwp, seq=dims["seq"], ctx=dims["ctx"],
                             n_lat_rows=dims["lat_rows"])
    return pl.pallas_call(
        kern,
        grid=(n_rows // tm,),
        in_specs=[pl.BlockSpec((tm, mw), lambda i: (i, 0)),
                  pl.BlockSpec((8, mw), lambda i: (jnp.maximum(i * (tm // 8) - 1, 0), 0)),
                  pl.BlockSpec((8, mw), lambda i: (jnp.minimum((i + 1) * (tm // 8), nb8 - 1), 0)),
                  pl.BlockSpec((tm, tw), lambda i: (i, tblk)),
                  pl.BlockSpec((8, tw), lambda i: (jnp.maximum(i * (tm // 8) - 1, 0), tblk)),
                  pl.BlockSpec((8, tw), lambda i: (jnp.minimum((i + 1) * (tm // 8), nb8 - 1), tblk)),
                  full((1, mw)), full((1, tw)), full((2, 1, rd)), full((2, lwp, rd)), full((2, 1, rd)), full((2, lwp, rd)),
                  full((lg, rd)), full((1, rd)), full((1, rd)), full((1, rd))],
        out_specs=[row_spec, row_spec, row_spec, dir_spec, dir_spec, dir_spec, row_spec, row_spec],
        out_shape=[sds((n_rows, rd), F32)] * 3 + [sds((2, n_rows, rd), F32)] * 3 + [sds((n_rows, rd), F32)] * 2,
        compiler_params=_params("parallel"), name="rwkv_prep",
    )(z_rkv, z_rkv, z_rkv, z, z, z, mu_main, mu_tail, w0.reshape(2, 1, rd), w_up_p, a0.reshape(2, 1, rd), a_up_p, g_up,
      k_k.reshape(1, rd), k_a.reshape(1, rd), r_k.reshape(1, rd))


def _rwkv_scan_kernel(r_ref, v_ref, kk_ref, lw_ref, km_ref, bb_ref, y_ref, s_ref, *, heads):
    d = pl.program_id(0)
    j = pl.program_id(3)

    @pl.when(j == 0)
    def _():
        s_ref[...] = jnp.zeros_like(s_ref)

    n = RWKV_HEAD
    chunk = lw_ref.shape[0]
    t_idx = lax.broadcasted_iota(jnp.int32, (chunk, chunk), 0)
    s_idx = lax.broadcasted_iota(jnp.int32, (chunk, chunk), 1)
    lead = jnp.where(d == 0, t_idx - s_idx, s_idx - t_idx)
    incl = lead >= 0
    assert chunk & (chunk - 1) == 0

    lw = lw_ref[...]
    cum = _dot_f32(incl.astype(F32), lw)
    tot = jnp.sum(lw, axis=0, keepdims=True)
    e_in = jnp.exp(cum)
    e_neg = jnp.exp(-cum)
    e_last = jnp.exp(tot - cum)
    kkn = kk_ref[...]
    km = km_ref[...]
    bb = bb_ref[...]
    r_t = r_ref[...] * e_in
    a_t = -kkn * jnp.exp(cum - lw)
    b_t = bb * e_neg
    k_t = km * e_neg
    b_l = bb * e_last
    k_l = km * e_last
    w_l = jnp.exp(tot)
    v_all = v_ref[...]

    hs = range(heads)
    sl = [slice(h * n, (h + 1) * n) for h in hs]
    t2 = lax.broadcasted_iota(jnp.int32, (2 * chunk, 2 * chunk), 0)
    s2 = lax.broadcasted_iota(jnp.int32, (2 * chunk, 2 * chunk), 1)
    tt = jnp.where(t2 >= chunk, t2 - chunk, t2)
    ss = jnp.where(s2 >= chunk, s2 - chunk, s2)
    keep = jnp.where(d == 0, tt - ss, ss - tt) >= jnp.where(t2 < chunk, 1, 0)
    right = lax.broadcasted_iota(jnp.int32, (chunk, 2 * chunk), 1) >= chunk
    eye_right = (lax.broadcasted_iota(jnp.int32, (chunk, 2 * chunk), 1)
                 == lax.broadcasted_iota(jnp.int32, (chunk, 2 * chunk), 0) + chunk).astype(F32)
    zeros_v = jnp.zeros((chunk, n), F32)
    stack = lambda top, bottom: jnp.concatenate([top, bottom], axis=0)

    s0 = [s_ref[h] for h in hs]
    v = [v_all[:, sl[h]] for h in hs]
    pair = [jnp.where(keep, _dot_nt(stack(a_t[:, sl[h]], r_t[:, sl[h]]), stack(b_t[:, sl[h]], k_t[:, sl[h]])), 0.0)
            for h in hs]
    a_side = [pair[h][:chunk] for h in hs]
    r_side = [pair[h][chunk:] for h in hs]
    x = [_dot_nt(a_t[:, sl[h]], s0[h]) + _dot(a_side[h], stack(zeros_v, v[h])) for h in hs]
    y0 = [_dot_nt(r_t[:, sl[h]], s0[h]) for h in hs]
    q = [jnp.where(right, eye_right, a_side[h]) for h in hs]
    for _ in range(chunk.bit_length() - 1):
        q = [_dot(q[h][:, :chunk], q[h]) + jnp.where(right, q[h], 0.0) for h in hs]
    uv = [stack(_dot(q[h][:, chunk:], x[h]), v[h]) for h in hs]
    y_ref[...] = jnp.concatenate([y0[h] + _dot(r_side[h], uv[h]) for h in hs], axis=1)
    for h in hs:
        s_ref[h] = s0[h] * w_l[:, sl[h]] + _dot_tn(uv[h], stack(b_l[:, sl[h]], k_l[:, sl[h]]))


def _chunk_index(d, b, j, dims):
    nc_ctx, nc_lat, nb = dims["ctx"] // CHUNK, dims["seq"] // CHUNK, dims["batch"]
    jc = jnp.where(d == 0, j, nc_ctx - 1 - j)
    jl = jnp.where(d == 0, j - nc_ctx, nc_lat - 1 - (j - nc_ctx))
    return jnp.where(j < nc_ctx, nb * nc_lat + b * nc_ctx + jc, b * nc_lat + jl)


def _rwkv_scan(r, v, kkn, lw, km, bb, dims, heads_per_step):
    n_rows, rd = r.shape
    gw = heads_per_step * RWKV_HEAD
    n_chunks = (dims["ctx"] + dims["seq"]) // CHUNK
    cidx = lambda d, b, g, j: _chunk_index(d, b, j, dims)
    row_spec = pl.BlockSpec((CHUNK, gw), lambda d, b, g, j: (cidx(d, b, g, j), g))
    dir_spec = pl.BlockSpec((None, CHUNK, gw), lambda d, b, g, j: (d, cidx(d, b, g, j), g))
    return pl.pallas_call(
        functools.partial(_rwkv_scan_kernel, heads=heads_per_step),
        grid=(2, dims["batch"], rd // gw, n_chunks),
        in_specs=[row_spec, row_spec, row_spec, dir_spec, dir_spec, dir_spec],
        out_specs=dir_spec,
        out_shape=jax.ShapeDtypeStruct((2, n_rows, rd), F32),
        scratch_shapes=[pltpu.VMEM((heads_per_step, RWKV_HEAD, RWKV_HEAD), F32)],
        compiler_params=_params("parallel", "parallel", "parallel", "arbitrary"), name="rwkv_scan",
    )(r, v, kkn, lw, km, bb)


def _mlstm_kernel(q_ref, k_ref, v_ref, gc_ref, gr_ref, bc_ref, br_ref, h_ref, c_ref, n_ref, m_ref,
                  *, heads, dqk, dv):
    d = pl.program_id(0)
    j = pl.program_id(2)

    @pl.when(j == 0)
    def _():
        c_ref[...] = jnp.zeros_like(c_ref)
        n_ref[...] = jnp.zeros_like(n_ref)
        m_ref[...] = jnp.zeros_like(m_ref)

    chunk = q_ref.shape[0]
    t_idx = lax.broadcasted_iota(jnp.int32, (chunk, chunk), 0)
    s_idx = lax.broadcasted_iota(jnp.int32, (chunk, chunk), 1)
    lead = jnp.where(d == 0, t_idx - s_idx, s_idx - t_idx)
    incl = lead >= 0
    tri = incl.astype(F32)
    tri_t = (lead <= 0).astype(F32)

    gcol = gc_ref[...] + bc_ref[...]
    grow = gr_ref[...] + br_ref[...]
    i_col = gcol[:, :heads]
    f_col = jax.nn.log_sigmoid(gcol[:, heads:])
    i_row = grow[:heads, :]
    f_row = jax.nn.log_sigmoid(grow[heads:, :])
    b_col = _dot_f32(tri, f_col)
    b_row = _dot_f32(f_row, tri_t)
    b_last = jnp.sum(f_col, axis=0, keepdims=True)
    scale = np.float32(dqk ** -0.5)

    hs = range(heads)
    q = [q_ref[:, h * dqk:(h + 1) * dqk] * scale for h in hs]
    k = [k_ref[:, h * dqk:(h + 1) * dqk] for h in hs]
    v = [v_ref[:, h * dv:(h + 1) * dv] for h in hs]
    c_st = [c_ref[h] for h in hs]
    n_st = [n_ref[h] for h in hs]
    m_st = [m_ref[h][:, 0:1] for h in hs]
    bc = [b_col[:, h:h + 1] for h in hs]
    qk = [_dot_nt(q[h], k[h]) for h in hs]
    qc = [_dot(q[h], c_st[h]) for h in hs]
    dmat = [jnp.where(incl, bc[h] + (i_row[h:h + 1, :] - b_row[h:h + 1, :]), -jnp.inf) for h in hs]
    inter = [bc[h] + m_st[h] for h in hs]
    m_t = [jnp.maximum(inter[h], jnp.max(dmat[h], axis=-1, keepdims=True)) for h in hs]
    w_inter = [jnp.exp(inter[h] - m_t[h]) for h in hs]
    s = [qk[h] * jnp.exp(dmat[h] - m_t[h]) for h in hs]
    num = [w_inter[h] * qc[h] + _dot(s[h], v[h]) for h in hs]
    den = [w_inter[h] * jnp.sum(q[h] * n_st[h], axis=-1, keepdims=True) + jnp.sum(s[h], axis=-1, keepdims=True)
           for h in hs]
    h_ref[...] = jnp.concatenate([num[h] / jnp.maximum(jnp.abs(den[h]), jnp.exp(-m_t[h])) for h in hs], axis=1)
    bl = [b_last[:, h:h + 1] for h in hs]
    g = [bl[h] - bc[h] + i_col[:, h:h + 1] for h in hs]
    m_new = [jnp.maximum(bl[h] + m_st[h], jnp.max(g[h], axis=0, keepdims=True)) for h in hs]
    decay = [jnp.exp(bl[h] + m_st[h] - m_new[h]) for h in hs]
    kw = [k[h] * jnp.exp(g[h] - m_new[h]) for h in hs]
    for h in hs:
        c_ref[h] = decay[h] * c_st[h] + _dot_tn(kw[h], v[h])
        n_ref[h] = decay[h] * n_st[h] + jnp.sum(kw[h], axis=0, keepdims=True)
        m_ref[h] = jnp.broadcast_to(m_new[h], m_ref.shape[1:])


def _mlstm_scan(z, lay, dims, gates_col, gates_row, bias_col, bias_row, heads, dqk, dv):
    n_rows = z.shape[0]
    n_chunks = (dims["ctx"] + dims["seq"]) // CHUNK
    md = heads * dv
    qw = heads * dqk
    cidx = lambda d, b, j: _chunk_index(d, b, j, dims)
    q_blk, k_blk, v_blk = lay["mq"] // qw, lay["mk"] // qw, lay["mv"] // md
    return pl.pallas_call(
        functools.partial(_mlstm_kernel, heads=heads, dqk=dqk, dv=dv),
        grid=(2, dims["batch"], n_chunks),
        in_specs=[pl.BlockSpec((CHUNK, qw), lambda d, b, j: (cidx(d, b, j), q_blk)),
                  pl.BlockSpec((CHUNK, qw), lambda d, b, j: (cidx(d, b, j), k_blk)),
                  pl.BlockSpec((CHUNK, md), lambda d, b, j: (cidx(d, b, j), v_blk)),
                  pl.BlockSpec((None, None, CHUNK, 2 * heads), lambda d, b, j: (d, cidx(d, b, j), 0, 0)),
                  pl.BlockSpec((None, None, 2 * heads, CHUNK), lambda d, b, j: (d, cidx(d, b, j), 0, 0)),
                  pl.BlockSpec((None, 1, 2 * heads), lambda d, b, j: (d, 0, 0)),
                  pl.BlockSpec((None, 2 * heads, 1), lambda d, b, j: (d, 0, 0))],
        out_specs=pl.BlockSpec((None, CHUNK, md), lambda d, b, j: (d, cidx(d, b, j), 0)),
        out_shape=jax.ShapeDtypeStruct((2, n_rows, md), F32),
        scratch_shapes=[pltpu.VMEM((heads, dqk, dv), F32), pltpu.VMEM((heads, 1, dqk), F32),
                        pltpu.VMEM((heads, 1, LANES), F32)],
        compiler_params=_params("parallel", "parallel", "arbitrary"), name="mlstm_scan",
    )(z, z, z, gates_col, gates_row, bias_col, bias_row)


def _even_finish_kernel(y_ref, bon_ref, gm_ref, gnw_ref, gnb_ref, h_ref, o_ref, nw_ref, out_ref,
                        *, rd, heads, dv):
    y = y_ref[0] + y_ref[1]
    inv_n = np.float32(1.0 / RWKV_HEAD)
    mean = _seg_sum(y, RWKV_HEAD) * inv_n
    yc = y - mean
    var = _seg_sum(yc * yc, RWKV_HEAD) * inv_n
    yn = yc * lax.rsqrt(var + GN_EPS) * gnw_ref[...] + gnb_ref[...] + bon_ref[...]
    out_ref[:, 0:rd] = (yn * gm_ref[...]).astype(out_ref.dtype)
    hm = h_ref[0] + h_ref[1]
    for h in range(heads):
        sl = slice(h * dv, (h + 1) * dv)
        x = hm[:, sl]
        xn = x * lax.rsqrt(jnp.mean(x * x, axis=-1, keepdims=True) + RMS_EPS) * nw_ref[:, sl]
        out_ref[:, rd + h * dv:rd + (h + 1) * dv] = (jax.nn.sigmoid(o_ref[:, sl]) * xn).astype(out_ref.dtype)


def _even_finish(y, bonus, gmul, gn_w, gn_b, hm, z, lay, ml_norm, n_rows, dims, heads, dv):
    rd = y.shape[2]
    md = hm.shape[2]
    tm = _tile(dims["ctx_rows"], 256, 8)
    o_blk = lay["mo"] // md
    return pl.pallas_call(
        functools.partial(_even_finish_kernel, rd=rd, heads=heads, dv=dv),
        grid=(n_rows // tm,),
        in_specs=[pl.BlockSpec((2, tm, rd), lambda i: (0, i, 0)),
                  pl.BlockSpec((tm, rd), lambda i: (i, 0)),
                  pl.BlockSpec((tm, rd), lambda i: (i, 0)),
                  pl.BlockSpec((1, rd), lambda i: (0, 0)),
                  pl.BlockSpec((1, rd), lambda i: (0, 0)),
                  pl.BlockSpec((2, tm, md), lambda i: (0, i, 0)),
                  pl.BlockSpec((tm, md), lambda i: (i, o_blk)),
                  pl.BlockSpec((1, md), lambda i: (0, 0))],
        out_specs=pl.BlockSpec((tm, rd + md), lambda i: (i, 0)),
        out_shape=jax.ShapeDtypeStruct((n_rows, rd + md), BF16),
        compiler_params=_params("parallel"), name="even_finish",
    )(y, bonus, gmul, gn_w.reshape(1, rd), gn_b.reshape(1, rd), hm, z, ml_norm.reshape(1, md))


def _attn_kernel(q_ref, kl_ref, kc_ref, vl_ref, vc_ref, o_ref, *, group, head):
    tq = q_ref.shape[0]
    q = jnp.concatenate([q_ref[:, g * head:(g + 1) * head] for g in range(group)], axis=0)
    ck = _tile(kl_ref.shape[0], 256, 8)
    keys = [kc_ref[...]] + [kl_ref[c * ck:(c + 1) * ck, :] for c in range(kl_ref.shape[0] // ck)]
    vals = [vc_ref[...]] + [vl_ref[c * ck:(c + 1) * ck, :] for c in range(kl_ref.shape[0] // ck)]
    m = acc = None
    for kc, vc in zip(keys, vals):
        s = _dot_nt(q, kc)
        row_max = jnp.max(s, axis=-1, keepdims=True)
        v_aug = jnp.concatenate([vc, jnp.ones_like(vc)], axis=1)
        if m is None:
            m = row_max
            acc = _dot(jnp.exp2(s - m), v_aug)
        else:
            m_new = jnp.maximum(m, row_max)
            acc = acc * jnp.exp2(m - m_new) + _dot(jnp.exp2(s - m_new), v_aug)
            m = m_new
    o = acc[:, :head] / acc[:, head:head + 1]
    for g in range(group):
        o_ref[:, g * head:(g + 1) * head] = o[g * tq:(g + 1) * tq].astype(o_ref.dtype)


def _attention(q, k, v, dims, head, group):
    seq, ctx, nb = dims["seq"], dims["ctx"], dims["batch"]
    kvh = k.shape[1] // head
    tq = _tile(seq, 256, 8)
    nq = seq // tq
    gw = group * head
    return pl.pallas_call(
        functools.partial(_attn_kernel, group=group, head=head),
        grid=(nb, kvh, nq),
        in_specs=[pl.BlockSpec((tq, gw), lambda b, h, i: (b * nq + i, h)),
                  pl.BlockSpec((seq, head), lambda b, h, i: (b, h)),
                  pl.BlockSpec((ctx, head), lambda b, h, i: (nb * (seq // ctx) + b, h)),
                  pl.BlockSpec((seq, head), lambda b, h, i: (b, h)),
                  pl.BlockSpec((ctx, head), lambda b, h, i: (nb * (seq // ctx) + b, h))],
        out_specs=pl.BlockSpec((tq, gw), lambda b, h, i: (b * nq + i, h)),
        out_shape=jax.ShapeDtypeStruct(q.shape, BF16),
        compiler_params=_params("parallel", "parallel", "parallel"), name="attention",
    )(q, k, k, v, v)


def _new_expert(i, be_ref):
    return jnp.logical_or(i == 0, be_ref[i] != be_ref[jnp.maximum(i - 1, 0)])


def _moe_up_kernel(be_ref, nu_ref, x_ref, w1_ref, w3_ref, mid_ref, w1_bf, w3_bf):
    i = pl.program_id(1)

    @pl.when(_new_expert(i, be_ref))
    def _():
        w1_bf[...] = w1_ref[...].astype(BF16)
        w3_bf[...] = w3_ref[...].astype(BF16)

    @pl.when(i < nu_ref[0])
    def _():
        x = _unpack_bf16_pairs(x_ref[...])
        a = jnp.dot(x, w1_bf[...], preferred_element_type=F32)
        b = jnp.dot(x, w3_bf[...], preferred_element_type=F32)
        mid_ref[...] = (a * jax.nn.sigmoid(a) * b).astype(mid_ref.dtype)

    @pl.when(i >= nu_ref[0])
    def _():
        mid_ref[...] = jnp.zeros_like(mid_ref)


def _moe_down_kernel(be_ref, nu_ref, mid_ref, w2_ref, o_ref, w2_bf):
    i = pl.program_id(1)

    @pl.when(_new_expert(i, be_ref))
    def _():
        w2_bf[...] = w2_ref[...].astype(BF16)

    @pl.when(i < nu_ref[0])
    def _():
        o_ref[...] = jnp.dot(mid_ref[...], w2_bf[...], preferred_element_type=F32)

    @pl.when(i >= nu_ref[0])
    def _():
        o_ref[...] = jnp.zeros_like(o_ref)


def _moe_ffn(xs, block_expert, n_used, w1, w3, w2, layer, bm):
    n_rows = xs.shape[0]
    d, de = w1.shape[2], w1.shape[3]
    tde = _tile(de, 512, LANES)
    tn = _tile(d, 4096, LANES)
    nblk = n_rows // bm
    mid = pl.pallas_call(
        _moe_up_kernel,
        grid_spec=pltpu.PrefetchScalarGridSpec(
            num_scalar_prefetch=2, grid=(de // tde, nblk),
            in_specs=[pl.BlockSpec((bm, d // 2), lambda k, i, be, nu: (i, 0)),
                      pl.BlockSpec((None, None, d, tde), lambda k, i, be, nu: (layer, be[i], 0, k)),
                      pl.BlockSpec((None, None, d, tde), lambda k, i, be, nu: (layer, be[i], 0, k))],
            out_specs=pl.BlockSpec((bm, tde), lambda k, i, be, nu: (i, k)),
            scratch_shapes=[pltpu.VMEM((d, tde), BF16), pltpu.VMEM((d, tde), BF16)]),
        out_shape=jax.ShapeDtypeStruct((n_rows, de), BF16),
        compiler_params=_params("arbitrary", "arbitrary"), name="moe_up",
    )(block_expert, n_used, xs, w1, w3)
    return pl.pallas_call(
        _moe_down_kernel,
        grid_spec=pltpu.PrefetchScalarGridSpec(
            num_scalar_prefetch=2, grid=(d // tn, nblk),
            in_specs=[pl.BlockSpec((bm, de), lambda n, i, be, nu: (i, 0)),
                      pl.BlockSpec((None, None, de, tn), lambda n, i, be, nu: (layer, be[i], 0, n))],
            out_specs=pl.BlockSpec((bm, tn), lambda n, i, be, nu: (i, n)),
            scratch_shapes=[pltpu.VMEM((de, tn), BF16)]),
        out_shape=jax.ShapeDtypeStruct((n_rows, d), F32),
        compiler_params=_params("arbitrary", "arbitrary"), name="moe_down",
    )(block_expert, n_used, mid, w2)


def _dispatch_plan(idx, n_experts, bm):
    n_tok = idx.shape[1]
    n_assign = n_tok * TOP_K
    expert = idx.T.reshape(-1)
    onehot = (expert[:, None] == jnp.arange(n_experts)[None, :]).astype(jnp.int32)
    rank = jnp.take_along_axis(jnp.cumsum(onehot, axis=0) - onehot, expert[:, None], 1)[:, 0]
    counts = jnp.sum(onehot, axis=0)
    padded = (counts + bm - 1) // bm * bm
    pad_end = jnp.cumsum(padded)
    pos = (pad_end - padded)[expert] + rank
    n_blocks = -(-n_assign // bm) + n_experts
    token = jnp.repeat(jnp.arange(n_tok, dtype=jnp.int32), TOP_K)
    filler = jnp.arange(n_blocks * bm, dtype=jnp.int32) % n_tok
    row_token = filler.at[pos].set(token)
    block_start = jnp.arange(n_blocks, dtype=jnp.int32) * bm
    block_expert = jnp.minimum(jnp.sum((pad_end[None, :] <= block_start[:, None]).astype(jnp.int32), axis=1),
                               n_experts - 1)
    n_used = (pad_end[-1] // bm).astype(jnp.int32).reshape(1)
    block_expert = jnp.where(jnp.arange(n_blocks) < n_used[0], block_expert,
                             block_expert[jnp.maximum(n_used[0] - 1, 0)])
    return pos.reshape(n_tok, TOP_K), row_token, block_expert, n_used


def _ffn_res_kernel(x_ref, f0_ref, f1_ref, g_ref, gt_ref, o_ref):
    g = g_ref[...]
    f = g[:, 0:1] * f0_ref[...] + g[:, 1:2] * f1_ref[...]
    o_ref[...] = x_ref[...] + gt_ref[...] * f


def _ffn_res_norm_kernel(x_ref, f0_ref, f1_ref, g_ref, gt_ref, nw_ref, o_ref):
    g = g_ref[...]
    f = g[:, 0:1] * f0_ref[...] + g[:, 1:2] * f1_ref[...]
    x = x_ref[...] + gt_ref[...] * f
    o_ref[...] = x * lax.rsqrt(jnp.mean(x * x, axis=-1, keepdims=True) + RMS_EPS) * nw_ref[...]


def _ffn_res_next_kernel(x_ref, f0_ref, f1_ref, g_ref, gt_ref, nw_ref, sh_ref, sc_ref, o_ref, h_ref):
    g = g_ref[...]
    f = g[:, 0:1] * f0_ref[...] + g[:, 1:2] * f1_ref[...]
    x = x_ref[...] + gt_ref[...] * f
    o_ref[...] = x
    y = x * lax.rsqrt(jnp.mean(x * x, axis=-1, keepdims=True) + RMS_EPS) * nw_ref[...]
    h_ref[...] = (y * (1.0 + sc_ref[...]) + sh_ref[...]).astype(h_ref.dtype)


def _ffn_residual(x, f0, f1, gate, mod3, gate_chunk, n_rows, dims, final_norm=None, next_norm=None):
    d = x.shape[1]
    tm = _tile(dims["ctx_rows"], 256, 8)
    grp = dims["group_of"](tm)
    row = pl.BlockSpec((tm, d), lambda i: (i, 0))
    in_specs = [row, row, row, pl.BlockSpec((tm, TOP_K), lambda i: (i, 0)),
                pl.BlockSpec((None, 1, d), lambda i: (grp(i), 0, gate_chunk))]
    args = [x, f0, f1, gate, mod3]
    kern = _ffn_res_kernel
    out_specs = row
    out_shape = jax.ShapeDtypeStruct((n_rows, d), F32)
    if final_norm is not None:
        in_specs.append(pl.BlockSpec((1, d), lambda i: (0, 0)))
        args.append(final_norm.reshape(1, d))
        kern = _ffn_res_norm_kernel
    elif next_norm is not None:
        norm_w, mod3_next = next_norm
        in_specs += [pl.BlockSpec((1, d), lambda i: (0, 0)),
                     pl.BlockSpec((None, 1, d), lambda i: (grp(i), 0, 0)),
                     pl.BlockSpec((None, 1, d), lambda i: (grp(i), 0, 1))]
        args += [norm_w.reshape(1, d), mod3_next, mod3_next]
        kern = _ffn_res_next_kernel
        out_specs = [row, row]
        out_shape = [out_shape, jax.ShapeDtypeStruct((n_rows, d), BF16)]
    return pl.pallas_call(
        kern, grid=(n_rows // tm,), in_specs=in_specs, out_specs=out_specs, out_shape=out_shape,
        compiler_params=_params("parallel"), name="ffn_residual",
    )(*args)


def _even_layout(rd, lw, la, lg, mh, dqk, md):
    lwp = _rup(lw + 4 * mh, LANES)
    assert _rup(la, LANES) == lwp
    lay = {"rd": rd, "lg": lg, "lwp": lwp}
    off = 0
    for name, width in (("mq", mh * dqk), ("mk", mh * dqk), ("mv", md), ("mo", md)):
        assert off % width == 0
        lay[name] = off
        off += width
    lay["width"] = off
    return lay


def _pad_cols(seg, width):
    return jnp.pad(seg, ((0, 0), (0, width - seg.shape[1])))


def _even_projection(h, w_in, n_rows, lay, lw, la, mh):
    rd, lg, lwp = lay["rd"], lay["lg"], lay["lwp"]
    o = 3 * rd
    rww = o + lw + la + lg
    m_end = rww + lay["width"]
    z_rkv = _proj(h, w_in, n_rows, F32, n_cols=o)
    tn = _tile(lay["width"], 512, LANES)
    shift, base, per_tile = rww % LANES, rww // LANES, tn // LANES
    if shift:
        plan = [(0, shift, LANES)] + [(q, 0, LANES) for q in range(1, per_tile)] + [(per_tile, 0, shift)]
    else:
        plan = [(q, 0, LANES) for q in range(per_tile)]
    slots = per_tile + (1 if shift else 0)
    z_m = _proj_cols(h, w_in, n_rows, lay["width"] // tn,
                     [lambda j, q=q: base + per_tile * j + q for q in range(slots)], plan)
    ids, plan_t = _column_pieces([(o + lw + la, rww), (o, o + lw), (m_end, m_end + 4 * mh), (o + lw, o + lw + la)],
                                 [lg, lw, lwp - lw, lwp])
    z_t = _proj_cols(h, w_in, n_rows, 1, [lambda j, b=b: b for b in ids], plan_t)
    return z_rkv, z_m, z_t


def _mix_vectors(mu, lay, lw, la):
    rd, lg, lwp = lay["rd"], lay["lg"], lay["lwp"]
    o = 3 * rd
    rww = o + lw + la + lg
    mu2 = mu.reshape(1, -1)
    mu_tail = jnp.concatenate([mu2[:, o + lw + la:rww], _pad_cols(mu2[:, o:o + lw], lwp),
                               _pad_cols(mu2[:, o + lw:o + lw + la], lwp)], axis=1)
    return mu2[:, :o], mu_tail


def _pad_rows(w, rows):
    return jnp.pad(w, ((0, 0), (0, rows - w.shape[1]), (0, 0)))


def _rope_tables(dims, head):
    seq, nb = dims["seq"], dims["batch"]
    pairs = head // 4
    rows = seq // GRID_W
    row = np.repeat(np.arange(rows), GRID_W).astype(np.float32)
    col = np.tile(np.arange(GRID_W), rows).astype(np.float32)
    inv = (np.float32(ROPE_THETA) ** (-np.arange(pairs, dtype=np.float32) / np.float32(pairs))).astype(np.float32)
    ang = np.concatenate([row[:, None] * inv, col[:, None] * inv], -1)
    cos, sin = np.cos(ang).astype(np.float32), np.sin(ang).astype(np.float32)
    cos_t = np.tile(np.concatenate([cos, cos], -1), (nb, 1))
    sin_t = np.tile(np.concatenate([-sin, sin], -1), (nb, 1))
    n_ctx = dims["ctx_rows"]
    return (jnp.asarray(np.concatenate([cos_t, np.ones((n_ctx, head), np.float32)], 0)),
            jnp.asarray(np.concatenate([sin_t, np.zeros((n_ctx, head), np.float32)], 0)))


def kernel(x, c, ctx, c_ctx, mod_w, mod_b, norm_mix, norm_ffn, norm_final, router_w, router_bias,
           exp_w1, exp_w3, exp_w2, ev_w_in, ev_w_out, rw_mu, rw_w0, rw_w_up, rw_a0, rw_a_up, rw_g_up,
           rw_k_k, rw_k_a, rw_r_k, rw_gn_w, rw_gn_b, ml_gate_b, ml_norm, at_w_qkv, at_q_norm, at_k_norm,
           at_w_o):
    nb, seq, d = x.shape
    n_ctx = ctx.shape[1]
    depth = mod_w.shape[0]
    lat_rows, ctx_rows = nb * seq, nb * n_ctx
    all_rows = lat_rows + ctx_rows
    assert nb + 1 <= MOD_ROWS and seq % n_ctx == 0 and n_ctx % CHUNK == 0
    dims = {"batch": nb, "seq": seq, "ctx": n_ctx, "lat_rows": lat_rows, "ctx_rows": ctx_rows,
            "group_of": lambda tm: (lambda i: jnp.minimum(i * tm // seq, nb))}

    n_experts = router_w.shape[1]
    rd = rw_w0.shape[-1]
    lw, la, lg = rw_w_up.shape[2], rw_a_up.shape[2], rw_g_up.shape[1]
    mh = ml_gate_b.shape[-1]
    md = ml_norm.shape[-1]
    dv = md // mh
    dqk = dv // 2
    head = at_q_norm.shape[-1]
    kv_dim = (at_w_qkv.shape[-1] - d) // 2
    group = (d // head) // (kv_dim // head)
    moe_bm = 256

    xa = (x.reshape(lat_rows, d), ctx.reshape(ctx_rows, d))
    cond = jnp.zeros((MOD_ROWS, d), F32).at[:nb].set(c).at[nb].set(c_ctx)
    lay = _even_layout(rd, lw, la, lg, mh, dqk, md)
    mods = _mod_tables(cond, mod_w, mod_b)
    w_in_bf16 = ev_w_in.astype(BF16)
    out = h_next = None

    for layer in range(depth):
        ctx_out = layer < depth - 1
        j = layer // 2
        rows_out = all_rows if ctx_out else lat_rows
        mod3 = mods[layer].reshape(MOD_ROWS, 1, 6 * d)

        h = h_next if h_next is not None else _norm_mod(xa, norm_mix[layer], mod3, 0, all_rows, dims)[0]
        if layer % 2 == 0:
            mu_main, mu_tail = _mix_vectors(rw_mu[j], lay, lw, la)
            z_rkv, z, z_t = _even_projection(h, (w_in_bf16, j), all_rows, lay, lw, la, mh)
            r, v, kkn, lwd, km, bb, bonus, gmul = _rwkv_prep(
                z_rkv, z_t, all_rows, lay, dims, mu_main, mu_tail, rw_w0[j], _pad_rows(rw_w_up[j], lay["lwp"]).astype(BF16),
                rw_a0[j], _pad_rows(rw_a_up[j], lay["lwp"]).astype(BF16), rw_g_up[j].astype(BF16),
                rw_k_k[j], rw_k_a[j], rw_r_k[j])
            y = _rwkv_scan(r, v, kkn, lwd, km, bb, dims, heads_per_step=min(32, rd // RWKV_HEAD))
            g_off = lay["lg"] + lw
            gates = z_t[:, g_off:g_off + 4 * mh].reshape(all_rows, 2, 2 * mh)
            gates_col = jnp.moveaxis(gates, 1, 0).reshape(2, all_rows // CHUNK, CHUNK, 2 * mh)
            gates_row = jnp.swapaxes(gates_col, 2, 3)
            bias = ml_gate_b[j].reshape(2, 2 * mh)
            hm = _mlstm_scan(z, lay, dims, gates_col, gates_row, bias.reshape(2, 1, 2 * mh),
                             bias.reshape(2, 2 * mh, 1), mh, dqk, dv)
            mix = _even_finish(y, bonus, gmul, rw_gn_w[j], rw_gn_b[j], hm, z, lay, ml_norm[j],
                               rows_out, dims, mh, dv)
            xa_new = _proj_residual(mix, (ev_w_out, j), xa, mod3, 2, rows_out, dims)
        else:
            if ctx_out:
                raise NotImplementedError("context output of an attention layer (depth > 2)")
            w_qkv = (at_w_qkv, j)
            cos_t, sin_t = _rope_tables(dims, head)
            q_scale = np.float32(head ** -0.5 * np.log2(np.e))
            q = _proj_qk(h, w_qkv, 0, d, at_q_norm[j], cos_t, sin_t, q_scale, lat_rows)
            k = _proj_qk(h, w_qkv, d, kv_dim, at_k_norm[j], cos_t, sin_t, np.float32(1.0), all_rows)
            v = _proj(h, w_qkv, all_rows, BF16, col_off=d + kv_dim, n_cols=kv_dim)
            att = _attention(q, k, v, dims, head, group)
            xa_new = _proj_residual(att, (at_w_o, j), xa, mod3, 2, lat_rows, dims)
        xa = xa_new

        h2, idx, gate_t = _norm_mod(xa, norm_ffn[layer], mod3, 3, rows_out, dims,
                                    router=(router_w, router_bias))
        gate = gate_t.T
        pos, row_token, block_expert, n_used = _dispatch_plan(idx, n_experts, moe_bm)
        xs = h2.at[row_token].get(mode="promise_in_bounds")
        ys = _moe_ffn(xs, block_expert, n_used, exp_w1, exp_w3, exp_w2, layer, moe_bm)
        f0 = ys.at[pos[:, 0]].get(mode="promise_in_bounds")
        f1 = ys.at[pos[:, 1]].get(mode="promise_in_bounds")
        last = layer == depth - 1
        if last:
            xa = _ffn_residual(xa, f0, f1, gate, mod3, 5, rows_out, dims, final_norm=norm_final)
            out = xa[:lat_rows].reshape(nb, seq, d)
        else:
            mod3_next = mods[layer + 1].reshape(MOD_ROWS, 1, 6 * d)
            xa, h_next = _ffn_residual(xa, f0, f1, gate, mod3, 5, rows_out, dims,
                                       next_norm=(norm_mix[layer + 1], mod3_next))
    return out
```
